```python
import jax
import jax.numpy as jnp
from jax import lax
import numpy as np

D_MODEL = 1024
BATCH = 16
SEQ = 4096
DEPTH = 2

POOL_WIDTH = D_MODEL // 4
POOL_GROUPS = 4
POOL_GROUP_WIDTH = POOL_WIDTH // POOL_GROUPS
POOL_WINDOWS = (2, 4, 8, 16)
LRU_WIDTH = D_MODEL // 4
LRU_BLOCKS = 4
LRU_BLOCK_WIDTH = LRU_WIDTH // LRU_BLOCKS
CONV_WIDTH = 4
LRU_C = 8.0
HEAD_DIM = 64
ATTN_WIDTH = D_MODEL - POOL_WIDTH - LRU_WIDTH
N_ATTN_HEADS = ATTN_WIDTH // HEAD_DIM
DILATED_PATTERNS = ((128, 1), (512, 4), (2048, 16))
ATTN_BLOCK = 128
ROPE_THETA = 10000.0
IN_WIDTH = POOL_WIDTH + 2 * LRU_WIDTH + 3 * ATTN_WIDTH
N_EXPERTS = 32
TOP_K = 4
EXPERT_FF = D_MODEL // 4
MOE_BLOCK = 512
SWIGLU_LIMIT = 7.0
SWIGLU_ALPHA = 1.702
NORM_EPS = 1e-6
NEG_INF = -1e30

kernel_name = 'hybrid_pool_rglru_dilated_attn_moe'


def _rms_norm(x, g):
    x32 = x.astype(jnp.float32)
    y = x32 * lax.rsqrt(jnp.mean(x32 * x32, axis=-1, keepdims=True) + NORM_EPS)
    return (y * g.astype(jnp.float32)).astype(x.dtype)


def _rope_tables(seq_len):
    pos = jnp.arange(seq_len, dtype=jnp.float32)
    inv_freq = ROPE_THETA ** (-jnp.arange(0, HEAD_DIM, 2, dtype=jnp.float32) / HEAD_DIM)
    ang = pos[:, None] * inv_freq[None, :]
    return jnp.cos(ang)[None, :, None, :], jnp.sin(ang)[None, :, None, :]


def _rope(x, cos, sin):
    half = HEAD_DIM // 2
    x1, x2 = x[..., :half], x[..., half:]
    return jnp.concatenate([x1 * cos - x2 * sin, x2 * cos + x1 * sin], axis=-1)


def _pool_mixer(u, pool_w, pool_scale):
    b, s, _ = u.shape
    u32 = u.astype(jnp.float32).reshape(b, s, POOL_GROUPS, POOL_GROUP_WIDTH)
    csum = lax.cumsum(u32, axis=1)
    count = jnp.arange(1, s + 1, dtype=jnp.float32)[None, :, None]
    means = []
    for g, w in enumerate(POOL_WINDOWS):
        cg = csum[:, :, g]
        lagged = jnp.pad(cg, ((0, 0), (w, 0), (0, 0)))[:, :s]
        means.append((cg - lagged) / jnp.minimum(count, float(w)))
    pooled = jnp.stack(means, axis=2) - u32
    y = jnp.einsum('bsgc,gce->bsge', pooled, pool_w.astype(jnp.float32)).reshape(b, s, POOL_WIDTH)
    return y * pool_scale.astype(jnp.float32)


def _lru_combine(left, right):
    a1, b1 = left
    a2, b2 = right
    return a1 * a2, a2 * b1 + b2


def _rglru_mixer(u, gate_in, conv_w, conv_b, wa, ba, wx, bx, lam):
    b, s, _ = u.shape
    up = jnp.pad(u, ((0, 0), (CONV_WIDTH - 1, 0), (0, 0)))
    xc = conv_b.astype(jnp.float32)
    for j in range(CONV_WIDTH):
        xc = xc + conv_w[j].astype(jnp.float32) * up[:, j:j + s].astype(jnp.float32)
    xblk = xc.reshape(b, s, LRU_BLOCKS, LRU_BLOCK_WIDTH)
    r = jax.nn.sigmoid(jnp.einsum('bsnc,nce->bsne', xblk, wa.astype(jnp.float32)).reshape(b, s, LRU_WIDTH) + ba)
    i = jax.nn.sigmoid(jnp.einsum('bsnc,nce->bsne', xblk, wx.astype(jnp.float32)).reshape(b, s, LRU_WIDTH) + bx)
    log_a = -LRU_C * r * jax.nn.softplus(-lam.astype(jnp.float32))
    a = jnp.exp(log_a)
    mult = jnp.sqrt(-jnp.expm1(2.0 * log_a))
    mult = mult.at[:, 0].set(1.0)
    _, h = lax.associative_scan(_lru_combine, (a, mult * (i * xc)), axis=1)
    return h * jax.nn.gelu(gate_in.astype(jnp.float32))


def _dilated_branch(q, k, v, window, dilation):
    b, s, h, dh = q.shape
    span = window // dilation
    sub = s // dilation
    n_blk = -(-sub // ATTN_BLOCK)
    pad = n_blk * ATTN_BLOCK - sub

    def to_blocks(t):
        t = t.reshape(b, sub, dilation, h, dh)
        t = jnp.pad(t, ((0, 0), (0, pad), (0, 0), (0, 0), (0, 0)))
        return t.reshape(b, n_blk, ATTN_BLOCK, dilation, h, dh)

    def with_previous(t):
        prev = jnp.pad(t, ((0, 0), (1, 0), (0, 0), (0, 0), (0, 0), (0, 0)))[:, :n_blk]
        return jnp.concatenate([prev, t], axis=2)

    qb = to_blocks(q)
    kb = with_previous(to_blocks(k))
    vb = with_previous(to_blocks(v))
    scores = jnp.einsum('bnqrhd,bnkrhd->bnrhqk', qb, kb) * (HEAD_DIM ** -0.5)
    qi = jnp.arange(ATTN_BLOCK)[:, None]
    kj = jnp.arange(2 * ATTN_BLOCK)[None, :]
    dist = qi + ATTN_BLOCK - kj
    band = (dist >= 0) & (dist <= span)
    valid = band[None] & ((jnp.arange(n_blk)[:, None, None] > 0) | (kj[None] >= ATTN_BLOCK))
    scores = jnp.where(valid[None, :, None, None], scores, NEG_INF)
    row_max = jnp.max(scores, axis=-1)
    p = jnp.exp(scores - row_max[..., None])
    den = jnp.sum(p, axis=-1)
    num = jnp.einsum('bnrhqk,bnkrhd->bnqrhd', p, vb)

    def from_blocks(t):
        t1 = t.reshape((b, n_blk * ATTN_BLOCK) + t.shape[3:])[:, :sub]
        return t1.reshape((b, s) + t1.shape[3:])

    return (from_blocks(num),
            from_blocks(den.transpose(0, 1, 4, 2, 3)),
            from_blocks(row_max.transpose(0, 1, 4, 2, 3)))


def _attention_mixer(q, k, v, q_norm_g, k_norm_g, cos, sin):
    b, s, _ = q.shape
    q = _rope(_rms_norm(q.reshape(b, s, N_ATTN_HEADS, HEAD_DIM), q_norm_g).astype(jnp.float32), cos, sin)
    k = _rope(_rms_norm(k.reshape(b, s, N_ATTN_HEADS, HEAD_DIM), k_norm_g).astype(jnp.float32), cos, sin)
    v = v.reshape(b, s, N_ATTN_HEADS, HEAD_DIM).astype(jnp.float32)
    branches = [_dilated_branch(q, k, v, w, d) for (w, d) in DILATED_PATTERNS]
    m_ref = branches[0][2]
    for br in branches[1:]:
        m_ref = jnp.maximum(m_ref, br[2])
    num = jnp.zeros_like(q)
    den = jnp.zeros_like(m_ref)
    for n_g, d_g, m_g in branches:
        wgt = jnp.exp(m_g - m_ref)
        num = num + wgt[..., None] * n_g
        den = den + wgt * d_g
    return (num / den[..., None]).reshape(b, s, ATTN_WIDTH)


def _moe(h, router_w, router_b, w_gate_up, b_gate_up, w_down, b_down):
    b, s, d = h.shape
    n_tok = b * s
    n_assign = n_tok * TOP_K
    n_blocks = -(-n_assign // MOE_BLOCK) + N_EXPERTS
    t = h.reshape(n_tok, d)
    logits = t.astype(jnp.float32) @ router_w.astype(jnp.float32) + router_b.astype(jnp.float32)
    top_val, top_idx = lax.top_k(logits, TOP_K)
    gates = jax.nn.softmax(top_val, axis=-1)
    flat_e = top_idx.reshape(-1).astype(jnp.int32)
    order = jnp.argsort(flat_e)
    e_sorted = flat_e[order]
    sizes = jnp.bincount(flat_e, length=N_EXPERTS).astype(jnp.int32)
    start = jnp.cumsum(sizes) - sizes
    pad_sizes = (sizes + MOE_BLOCK - 1) // MOE_BLOCK * MOE_BLOCK
    pad_end = jnp.cumsum(pad_sizes)
    pad_start = pad_end - pad_sizes
    dest_sorted = pad_start[e_sorted] + jnp.arange(n_assign, dtype=jnp.int32) - start[e_sorted]
    dest = jnp.zeros((n_assign,), jnp.int32).at[order].set(dest_sorted.astype(jnp.int32))
    row_tok = jnp.zeros((n_blocks * MOE_BLOCK,), jnp.int32).at[dest].set(
        jnp.arange(n_assign, dtype=jnp.int32) // TOP_K)
    block_e = jnp.minimum(jnp.searchsorted(pad_end, jnp.arange(n_blocks, dtype=jnp.int32) * MOE_BLOCK,
                                           side='right'), N_EXPERTS - 1)
    xb = t[row_tok].reshape(n_blocks, MOE_BLOCK, d)
    gu = jnp.einsum('nrd,ndf->nrf', xb, w_gate_up[block_e]) + b_gate_up[block_e][:, None, :]
    gate = jnp.minimum(gu[..., :EXPERT_FF], SWIGLU_LIMIT)
    up = jnp.clip(gu[..., EXPERT_FF:], -SWIGLU_LIMIT, SWIGLU_LIMIT)
    act = (up + 1.0) * gate * jax.nn.sigmoid(SWIGLU_ALPHA * gate)
    y = jnp.einsum('nrf,nfd->nrd', act, w_down[block_e]) + b_down[block_e][:, None, :]
    y_assign = y.reshape(n_blocks * MOE_BLOCK, d)[dest].reshape(n_tok, TOP_K, d)
    out = jnp.einsum('tkd,tk->td', y_assign, gates.astype(y.dtype))
    return out.reshape(b, s, d)


def setup_inputs(seed: int = 0) -> dict:
    key = jax.random.key(seed)
    ks = jax.random.split(key, 24)
    f32 = jnp.float32

    def nrm(k, shape, scale):
        return jax.random.normal(k, shape, f32) * scale

    lam_u = jax.random.uniform(ks[11], (DEPTH, LRU_WIDTH), f32, 0.9, 0.999)
    lam_s = lam_u ** (1.0 / LRU_C)
    return {
        'x': nrm(ks[0], (BATCH, SEQ, D_MODEL), 1.0),
        'norm1_g': 1.0 + nrm(ks[1], (DEPTH, D_MODEL), 0.02),
        'w_in': nrm(ks[2], (DEPTH, D_MODEL, IN_WIDTH), D_MODEL ** -0.5),
        'pool_w': nrm(ks[3], (DEPTH, POOL_GROUPS, POOL_GROUP_WIDTH, POOL_GROUP_WIDTH), POOL_GROUP_WIDTH ** -0.5),
        'pool_scale': 1.0 + nrm(ks[4], (DEPTH, POOL_WIDTH), 0.02),
        'conv_w': nrm(ks[5], (DEPTH, CONV_WIDTH, LRU_WIDTH), CONV_WIDTH ** -0.5),
        'conv_b': nrm(ks[6], (DEPTH, LRU_WIDTH), 0.01),
        'lru_wa': nrm(ks[7], (DEPTH, LRU_BLOCKS, LRU_BLOCK_WIDTH, LRU_BLOCK_WIDTH), LRU_BLOCK_WIDTH ** -0.5),
        'lru_ba': nrm(ks[8], (DEPTH, LRU_WIDTH), 0.01),
        'lru_wx': nrm(ks[9], (DEPTH, LRU_BLOCKS, LRU_BLOCK_WIDTH, LRU_BLOCK_WIDTH), LRU_BLOCK_WIDTH ** -0.5),
        'lru_bx': nrm(ks[10], (DEPTH, LRU_WIDTH), 0.01),
        'lru_lambda': jnp.log(lam_s) - jnp.log1p(-lam_s),
        'q_norm_g': 1.0 + nrm(ks[12], (DEPTH, HEAD_DIM), 0.02),
        'k_norm_g': 1.0 + nrm(ks[13], (DEPTH, HEAD_DIM), 0.02),
        'out_norm_g': 1.0 + nrm(ks[14], (DEPTH, D_MODEL), 0.02),
        'w_out': nrm(ks[15], (DEPTH, D_MODEL, D_MODEL), D_MODEL ** -0.5),
        'norm2_g': 1.0 + nrm(ks[16], (DEPTH, D_MODEL), 0.02),
        'router_w': nrm(ks[17], (DEPTH, D_MODEL, N_EXPERTS), D_MODEL ** -0.5),
        'router_b': nrm(ks[18], (DEPTH, N_EXPERTS), 0.01),
        'w_gate_up': nrm(ks[19], (DEPTH, N_EXPERTS, D_MODEL, 2 * EXPERT_FF), D_MODEL ** -0.5),
        'b_gate_up': nrm(ks[20], (DEPTH, N_EXPERTS, 2 * EXPERT_FF), 0.01),
        'w_down': nrm(ks[21], (DEPTH, N_EXPERTS, EXPERT_FF, D_MODEL), EXPERT_FF ** -0.5),
        'b_down': nrm(ks[22], (DEPTH, N_EXPERTS, D_MODEL), 0.01),
    }


def reference(x, norm1_g, w_in, pool_w, pool_scale, conv_w, conv_b, lru_wa, lru_ba, lru_wx, lru_bx,
              lru_lambda, q_norm_g, k_norm_g, out_norm_g, w_out, norm2_g, router_w, router_b,
              w_gate_up, b_gate_up, w_down, b_down):
    cos, sin = _rope_tables(x.shape[1])
    c0 = POOL_WIDTH
    c1 = c0 + LRU_WIDTH
    c2 = c1 + LRU_WIDTH
    c3 = c2 + ATTN_WIDTH
    c4 = c3 + ATTN_WIDTH
    a_end = POOL_WIDTH
    b_end = POOL_WIDTH + LRU_WIDTH
    for l in range(DEPTH):
        h = _rms_norm(x, norm1_g[l])
        proj = h @ w_in[l]
        y_pool = _pool_mixer(proj[..., :c0], pool_w[l], pool_scale[l])
        y_lru = _rglru_mixer(proj[..., c0:c1], proj[..., c1:c2], conv_w[l], conv_b[l],
                             lru_wa[l], lru_ba[l], lru_wx[l], lru_bx[l], lru_lambda[l])
        y_attn = _attention_mixer(proj[..., c2:c3], proj[..., c3:c4], proj[..., c4:],
                                  q_norm_g[l], k_norm_g[l], cos, sin)
        og = out_norm_g[l]
        mix = jnp.concatenate([_rms_norm(y_pool, og[:a_end]),
                               _rms_norm(y_lru, og[a_end:b_end]),
                               _rms_norm(y_attn, og[b_end:])], axis=-1).astype(x.dtype)
        x = x + mix @ w_out[l]
        x = x + _moe(_rms_norm(x, norm2_g[l]), router_w[l], router_b[l],
                     w_gate_up[l], b_gate_up[l], w_down[l], b_down[l])
    return x
```

```python
import functools

import jax
import jax.numpy as jnp
from jax import lax
from jax.experimental import pallas as pl
from jax.experimental.pallas import tpu as pltpu
from jax.scipy.linalg import block_diag

f32 = jnp.float32
bf16 = jnp.bfloat16
i32 = jnp.int32

D_MODEL = 1024
POOL_WIDTH = 256
POOL_WINDOWS = (2, 4, 8, 16)
LRU_WIDTH = 256
LRU_C = 8.0
ATTN_WIDTH = 512
HEAD_DIM = 64
DILATIONS = (1, 4, 16)
ATTN_BLOCK = 128
ROPE_THETA = 10000.0
N_EXPERTS = 32
TOP_K = 4
EXPERT_FF = 256
SWIGLU_LIMIT = 7.0
SWIGLU_ALPHA = 1.702
NORM_EPS = 1e-6
NEG_INF = -1e30

LANES = 128
POOL_HALO = 16
CONV_HALO = 8
VMEM_LIMIT = 56 * 1024 * 1024

PROJ_TILE = 512
MIX_TILE = 512
MOE_TILE = 1024
MOE_CHUNK = 128


def _rms(x, g):
    return x * lax.rsqrt(jnp.mean(x * x, axis=-1, keepdims=True) + NORM_EPS) * g


_IN_SPLITS = (0, 256, 512, 768, 1280, 1792, 2304)


def _in_proj_kernel(x_ref, g_ref, w_ref, *out_refs):
    h = _rms(x_ref[...], g_ref[...]).astype(bf16)
    for o_ref, c0, c1 in zip(out_refs, _IN_SPLITS[:-1], _IN_SPLITS[1:]):
        o_ref[...] = jnp.dot(h, w_ref[:, c0:c1], preferred_element_type=f32)


def _in_proj(x, g, w):
    t = x.shape[0]
    tm = PROJ_TILE
    widths = [c1 - c0 for c0, c1 in zip(_IN_SPLITS[:-1], _IN_SPLITS[1:])]
    return pl.pallas_call(
        _in_proj_kernel,
        grid=(t // tm,),
        in_specs=[pl.BlockSpec((tm, D_MODEL), lambda i: (i, 0)),
                  pl.BlockSpec((1, D_MODEL), lambda i: (0, 0)),
                  pl.BlockSpec((D_MODEL, _IN_SPLITS[-1]), lambda i: (0, 0))],
        out_specs=[pl.BlockSpec((tm, w_), lambda i: (i, 0)) for w_ in widths],
        out_shape=[jax.ShapeDtypeStruct((t, w_), f32) for w_ in widths],
        compiler_params=pltpu.CompilerParams(dimension_semantics=("arbitrary",),
                                             vmem_limit_bytes=VMEM_LIMIT),
        name="in_proj",
    )(x, g, w)


def _mixer_kernel(pool_ref, lru_ref, gate_ref, pw_ref, ps_ref, cw_ref, cb_ref, wri_ref, bri_ref, lam_ref,
                  ypool_ref, ylru_ref,
                  b0, b1, b2, b3, cbuf, a_s, b_s, h_s, a2_s, b2_s, e2_s, e3_s, hc, *, ts):
    j = pl.program_id(1)
    n1 = ts // 8
    n2 = n1 // 8

    @pl.when(j == 0)
    def _():
        for buf in (b0, b1, b2, b3):
            buf[pl.ds(0, POOL_HALO), :] = jnp.zeros((POOL_HALO, POOL_WIDTH), f32)
        cbuf[pl.ds(0, CONV_HALO), :] = jnp.zeros((CONV_HALO, LRU_WIDTH), f32)
        hc[...] = jnp.zeros_like(hc)

    row = lax.broadcasted_iota(i32, (ts, POOL_WIDTH), 0) + j * ts
    grp = lax.broadcasted_iota(i32, (ts, POOL_WIDTH), 1) // (POOL_WIDTH // len(POOL_WINDOWS))

    u = pool_ref[...]
    b0[pl.ds(POOL_HALO, ts), :] = u
    s1 = u + b0[pl.ds(POOL_HALO - 1, ts), :]
    b1[pl.ds(POOL_HALO, ts), :] = s1
    s2 = s1 + b1[pl.ds(POOL_HALO - 2, ts), :]
    b2[pl.ds(POOL_HALO, ts), :] = s2
    s3 = s2 + b2[pl.ds(POOL_HALO - 4, ts), :]
    b3[pl.ds(POOL_HALO, ts), :] = s3
    s4 = s3 + b3[pl.ds(POOL_HALO - 8, ts), :]
    for buf in (b0, b1, b2, b3):
        buf[pl.ds(0, POOL_HALO), :] = buf[pl.ds(ts, POOL_HALO), :]
    sums = jnp.where(grp == 0, s1, jnp.where(grp == 1, s2, jnp.where(grp == 2, s3, s4)))
    win = jnp.where(grp == 0, float(POOL_WINDOWS[0]),
                    jnp.where(grp == 1, float(POOL_WINDOWS[1]),
                              jnp.where(grp == 2, float(POOL_WINDOWS[2]), float(POOL_WINDOWS[3]))))
    cnt = jnp.minimum((row + 1).astype(f32), win)
    pooled = sums / cnt - u
    ypool_ref[...] = jnp.dot(pooled.astype(bf16), pw_ref[...], preferred_element_type=f32) * ps_ref[...]

    lu = lru_ref[...]
    cbuf[pl.ds(CONV_HALO, ts), :] = lu
    xc = (cb_ref[...] + cw_ref[3:4, :] * lu
          + cw_ref[2:3, :] * cbuf[pl.ds(CONV_HALO - 1, ts), :]
          + cw_ref[1:2, :] * cbuf[pl.ds(CONV_HALO - 2, ts), :]
          + cw_ref[0:1, :] * cbuf[pl.ds(CONV_HALO - 3, ts), :])
    cbuf[pl.ds(0, CONV_HALO), :] = cbuf[pl.ds(ts, CONV_HALO), :]
    ri = jnp.dot(xc.astype(bf16), wri_ref[...], preferred_element_type=f32) + bri_ref[...]
    r = jax.nn.sigmoid(ri[:, :LRU_WIDTH])
    ig = jax.nn.sigmoid(ri[:, LRU_WIDTH:])
    log_a = -LRU_C * r * jax.nn.softplus(-lam_ref[...])
    a = jnp.exp(log_a)
    th = jnp.tanh(log_a)
    mult = jnp.sqrt(-2.0 * th / (1.0 - th))
    mult = jnp.where(row == 0, 1.0, mult)
    bb = mult * (ig * xc)

    def rows(ref, hf, s, n):
        return ref[hf, pl.ds(s, n, stride=8), :]

    for hf in range(LRU_WIDTH // LANES):
        a_s[hf] = a[:, hf * LANES:(hf + 1) * LANES]
        b_s[hf] = bb[:, hf * LANES:(hf + 1) * LANES]
        pa, pb = rows(a_s, hf, 0, n1), rows(b_s, hf, 0, n1)
        for s in range(1, 8):
            as_ = rows(a_s, hf, s, n1)
            pb = as_ * pb + rows(b_s, hf, s, n1)
            pa = as_ * pa
        a2_s[hf] = pa
        b2_s[hf] = pb
        pa, pb = rows(a2_s, hf, 0, n2), rows(b2_s, hf, 0, n2)
        for s in range(1, 8):
            as_ = rows(a2_s, hf, s, n2)
            pb = as_ * pb + rows(b2_s, hf, s, n2)
            pa = as_ * pa
        h = hc[hf, 0:1, :]
        for k in range(8):
            e3_s[hf, k:k + 1, :] = h
            h = pa[k:k + 1, :] * h + pb[k:k + 1, :]
        hc[hf, 0:1, :] = h
        prev = e3_s[hf]
        for s in range(8):
            e2_s[hf, pl.ds(s, n2, stride=8), :] = prev
            prev = rows(a2_s, hf, s, n2) * prev + rows(b2_s, hf, s, n2)
        prev = e2_s[hf]
        for s in range(8):
            prev = rows(a_s, hf, s, n1) * prev + rows(b_s, hf, s, n1)
            h_s[hf, pl.ds(s, n1, stride=8), :] = prev

    hfull = jnp.concatenate([h_s[hf] for hf in range(LRU_WIDTH // LANES)], axis=1)
    ylru_ref[...] = hfull * jax.nn.gelu(gate_ref[...])


def _mixer(pool_u, lru_u, gate, pw, ps, cw, cb, wri, bri, lam, batch, seq):
    ts = MIX_TILE
    nt = seq // ts
    t = batch * seq
    nh = LRU_WIDTH // LANES
    tile = lambda w_: pl.BlockSpec((ts, w_), lambda b, j: (b * nt + j, 0))
    full = lambda shape: pl.BlockSpec(shape, lambda b, j: (0,) * len(shape))
    return pl.pallas_call(
        functools.partial(_mixer_kernel, ts=ts),
        grid=(batch, nt),
        in_specs=[tile(POOL_WIDTH), tile(LRU_WIDTH), tile(LRU_WIDTH),
                  full((POOL_WIDTH, POOL_WIDTH)), full((1, POOL_WIDTH)),
                  full((4, LRU_WIDTH)), full((1, LRU_WIDTH)),
                  full((LRU_WIDTH, 2 * LRU_WIDTH)), full((1, 2 * LRU_WIDTH)), full((1, LRU_WIDTH))],
        out_specs=[tile(POOL_WIDTH), tile(LRU_WIDTH)],
        out_shape=[jax.ShapeDtypeStruct((t, POOL_WIDTH), f32), jax.ShapeDtypeStruct((t, LRU_WIDTH), f32)],
        scratch_shapes=[pltpu.VMEM((ts + POOL_HALO, POOL_WIDTH), f32)] * 4
        + [pltpu.VMEM((ts + CONV_HALO, LRU_WIDTH), f32)]
        + [pltpu.VMEM((nh, ts, LANES), f32)] * 3
        + [pltpu.VMEM((nh, ts // 8, LANES), f32)] * 3
        + [pltpu.VMEM((nh, 8, LANES), f32)] * 2,
        compiler_params=pltpu.CompilerParams(dimension_semantics=("arbitrary", "arbitrary"),
                                             vmem_limit_bytes=VMEM_LIMIT),
        name="pool_lru_mixer",
    )(pool_u, lru_u, gate, pw, ps, cw, cb, wri, bri, lam)


def _attn_kernel(q_ref, k_ref, v_ref, cos_ref, sin_ref, qg_ref, kg_ref, o_ref,
                 qs, ks, m_s, l_s, acc_s, *, seq):
    blk = ATTN_BLOCK
    prep_rows = 256
    lane = lax.broadcasted_iota(i32, (prep_rows, LANES), 1)
    head0 = lane < HEAD_DIM
    first_half = (lane % HEAD_DIM) < (HEAD_DIM // 2)

    def prep(c, carry):
        sl = pl.ds(pl.multiple_of(c * prep_rows, prep_rows), prep_rows)
        cs = cos_ref[sl, :]
        sn = sin_ref[sl, :]
        for src, dst, g_ref, scale in ((q_ref, qs, qg_ref, HEAD_DIM ** -0.5), (k_ref, ks, kg_ref, 1.0)):
            x = src[sl, :]
            xx = x * x
            ms0 = jnp.sum(jnp.where(head0, xx, 0.0), axis=-1, keepdims=True)
            ms1 = jnp.sum(jnp.where(head0, 0.0, xx), axis=-1, keepdims=True)
            ms = jnp.where(head0, ms0, ms1) * (1.0 / HEAD_DIM)
            xn = x * lax.rsqrt(ms + NORM_EPS) * g_ref[...]
            partner = jnp.where(first_half,
                                pltpu.roll(xn, LANES - HEAD_DIM // 2, 1),
                                pltpu.roll(xn, HEAD_DIM // 2, 1))
            dst[sl, :] = (xn * cs + partner * sn) * scale
        return carry

    lax.fori_loop(0, seq // prep_rows, prep, 0)

    qi = lax.broadcasted_iota(i32, (blk, 2 * blk), 0)
    kj = lax.broadcasted_iota(i32, (blk, 2 * blk), 1)
    mask_cur = (kj >= blk) & (kj - blk <= qi)
    mask_prev = (kj < blk) & (kj >= qi)
    lane_b = lax.broadcasted_iota(i32, (blk, LANES), 1)
    h0 = lane_b < HEAD_DIM

    for bi, d in enumerate(DILATIONS):
        nb = seq // (blk * d)

        def ld(ref, start, d=d):
            if d == 1:
                return ref[pl.ds(pl.multiple_of(start, blk), blk), :]
            return ref[pl.ds(start, blk, stride=d), :]

        def st(ref, start, val, d=d):
            if d == 1:
                ref[pl.ds(pl.multiple_of(start, blk), blk), :] = val
            else:
                ref[pl.ds(start, blk, stride=d), :] = val

        def body(idx, carry, d=d, nb=nb, bi=bi, ld=ld, st=st):
            res = idx // nb
            n = idx % nb
            q_start = n * (blk * d) + res
            p_start = jnp.maximum(n - 1, 0) * (blk * d) + res
            qb = ld(qs, q_start)
            kb = jnp.concatenate([ld(ks, p_start), ld(ks, q_start)], axis=0).astype(bf16)
            vb = jnp.concatenate([ld(v_ref, p_start), ld(v_ref, q_start)], axis=0).astype(bf16)
            valid = mask_cur | (mask_prev & (n > 0))
            ms, ls, nums = [], [], []
            for hh in range(2):
                qh = jnp.where(h0 if hh == 0 else ~h0, qb, 0.0).astype(bf16)
                s = lax.dot_general(qh, kb, (((1,), (1,)), ((), ())), preferred_element_type=f32)
                s = jnp.where(valid, s, NEG_INF)
                mb = jnp.max(s, axis=-1, keepdims=True)
                p = jnp.exp(s - mb)
                ls.append(jnp.sum(p, axis=-1, keepdims=True))
                ms.append(mb)
                nums.append(jnp.dot(p.astype(bf16), vb, preferred_element_type=f32))
            m_blk = jnp.where(h0, ms[0], ms[1])
            l_blk = jnp.where(h0, ls[0], ls[1])
            n_blk = jnp.where(h0, nums[0], nums[1])
            if bi == 0:
                st(m_s, q_start, m_blk)
                st(l_s, q_start, l_blk)
                st(acc_s, q_start, n_blk)
            else:
                m_old = ld(m_s, q_start)
                m_new = jnp.maximum(m_old, m_blk)
                w_old = jnp.exp(m_old - m_new)
                w_blk = jnp.exp(m_blk - m_new)
                st(m_s, q_start, m_new)
                st(l_s, q_start, w_old * ld(l_s, q_start) + w_blk * l_blk)
                st(acc_s, q_start, w_old * ld(acc_s, q_start) + w_blk * n_blk)
            return carry

        lax.fori_loop(0, seq // blk, body, 0)

    def fin(c, carry):
        sl = pl.ds(pl.multiple_of(c * prep_rows, prep_rows), prep_rows)
        o_ref[sl, :] = acc_s[sl, :] / l_s[sl, :]
        return carry

    lax.fori_loop(0, seq // prep_rows, fin, 0)


def _attention(q, k, v, cos_t, sin_t, qg, kg, batch, seq):
    t = batch * seq
    npair = ATTN_WIDTH // LANES
    tile = pl.BlockSpec((seq, LANES), lambda b, p: (b, p))
    table = pl.BlockSpec((seq, LANES), lambda b, p: (0, 0))
    gain = pl.BlockSpec((1, LANES), lambda b, p: (0, 0))
    return pl.pallas_call(
        functools.partial(_attn_kernel, seq=seq),
        grid=(batch, npair),
        in_specs=[tile, tile, tile, table, table, gain, gain],
        out_specs=tile,
        out_shape=jax.ShapeDtypeStruct((t, ATTN_WIDTH), f32),
        scratch_shapes=[pltpu.VMEM((seq, LANES), f32)] * 5,
        compiler_params=pltpu.CompilerParams(dimension_semantics=("arbitrary", "arbitrary"),
                                             vmem_limit_bytes=VMEM_LIMIT),
        name="dilated_attention",
    )(q, k, v, cos_t, sin_t, qg, kg)


def _out_proj_kernel(x_ref, yp_ref, yl_ref, ya_ref, og_ref, wo_ref, g2_ref, rw_ref, rb_ref,
                     xo_ref, idx_ref, gate_ref, rank_ref, cnt_ref, cnt_s, *, tm, steps_per_tile):
    i = pl.program_id(0)
    c0, c1 = POOL_WIDTH, POOL_WIDTH + LRU_WIDTH
    acc = x_ref[...]
    for y_ref, lo, hi in ((yp_ref, 0, c0), (yl_ref, c0, c1), (ya_ref, c1, D_MODEL)):
        mix = _rms(y_ref[...], og_ref[:, lo:hi]).astype(bf16)
        acc = acc + jnp.dot(mix, wo_ref[lo:hi, :], preferred_element_type=f32)
    xo_ref[...] = acc

    h2 = _rms(acc, g2_ref[...])
    logits = lax.dot_general(rw_ref[...], h2, (((1,), (1,)), ((), ())), precision=lax.Precision.HIGHEST,
                             preferred_element_type=f32) + rb_ref[...]
    eio = lax.broadcasted_iota(i32, (N_EXPERTS, tm), 0)
    cur = logits
    vals, idxs = [], []
    for _ in range(TOP_K):
        mx = jnp.max(cur, axis=0, keepdims=True)
        ix = jnp.min(jnp.where(cur == mx, eio, N_EXPERTS), axis=0, keepdims=True)
        vals.append(mx)
        idxs.append(ix)
        cur = jnp.where(eio == ix, -jnp.inf, cur)
    ex = [jnp.exp(v - vals[0]) for v in vals]
    den = ex[0] + ex[1] + ex[2] + ex[3]

    @pl.when(i % steps_per_tile == 0)
    def _():
        cnt_s[...] = jnp.zeros_like(cnt_s)

    onehot = jnp.zeros((N_EXPERTS, tm), f32)
    for ix in idxs:
        onehot = onehot + (eio == ix).astype(f32)
    before = (lax.broadcasted_iota(i32, (tm, tm), 0) < lax.broadcasted_iota(i32, (tm, tm), 1)).astype(bf16)
    prefix = jnp.dot(onehot.astype(bf16), before, preferred_element_type=f32) + cnt_s[:, 0:1]
    for k in range(TOP_K):
        idx_ref[k:k + 1, :] = idxs[k]
        gate_ref[k:k + 1, :] = ex[k] / den
        rank_ref[k:k + 1, :] = jnp.sum(jnp.where(eio == idxs[k], prefix, 0.0), axis=0, keepdims=True).astype(i32)
    cnt_s[...] = cnt_s[...] + jnp.sum(onehot, axis=1, keepdims=True)
    cnt_ref[0] = cnt_s[...]


def _out_proj(x, yp, yl, ya, og, wo, g2, rw_t, rb):
    t = x.shape[0]
    tm = PROJ_TILE
    steps = t // tm
    tile = lambda w_: pl.BlockSpec((tm, w_), lambda i: (i, 0))
    full = lambda shape: pl.BlockSpec(shape, lambda i: (0,) * len(shape))
    route = pl.BlockSpec((TOP_K, tm), lambda i: (0, i))
    return pl.pallas_call(
        functools.partial(_out_proj_kernel, tm=tm, steps_per_tile=MOE_TILE // tm),
        grid=(steps,),
        in_specs=[tile(D_MODEL), tile(POOL_WIDTH), tile(LRU_WIDTH), tile(ATTN_WIDTH),
                  full((1, D_MODEL)), full((D_MODEL, D_MODEL)), full((1, D_MODEL)),
                  full((N_EXPERTS, D_MODEL)), full((N_EXPERTS, 1))],
        out_specs=[tile(D_MODEL), route, route, route,
                   pl.BlockSpec((1, N_EXPERTS, LANES), lambda i: (i, 0, 0))],
        out_shape=[jax.ShapeDtypeStruct((t, D_MODEL), f32),
                   jax.ShapeDtypeStruct((TOP_K, t), i32),
                   jax.ShapeDtypeStruct((TOP_K, t), f32),
                   jax.ShapeDtypeStruct((TOP_K, t), i32),
                   jax.ShapeDtypeStruct((steps, N_EXPERTS, LANES), f32)],
        scratch_shapes=[pltpu.VMEM((N_EXPERTS, LANES), f32)],
        compiler_params=pltpu.CompilerParams(dimension_semantics=("arbitrary",),
                                             vmem_limit_bytes=VMEM_LIMIT),
        name="out_proj_router",
    )(x, yp, yl, ya, og, wo, g2, rw_t, rb)


def _moe_kernel(off_ref, idx_ref, rank_ref, gate_ref, x_ref, g2_ref, wgu_ref, bgu_ref, wd_ref, bd_ref,
                o_ref, xs3, h3, *, tt, ch):
    i = pl.program_id(0)
    e = pl.program_id(1)
    base = i * (N_EXPERTS + 1)
    norm_rows = 256
    unroll = 8

    def pos(k, tok):
        return off_ref[base + idx_ref[k, tok]] + rank_ref[k, tok]

    @pl.when(e == 0)
    def _dispatch():
        def norm(c, carry):
            sl = pl.ds(pl.multiple_of(c * norm_rows, norm_rows), norm_rows)
            h3[sl] = _rms(x_ref[sl, :], g2_ref[...]).reshape(norm_rows, 1, D_MODEL)
            return carry

        lax.fori_loop(0, tt // norm_rows, norm, 0)
        xs3[pl.ds(TOP_K * tt, ch)] = jnp.zeros((ch, 1, D_MODEL), f32)

        def scatter(c, carry):
            for u_ in range(unroll):
                tok = c * unroll + u_
                row = h3[tok]
                for k in range(TOP_K):
                    xs3[pos(k, tok)] = row
            return carry

        lax.fori_loop(0, tt // unroll, scatter, 0)

    lo = off_ref[base + e]
    hi = off_ref[base + e + 1]

    def chunk(c, carry):
        start = lo + c * ch
        xc = xs3[pl.ds(start, ch)].reshape(ch, D_MODEL)
        gu = jnp.dot(xc.astype(bf16), wgu_ref[0], preferred_element_type=f32) + bgu_ref[0]
        gate = jnp.minimum(gu[:, :EXPERT_FF], SWIGLU_LIMIT)
        up = jnp.clip(gu[:, EXPERT_FF:], -SWIGLU_LIMIT, SWIGLU_LIMIT)
        act = (up + 1.0) * gate * jax.nn.sigmoid(SWIGLU_ALPHA * gate)
        y = jnp.dot(act.astype(bf16), wd_ref[0], preferred_element_type=f32) + bd_ref[0]
        mine = (lax.broadcasted_iota(i32, (ch, 1), 0) + start) < hi
        xs3[pl.ds(start, ch)] = jnp.where(mine, y, xc).reshape(ch, 1, D_MODEL)
        return carry

    lax.fori_loop(0, (hi - lo + ch - 1) // ch, chunk, 0)

    @pl.when(e == N_EXPERTS - 1)
    def _combine():
        def gather(c, carry):
            for u_ in range(unroll):
                tok = c * unroll + u_
                acc = gate_ref[0, tok] * xs3[pos(0, tok)]
                for k in range(1, TOP_K):
                    acc = acc + gate_ref[k, tok] * xs3[pos(k, tok)]
                h3[tok] = acc
            return carry

        lax.fori_loop(0, tt // unroll, gather, 0)

        def out(c, carry):
            sl = pl.ds(pl.multiple_of(c * norm_rows, norm_rows), norm_rows)
            o_ref[sl, :] = x_ref[sl, :] + h3[sl].reshape(norm_rows, D_MODEL)
            return carry

        lax.fori_loop(0, tt // norm_rows, out, 0)


def _moe(x, idx, rank, gates, offsets, g2, wgu, bgu, wd, bd):
    t = x.shape[0]
    tt = MOE_TILE
    ch = MOE_CHUNK
    route = pl.BlockSpec((TOP_K, tt), lambda i, e, off: (0, i), memory_space=pltpu.SMEM)
    return pl.pallas_call(
        functools.partial(_moe_kernel, tt=tt, ch=ch),
        grid_spec=pltpu.PrefetchScalarGridSpec(
            num_scalar_prefetch=1,
            grid=(t // tt, N_EXPERTS),
            in_specs=[route, route, route,
                      pl.BlockSpec((tt, D_MODEL), lambda i, e, off: (i, 0)),
                      pl.BlockSpec((1, D_MODEL), lambda i, e, off: (0, 0)),
                      pl.BlockSpec((1, D_MODEL, 2 * EXPERT_FF), lambda i, e, off: (e, 0, 0)),
                      pl.BlockSpec((1, 1, 2 * EXPERT_FF), lambda i, e, off: (e, 0, 0)),
                      pl.BlockSpec((1, EXPERT_FF, D_MODEL), lambda i, e, off: (e, 0, 0)),
                      pl.BlockSpec((1, 1, D_MODEL), lambda i, e, off: (e, 0, 0))],
            out_specs=pl.BlockSpec((tt, D_MODEL), lambda i, e, off: (i, 0)),
            scratch_shapes=[pltpu.VMEM((TOP_K * tt + ch, 1, D_MODEL), f32),
                            pltpu.VMEM((tt, 1, D_MODEL), f32)]),
        out_shape=jax.ShapeDtypeStruct((t, D_MODEL), f32),
        compiler_params=pltpu.CompilerParams(dimension_semantics=("arbitrary", "arbitrary"),
                                             vmem_limit_bytes=VMEM_LIMIT),
        name="moe_experts",
    )(offsets, idx, rank, gates, x, g2, wgu, bgu, wd, bd)


def _rope_tables(seq):
    pos = jnp.arange(seq, dtype=f32)
    inv_freq = ROPE_THETA ** (-jnp.arange(0, HEAD_DIM, 2, dtype=f32) / HEAD_DIM)
    ang = pos[:, None] * inv_freq[None, :]
    cos, sin = jnp.cos(ang), jnp.sin(ang)
    reps = LANES // HEAD_DIM
    cos_t = jnp.tile(jnp.concatenate([cos, cos], axis=1), (1, reps))
    sin_t = jnp.tile(jnp.concatenate([-sin, sin], axis=1), (1, reps))
    return cos_t, sin_t


def _layer(x, cos_t, sin_t, batch, seq, p):
    t = batch * seq
    row = lambda v: v.reshape(1, -1)
    pool_u, lru_u, gate, q, k, v = _in_proj(x, row(p["norm1_g"]), p["w_in"].astype(bf16))
    wri = jnp.concatenate([block_diag(*p["lru_wa"]), block_diag(*p["lru_wx"])], axis=1).astype(bf16)
    bri = jnp.concatenate([p["lru_ba"], p["lru_bx"]]).reshape(1, -1)
    y_pool, y_lru = _mixer(pool_u, lru_u, gate, block_diag(*p["pool_w"]).astype(bf16), row(p["pool_scale"]),
                           p["conv_w"], row(p["conv_b"]), wri, bri, row(p["lru_lambda"]), batch, seq)
    reps = LANES // HEAD_DIM
    y_attn = _attention(q, k, v, cos_t, sin_t, row(jnp.tile(p["q_norm_g"], reps)),
                        row(jnp.tile(p["k_norm_g"], reps)), batch, seq)
    x, idx, gates, rank, cnt = _out_proj(x, y_pool, y_lru, y_attn, row(p["out_norm_g"]), p["w_out"].astype(bf16),
                                         row(p["norm2_g"]), p["router_w"].T, p["router_b"].reshape(-1, 1))
    steps_per_tile = MOE_TILE // PROJ_TILE
    counts = cnt[steps_per_tile - 1::steps_per_tile, :, 0].astype(i32)
    offsets = jnp.concatenate([jnp.zeros((counts.shape[0], 1), i32), jnp.cumsum(counts, axis=1)], axis=1)
    return _moe(x, idx, rank, gates, offsets.reshape(-1), row(p["norm2_g"]),
                p["w_gate_up"].astype(bf16), p["b_gate_up"][:, None, :],
                p["w_down"].astype(bf16), p["b_down"][:, None, :])


def kernel(x, norm1_g, w_in, pool_w, pool_scale, conv_w, conv_b, lru_wa, lru_ba, lru_wx, lru_bx, lru_lambda,
           q_norm_g, k_norm_g, out_norm_g, w_out, norm2_g, router_w, router_b, w_gate_up, b_gate_up,
           w_down, b_down):
    batch, seq, d = x.shape
    params = dict(norm1_g=norm1_g, w_in=w_in, pool_w=pool_w, pool_scale=pool_scale, conv_w=conv_w, conv_b=conv_b,
                  lru_wa=lru_wa, lru_ba=lru_ba, lru_wx=lru_wx, lru_bx=lru_bx, lru_lambda=lru_lambda,
                  q_norm_g=q_norm_g, k_norm_g=k_norm_g, out_norm_g=out_norm_g, w_out=w_out, norm2_g=norm2_g,
                  router_w=router_w, router_b=router_b, w_gate_up=w_gate_up, b_gate_up=b_gate_up,
                  w_down=w_down, b_down=b_down)
    cos_t, sin_t = _rope_tables(seq)
    xt = x.reshape(batch * seq, d)
    for layer in range(norm1_g.shape[0]):
        xt = _layer(xt, cos_t, sin_t, batch, seq, {name: val[layer] for name, val in params.items()})
    return xt.reshape(batch, seq, d)
```

```python
import functools

import jax
import jax.numpy as jnp
from jax import lax
from jax.experimental import pallas as pl
from jax.experimental.pallas import tpu as pltpu
from jax.scipy.linalg import block_diag

f32 = jnp.float32
bf16 = jnp.bfloat16
i32 = jnp.int32

D_MODEL = 1024
POOL_WIDTH = 256
POOL_WINDOWS = (2, 4, 8, 16)
LRU_WIDTH = 256
LRU_C = 8.0
ATTN_WIDTH = 512
HEAD_DIM = 64
DILATIONS = (1, 4, 16)
ATTN_BLOCK = 128
ROPE_THETA = 10000.0
N_EXPERTS = 32
TOP_K = 4
EXPERT_FF = 256
SWIGLU_LIMIT = 7.0
SWIGLU_ALPHA = 1.702
NORM_EPS = 1e-6
NEG_INF = -1e30

LANES = 128
POOL_HALO = 16
CONV_HALO = 8
VMEM_LIMIT = 56 * 1024 * 1024

PROJ_TILE = 512
MIX_TILE = 512
MOE_TILE = 2048
MOE_CHUNK = 128
ATTN_INTERLEAVE = 4


def _rms(x, g):
    return x * lax.rsqrt(jnp.mean(x * x, axis=-1, keepdims=True) + NORM_EPS) * g


_IN_SPLITS = (0, 256, 512, 768, 1280, 1792, 2304)


def _in_proj_kernel(x_ref, g_ref, w_ref, *out_refs):
    h = _rms(x_ref[...], g_ref[...]).astype(bf16)
    for o_ref, c0, c1 in zip(out_refs, _IN_SPLITS[:-1], _IN_SPLITS[1:]):
        o_ref[...] = jnp.dot(h, w_ref[:, c0:c1], preferred_element_type=f32)


def _in_proj(x, g, w):
    t = x.shape[0]
    tm = PROJ_TILE
    widths = [c1 - c0 for c0, c1 in zip(_IN_SPLITS[:-1], _IN_SPLITS[1:])]
    return pl.pallas_call(
        _in_proj_kernel,
        grid=(t // tm,),
        in_specs=[pl.BlockSpec((tm, D_MODEL), lambda i: (i, 0)),
                  pl.BlockSpec((1, D_MODEL), lambda i: (0, 0)),
                  pl.BlockSpec((D_MODEL, _IN_SPLITS[-1]), lambda i: (0, 0))],
        out_specs=[pl.BlockSpec((tm, w_), lambda i: (i, 0)) for w_ in widths],
        out_shape=[jax.ShapeDtypeStruct((t, w_), f32) for w_ in widths],
        compiler_params=pltpu.CompilerParams(dimension_semantics=("arbitrary",),
                                             vmem_limit_bytes=VMEM_LIMIT),
        name="in_proj",
    )(x, g, w)


def _mixer_kernel(pool_ref, lru_ref, gate_ref, pw_ref, ps_ref, cw_ref, cb_ref, wri_ref, bri_ref, lam_ref,
                  ypool_ref, ylru_ref,
                  b0, b1, b2, b3, cbuf, a_s, b_s, h_s, a2_s, b2_s, e2_s, e3_s, hc, *, ts):
    j = pl.program_id(1)
    n1 = ts // 8
    n2 = n1 // 8

    @pl.when(j == 0)
    def _():
        for buf in (b0, b1, b2, b3):
            buf[pl.ds(0, POOL_HALO), :] = jnp.zeros((POOL_HALO, POOL_WIDTH), f32)
        cbuf[pl.ds(0, CONV_HALO), :] = jnp.zeros((CONV_HALO, LRU_WIDTH), f32)
        hc[...] = jnp.zeros_like(hc)

    row = lax.broadcasted_iota(i32, (ts, POOL_WIDTH), 0) + j * ts
    grp = lax.broadcasted_iota(i32, (ts, POOL_WIDTH), 1) // (POOL_WIDTH // len(POOL_WINDOWS))

    u = pool_ref[...]
    b0[pl.ds(POOL_HALO, ts), :] = u
    s1 = u + b0[pl.ds(POOL_HALO - 1, ts), :]
    b1[pl.ds(POOL_HALO, ts), :] = s1
    s2 = s1 + b1[pl.ds(POOL_HALO - 2, ts), :]
    b2[pl.ds(POOL_HALO, ts), :] = s2
    s3 = s2 + b2[pl.ds(POOL_HALO - 4, ts), :]
    b3[pl.ds(POOL_HALO, ts), :] = s3
    s4 = s3 + b3[pl.ds(POOL_HALO - 8, ts), :]
    for buf in (b0, b1, b2, b3):
        buf[pl.ds(0, POOL_HALO), :] = buf[pl.ds(ts, POOL_HALO), :]
    sums = jnp.where(grp == 0, s1, jnp.where(grp == 1, s2, jnp.where(grp == 2, s3, s4)))
    win = jnp.where(grp == 0, float(POOL_WINDOWS[0]),
                    jnp.where(grp == 1, float(POOL_WINDOWS[1]),
                              jnp.where(grp == 2, float(POOL_WINDOWS[2]), float(POOL_WINDOWS[3]))))
    cnt = jnp.minimum((row + 1).astype(f32), win)
    pooled = sums / cnt - u
    ypool_ref[...] = jnp.dot(pooled.astype(bf16), pw_ref[...], preferred_element_type=f32) * ps_ref[...]

    lu = lru_ref[...]
    cbuf[pl.ds(CONV_HALO, ts), :] = lu
    xc = (cb_ref[...] + cw_ref[3:4, :] * lu
          + cw_ref[2:3, :] * cbuf[pl.ds(CONV_HALO - 1, ts), :]
          + cw_ref[1:2, :] * cbuf[pl.ds(CONV_HALO - 2, ts), :]
          + cw_ref[0:1, :] * cbuf[pl.ds(CONV_HALO - 3, ts), :])
    cbuf[pl.ds(0, CONV_HALO), :] = cbuf[pl.ds(ts, CONV_HALO), :]
    ri = jnp.dot(xc.astype(bf16), wri_ref[...], preferred_element_type=f32) + bri_ref[...]
    r = jax.nn.sigmoid(ri[:, :LRU_WIDTH])
    ig = jax.nn.sigmoid(ri[:, LRU_WIDTH:])
    log_a = -LRU_C * r * jax.nn.softplus(-lam_ref[...])
    a = jnp.exp(log_a)
    th = jnp.tanh(log_a)
    mult = jnp.sqrt(-2.0 * th / (1.0 - th))
    mult = jnp.where(row == 0, 1.0, mult)
    bb = mult * (ig * xc)

    def rows(ref, hf, s, n):
        return ref[hf, pl.ds(s, n, stride=8), :]

    for hf in range(LRU_WIDTH // LANES):
        a_s[hf] = a[:, hf * LANES:(hf + 1) * LANES]
        b_s[hf] = bb[:, hf * LANES:(hf + 1) * LANES]
        pa, pb = rows(a_s, hf, 0, n1), rows(b_s, hf, 0, n1)
        for s in range(1, 8):
            as_ = rows(a_s, hf, s, n1)
            pb = as_ * pb + rows(b_s, hf, s, n1)
            pa = as_ * pa
        a2_s[hf] = pa
        b2_s[hf] = pb
        pa, pb = rows(a2_s, hf, 0, n2), rows(b2_s, hf, 0, n2)
        for s in range(1, 8):
            as_ = rows(a2_s, hf, s, n2)
            pb = as_ * pb + rows(b2_s, hf, s, n2)
            pa = as_ * pa
        h = hc[hf, 0:1, :]
        for k in range(8):
            e3_s[hf, k:k + 1, :] = h
            h = pa[k:k + 1, :] * h + pb[k:k + 1, :]
        hc[hf, 0:1, :] = h
        prev = e3_s[hf]
        for s in range(8):
            e2_s[hf, pl.ds(s, n2, stride=8), :] = prev
            prev = rows(a2_s, hf, s, n2) * prev + rows(b2_s, hf, s, n2)
        prev = e2_s[hf]
        for s in range(8):
            prev = rows(a_s, hf, s, n1) * prev + rows(b_s, hf, s, n1)
            h_s[hf, pl.ds(s, n1, stride=8), :] = prev

    hfull = jnp.concatenate([h_s[hf] for hf in range(LRU_WIDTH // LANES)], axis=1)
    ylru_ref[...] = hfull * jax.nn.gelu(gate_ref[...])


def _mixer(pool_u, lru_u, gate, pw, ps, cw, cb, wri, bri, lam, batch, seq):
    ts = MIX_TILE
    nt = seq // ts
    t = batch * seq
    nh = LRU_WIDTH // LANES
    tile = lambda w_: pl.BlockSpec((ts, w_), lambda b, j: (b * nt + j, 0))
    full = lambda shape: pl.BlockSpec(shape, lambda b, j: (0,) * len(shape))
    return pl.pallas_call(
        functools.partial(_mixer_kernel, ts=ts),
        grid=(batch, nt),
        in_specs=[tile(POOL_WIDTH), tile(LRU_WIDTH), tile(LRU_WIDTH),
                  full((POOL_WIDTH, POOL_WIDTH)), full((1, POOL_WIDTH)),
                  full((4, LRU_WIDTH)), full((1, LRU_WIDTH)),
                  full((LRU_WIDTH, 2 * LRU_WIDTH)), full((1, 2 * LRU_WIDTH)), full((1, LRU_WIDTH))],
        out_specs=[tile(POOL_WIDTH), tile(LRU_WIDTH)],
        out_shape=[jax.ShapeDtypeStruct((t, POOL_WIDTH), f32), jax.ShapeDtypeStruct((t, LRU_WIDTH), f32)],
        scratch_shapes=[pltpu.VMEM((ts + POOL_HALO, POOL_WIDTH), f32)] * 4
        + [pltpu.VMEM((ts + CONV_HALO, LRU_WIDTH), f32)]
        + [pltpu.VMEM((nh, ts, LANES), f32)] * 3
        + [pltpu.VMEM((nh, ts // 8, LANES), f32)] * 3
        + [pltpu.VMEM((nh, 8, LANES), f32)] * 2,
        compiler_params=pltpu.CompilerParams(dimension_semantics=("arbitrary", "arbitrary"),
                                             vmem_limit_bytes=VMEM_LIMIT),
        name="pool_lru_mixer",
    )(pool_u, lru_u, gate, pw, ps, cw, cb, wri, bri, lam)


def _attn_kernel(q_ref, k_ref, v_ref, cos_ref, sin_ref, qg_ref, kg_ref, o_ref,
                 qs, ks, m_s, l_s, acc_s, *, seq):
    blk = ATTN_BLOCK
    prep_rows = 256
    lane = lax.broadcasted_iota(i32, (prep_rows, LANES), 1)
    head0 = lane < HEAD_DIM
    first_half = (lane % HEAD_DIM) < (HEAD_DIM // 2)

    def prep(c, carry):
        sl = pl.ds(pl.multiple_of(c * prep_rows, prep_rows), prep_rows)
        cs = cos_ref[sl, :]
        sn = sin_ref[sl, :]
        for src, dst, g_ref, scale in ((q_ref, qs, qg_ref, HEAD_DIM ** -0.5), (k_ref, ks, kg_ref, 1.0)):
            x = src[sl, :]
            xx = x * x
            ms0 = jnp.sum(jnp.where(head0, xx, 0.0), axis=-1, keepdims=True)
            ms1 = jnp.sum(jnp.where(head0, 0.0, xx), axis=-1, keepdims=True)
            ms = jnp.where(head0, ms0, ms1) * (1.0 / HEAD_DIM)
            xn = x * lax.rsqrt(ms + NORM_EPS) * g_ref[...]
            partner = jnp.where(first_half,
                                pltpu.roll(xn, LANES - HEAD_DIM // 2, 1),
                                pltpu.roll(xn, HEAD_DIM // 2, 1))
            dst[sl, :] = (xn * cs + partner * sn) * scale
        return carry

    lax.fori_loop(0, seq // prep_rows, prep, 0)

    qi = lax.broadcasted_iota(i32, (blk, 2 * blk), 0)
    kj = lax.broadcasted_iota(i32, (blk, 2 * blk), 1)
    mask_cur = (kj >= blk) & (kj - blk <= qi)
    mask_prev = (kj < blk) & (kj >= qi)
    h0 = lax.broadcasted_iota(i32, (blk, LANES), 1) < HEAD_DIM
    h0_keys = lax.broadcasted_iota(i32, (2 * blk, LANES), 1) < HEAD_DIM
    heads = ((h0, h0_keys), (~h0, ~h0_keys))
    state = (m_s, l_s, acc_s)

    for bi, d in enumerate(DILATIONS):
        nb = seq // (blk * d)

        def ld(ref, start, d=d):
            if d == 1:
                return ref[pl.ds(pl.multiple_of(start, blk), blk), :]
            return ref[pl.ds(start, blk, stride=d), :]

        def st(ref, start, val, d=d):
            if d == 1:
                ref[pl.ds(pl.multiple_of(start, blk), blk), :] = val
            else:
                ref[pl.ds(start, blk, stride=d), :] = val

        def body(it, carry, d=d, nb=nb, bi=bi, ld=ld, st=st):
            loaded = []
            for u_ in range(ATTN_INTERLEAVE):
                idx = it * ATTN_INTERLEAVE + u_
                res = idx // nb
                n = idx % nb
                q_start = n * (blk * d) + res
                p_start = jnp.maximum(n - 1, 0) * (blk * d) + res
                qb = ld(qs, q_start)
                kb = jnp.concatenate([ld(ks, p_start), ld(ks, q_start)], axis=0).astype(bf16)
                vb = jnp.concatenate([ld(v_ref, p_start), ld(v_ref, q_start)], axis=0)
                old = [ld(ref, q_start) for ref in state] if bi > 0 else None
                loaded.append((n, q_start, qb, kb, vb, old))
            results = []
            for n, q_start, qb, kb, vb, old in loaded:
                valid = mask_cur | (mask_prev & (n > 0))
                mbs, cs = [], []
                for hq, hk in heads:
                    qh = jnp.where(hq, qb, 0.0).astype(bf16)
                    s = lax.dot_general(qh, kb, (((1,), (1,)), ((), ())), preferred_element_type=f32)
                    s = jnp.where(valid, s, NEG_INF)
                    mb = jnp.max(s, axis=-1, keepdims=True)
                    p = jnp.exp(s - mb).astype(bf16)
                    vh = jnp.where(hk, vb, 1.0).astype(bf16)
                    cs.append(jnp.dot(p, vh, preferred_element_type=f32))
                    mbs.append(mb)
                m_blk = jnp.where(h0, mbs[0], mbs[1])
                n_blk = jnp.where(h0, cs[0], cs[1])
                l_blk = jnp.where(h0, pltpu.roll(cs[0], HEAD_DIM, 1), pltpu.roll(cs[1], HEAD_DIM, 1))
                if old is None:
                    results.append((q_start, m_blk, l_blk, n_blk))
                else:
                    m_old, l_old, a_old = old
                    m_new = jnp.maximum(m_old, m_blk)
                    w_old = jnp.exp(m_old - m_new)
                    w_blk = jnp.exp(m_blk - m_new)
                    results.append((q_start, m_new, w_old * l_old + w_blk * l_blk, w_old * a_old + w_blk * n_blk))
            for q_start, *vals in results:
                for ref, val in zip(state, vals):
                    st(ref, q_start, val)
            return carry

        lax.fori_loop(0, seq // (blk * ATTN_INTERLEAVE), body, 0)

    def fin(c, carry):
        sl = pl.ds(pl.multiple_of(c * prep_rows, prep_rows), prep_rows)
        o_ref[sl, :] = acc_s[sl, :] / l_s[sl, :]
        return carry

    lax.fori_loop(0, seq // prep_rows, fin, 0)


def _attention(q, k, v, cos_t, sin_t, qg, kg, batch, seq):
    t = batch * seq
    npair = ATTN_WIDTH // LANES
    tile = pl.BlockSpec((seq, LANES), lambda b, p: (b, p))
    table = pl.BlockSpec((seq, LANES), lambda b, p: (0, 0))
    gain = pl.BlockSpec((1, LANES), lambda b, p: (0, 0))
    return pl.pallas_call(
        functools.partial(_attn_kernel, seq=seq),
        grid=(batch, npair),
        in_specs=[tile, tile, tile, table, table, gain, gain],
        out_specs=tile,
        out_shape=jax.ShapeDtypeStruct((t, ATTN_WIDTH), f32),
        scratch_shapes=[pltpu.VMEM((seq, LANES), f32)] * 5,
        compiler_params=pltpu.CompilerParams(dimension_semantics=("arbitrary", "arbitrary"),
                                             vmem_limit_bytes=VMEM_LIMIT),
        name="dilated_attention",
    )(q, k, v, cos_t, sin_t, qg, kg)


def _out_proj_kernel(x_ref, yp_ref, yl_ref, ya_ref, og_ref, wo_ref, g2_ref, rw_ref, rb_ref,
                     xo_ref, idx_ref, gate_ref, rank_ref, cnt_ref, cnt_s, *, tm, steps_per_tile):
    i = pl.program_id(0)
    c0, c1 = POOL_WIDTH, POOL_WIDTH + LRU_WIDTH
    acc = x_ref[...]
    for y_ref, lo, hi in ((yp_ref, 0, c0), (yl_ref, c0, c1), (ya_ref, c1, D_MODEL)):
        mix = _rms(y_ref[...], og_ref[:, lo:hi]).astype(bf16)
        acc = acc + jnp.dot(mix, wo_ref[lo:hi, :], preferred_element_type=f32)
    xo_ref[...] = acc

    h2 = _rms(acc, g2_ref[...])
    logits = lax.dot_general(rw_ref[...], h2, (((1,), (1,)), ((), ())), precision=lax.Precision.HIGHEST,
                             preferred_element_type=f32) + rb_ref[...]
    eio = lax.broadcasted_iota(i32, (N_EXPERTS, tm), 0)
    cur = logits
    vals, idxs = [], []
    for _ in range(TOP_K):
        mx = jnp.max(cur, axis=0, keepdims=True)
        ix = jnp.min(jnp.where(cur == mx, eio, N_EXPERTS), axis=0, keepdims=True)
        vals.append(mx)
        idxs.append(ix)
        cur = jnp.where(eio == ix, -jnp.inf, cur)
    ex = [jnp.exp(v - vals[0]) for v in vals]
    den = ex[0] + ex[1] + ex[2] + ex[3]

    @pl.when(i % steps_per_tile == 0)
    def _():
        cnt_s[...] = jnp.zeros_like(cnt_s)

    onehot = jnp.zeros((N_EXPERTS, tm), f32)
    for ix in idxs:
        onehot = onehot + (eio == ix).astype(f32)
    before = (lax.broadcasted_iota(i32, (tm, tm), 0) < lax.broadcasted_iota(i32, (tm, tm), 1)).astype(bf16)
    prefix = jnp.dot(onehot.astype(bf16), before, preferred_element_type=f32) + cnt_s[:, 0:1]
    for k in range(TOP_K):
        idx_ref[k:k + 1, :] = idxs[k]
        gate_ref[k:k + 1, :] = ex[k] / den
        rank_ref[k:k + 1, :] = jnp.sum(jnp.where(eio == idxs[k], prefix, 0.0), axis=0, keepdims=True).astype(i32)
    cnt_s[...] = cnt_s[...] + jnp.sum(onehot, axis=1, keepdims=True)
    cnt_ref[0] = cnt_s[...]


def _out_proj(x, yp, yl, ya, og, wo, g2, rw_t, rb):
    t = x.shape[0]
    tm = PROJ_TILE
    steps = t // tm
    tile = lambda w_: pl.BlockSpec((tm, w_), lambda i: (i, 0))
    full = lambda shape: pl.BlockSpec(shape, lambda i: (0,) * len(shape))
    route = pl.BlockSpec((TOP_K, tm), lambda i: (0, i))
    return pl.pallas_call(
        functools.partial(_out_proj_kernel, tm=tm, steps_per_tile=MOE_TILE // tm),
        grid=(steps,),
        in_specs=[tile(D_MODEL), tile(POOL_WIDTH), tile(LRU_WIDTH), tile(ATTN_WIDTH),
                  full((1, D_MODEL)), full((D_MODEL, D_MODEL)), full((1, D_MODEL)),
                  full((N_EXPERTS, D_MODEL)), full((N_EXPERTS, 1))],
        out_specs=[tile(D_MODEL), route, route, route,
                   pl.BlockSpec((1, N_EXPERTS, LANES), lambda i: (i, 0, 0))],
        out_shape=[jax.ShapeDtypeStruct((t, D_MODEL), f32),
                   jax.ShapeDtypeStruct((TOP_K, t), i32),
                   jax.ShapeDtypeStruct((TOP_K, t), f32),
                   jax.ShapeDtypeStruct((TOP_K, t), i32),
                   jax.ShapeDtypeStruct((steps, N_EXPERTS, LANES), f32)],
        scratch_shapes=[pltpu.VMEM((N_EXPERTS, LANES), f32)],
        compiler_params=pltpu.CompilerParams(dimension_semantics=("arbitrary",),
                                             vmem_limit_bytes=VMEM_LIMIT),
        name="out_proj_router",
    )(x, yp, yl, ya, og, wo, g2, rw_t, rb)


ROW_SLAB = 8


def _slab_cols(ref, first_row, n_rows, col):
    return ref[pl.ds(first_row * ROW_SLAB + col, n_rows, stride=ROW_SLAB), :]


def _moe_kernel(off_ref, pos_ref, gate_ref, x_ref, g2_ref, wgu_ref, bgu_ref, wd_ref, bd_ref,
                o_ref, sorted_s, stage_s, *, tt, ch, io_rows):
    i = pl.program_id(0)
    s = pl.program_id(1)
    n_io = tt // io_rows
    base = i * (N_EXPERTS + 1)
    unroll = 8
    n_col = D_MODEL // LANES

    def slab(ref, r):
        return ref.at[pl.ds(pl.multiple_of(r * ROW_SLAB, ROW_SLAB), ROW_SLAB), :]

    @pl.when(s < n_io)
    def _dispatch():
        @pl.when(s == 0)
        def _():
            sorted_s[pl.ds(TOP_K * tt * ROW_SLAB, ch * ROW_SLAB), :] = jnp.zeros((ch * ROW_SLAB, LANES), f32)

        h = _rms(x_ref[...], g2_ref[...])
        for j in range(n_col):
            stage_s[pl.ds(j, io_rows, stride=ROW_SLAB), :] = h[:, j * LANES:(j + 1) * LANES]

        def scatter(c, carry):
            for u_ in range(unroll):
                tok = c * unroll + u_
                row = slab(stage_s, tok)[...]
                for k in range(TOP_K):
                    slab(sorted_s, pos_ref[tok * TOP_K + k])[...] = row
            return carry

        lax.fori_loop(0, io_rows // unroll, scatter, 0)

    @pl.when((s >= n_io) & (s < n_io + N_EXPERTS))
    def _experts():
        e = s - n_io
        lo = off_ref[base + e]
        hi = off_ref[base + e + 1]

        def chunk(c, carry):
            start = lo + c * ch
            xc = jnp.concatenate([_slab_cols(sorted_s, start, ch, j) for j in range(n_col)], axis=1)
            gu = jnp.dot(xc.astype(bf16), wgu_ref[0], preferred_element_type=f32) + bgu_ref[0]
            gate = jnp.minimum(gu[:, :EXPERT_FF], SWIGLU_LIMIT)
            up = jnp.clip(gu[:, EXPERT_FF:], -SWIGLU_LIMIT, SWIGLU_LIMIT)
            act = (up + 1.0) * gate * jax.nn.sigmoid(SWIGLU_ALPHA * gate)
            y = jnp.dot(act.astype(bf16), wd_ref[0], preferred_element_type=f32) + bd_ref[0]
            mine = (lax.broadcasted_iota(i32, (ch, 1), 0) + start) < hi
            y = jnp.where(mine, y, xc)
            for j in range(n_col):
                sorted_s[pl.ds(start * ROW_SLAB + j, ch, stride=ROW_SLAB), :] = y[:, j * LANES:(j + 1) * LANES]
            return carry

        lax.fori_loop(0, (hi - lo + ch - 1) // ch, chunk, 0)

    @pl.when(s >= n_io + N_EXPERTS)
    def _combine():
        def gather(c, carry):
            for u_ in range(unroll):
                tok = c * unroll + u_
                acc = gate_ref[tok * TOP_K] * slab(sorted_s, pos_ref[tok * TOP_K])[...]
                for k in range(1, TOP_K):
                    acc = acc + gate_ref[tok * TOP_K + k] * slab(sorted_s, pos_ref[tok * TOP_K + k])[...]
                slab(stage_s, tok)[...] = acc
            return carry

        lax.fori_loop(0, io_rows // unroll, gather, 0)
        moe = jnp.concatenate([_slab_cols(stage_s, 0, io_rows, j) for j in range(n_col)], axis=1)
        o_ref[...] = x_ref[...] + moe


def _moe(x, pos, gates, offsets, g2, wgu, bgu, wd, bd):
    t = x.shape[0]
    tt = MOE_TILE
    ch = MOE_CHUNK
    io_rows = PROJ_TILE
    n_io = tt // io_rows
    n_steps = 2 * n_io + N_EXPERTS

    def tok_idx(i, s, off):
        in_combine = s >= n_io + N_EXPERTS
        return i * n_io + jnp.where(s < n_io, s, jnp.where(in_combine, s - n_io - N_EXPERTS, n_io - 1))

    def out_idx(i, s, off):
        return i * n_io + jnp.maximum(s - n_io - N_EXPERTS, 0)

    def expert_idx(i, s, off):
        return jnp.clip(s - n_io, 0, N_EXPERTS - 1)

    route = pl.BlockSpec((TOP_K * io_rows,), lambda i, s, off: (tok_idx(i, s, off),), memory_space=pltpu.SMEM)
    return pl.pallas_call(
        functools.partial(_moe_kernel, tt=tt, ch=ch, io_rows=io_rows),
        grid_spec=pltpu.PrefetchScalarGridSpec(
            num_scalar_prefetch=1,
            grid=(t // tt, n_steps),
            in_specs=[route, route,
                      pl.BlockSpec((io_rows, D_MODEL), lambda i, s, off: (tok_idx(i, s, off), 0)),
                      pl.BlockSpec((1, D_MODEL), lambda i, s, off: (0, 0)),
                      pl.BlockSpec((1, D_MODEL, 2 * EXPERT_FF), lambda i, s, off: (expert_idx(i, s, off), 0, 0)),
                      pl.BlockSpec((1, 1, 2 * EXPERT_FF), lambda i, s, off: (expert_idx(i, s, off), 0, 0)),
                      pl.BlockSpec((1, EXPERT_FF, D_MODEL), lambda i, s, off: (expert_idx(i, s, off), 0, 0)),
                      pl.BlockSpec((1, 1, D_MODEL), lambda i, s, off: (expert_idx(i, s, off), 0, 0))],
            out_specs=pl.BlockSpec((io_rows, D_MODEL), lambda i, s, off: (out_idx(i, s, off), 0)),
            scratch_shapes=[pltpu.VMEM(((TOP_K * tt + ch) * ROW_SLAB, LANES), f32),
                            pltpu.VMEM((io_rows * ROW_SLAB, LANES), f32)]),
        out_shape=jax.ShapeDtypeStruct((t, D_MODEL), f32),
        compiler_params=pltpu.CompilerParams(dimension_semantics=("arbitrary", "arbitrary"),
                                             vmem_limit_bytes=VMEM_LIMIT),
        name="moe_experts",
    )(offsets, pos, gates, x, g2, wgu, bgu, wd, bd)


def _rope_tables(seq):
    pos = jnp.arange(seq, dtype=f32)
    inv_freq = ROPE_THETA ** (-jnp.arange(0, HEAD_DIM, 2, dtype=f32) / HEAD_DIM)
    ang = pos[:, None] * inv_freq[None, :]
    cos, sin = jnp.cos(ang), jnp.sin(ang)
    reps = LANES // HEAD_DIM
    cos_t = jnp.tile(jnp.concatenate([cos, cos], axis=1), (1, reps))
    sin_t = jnp.tile(jnp.concatenate([-sin, sin], axis=1), (1, reps))
    return cos_t, sin_t


def _layer(x, cos_t, sin_t, batch, seq, p):
    t = batch * seq
    row = lambda v: v.reshape(1, -1)
    pool_u, lru_u, gate, q, k, v = _in_proj(x, row(p["norm1_g"]), p["w_in"].astype(bf16))
    wri = jnp.concatenate([block_diag(*p["lru_wa"]), block_diag(*p["lru_wx"])], axis=1).astype(bf16)
    bri = jnp.concatenate([p["lru_ba"], p["lru_bx"]]).reshape(1, -1)
    y_pool, y_lru = _mixer(pool_u, lru_u, gate, block_diag(*p["pool_w"]).astype(bf16), row(p["pool_scale"]),
                           p["conv_w"], row(p["conv_b"]), wri, bri, row(p["lru_lambda"]), batch, seq)
    reps = LANES // HEAD_DIM
    y_attn = _attention(q, k, v, cos_t, sin_t, row(jnp.tile(p["q_norm_g"], reps)),
                        row(jnp.tile(p["k_norm_g"], reps)), batch, seq)
    x, idx, gates, rank, cnt = _out_proj(x, y_pool, y_lru, y_attn, row(p["out_norm_g"]), p["w_out"].astype(bf16),
                                         row(p["norm2_g"]), p["router_w"].T, p["router_b"].reshape(-1, 1))
    steps_per_tile = MOE_TILE // PROJ_TILE
    n_tiles = t // MOE_TILE
    counts = cnt[steps_per_tile - 1::steps_per_tile, :, 0].astype(i32)
    offsets = jnp.concatenate([jnp.zeros((n_tiles, 1), i32), jnp.cumsum(counts, axis=1)], axis=1)
    seg_start = jnp.take_along_axis(offsets[None, :, :N_EXPERTS], idx.reshape(TOP_K, n_tiles, MOE_TILE), axis=2)
    pos = rank + seg_start.reshape(TOP_K, t)
    return _moe(x, pos.T.reshape(-1), gates.T.reshape(-1), offsets.reshape(-1), row(p["norm2_g"]),
                p["w_gate_up"].astype(bf16), p["b_gate_up"][:, None, :],
                p["w_down"].astype(bf16), p["b_down"][:, None, :])


def kernel(x, norm1_g, w_in, pool_w, pool_scale, conv_w, conv_b, lru_wa, lru_ba, lru_wx, lru_bx, lru_lambda,
           q_norm_g, k_norm_g, out_norm_g, w_out, norm2_g, router_w, router_b, w_gate_up, b_gate_up,
           w_down, b_down):
    batch, seq, d = x.shape
    params = dict(norm1_g=norm1_g, w_in=w_in, pool_w=pool_w, pool_scale=pool_scale, conv_w=conv_w, conv_b=conv_b,
                  lru_wa=lru_wa, lru_ba=lru_ba, lru_wx=lru_wx, lru_bx=lru_bx, lru_lambda=lru_lambda,
                  q_norm_g=q_norm_g, k_norm_g=k_norm_g, out_norm_g=out_norm_g, w_out=w_out, norm2_g=norm2_g,
                  router_w=router_w, router_b=router_b, w_gate_up=w_gate_up, b_gate_up=b_gate_up,
                  w_down=w_down, b_down=b_down)
    cos_t, sin_t = _rope_tables(seq)
    xt = x.reshape(batch * seq, d)
    for layer in range(norm1_g.shape[0]):
        xt = _layer(xt, cos_t, sin_t, batch, seq, {name: val[layer] for name, val in params.items()})
    return xt.reshape(batch, seq, d)
```

```python
import functools

import jax
import jax.numpy as jnp
from jax import lax
from jax.experimental import pallas as pl
from jax.experimental.pallas import tpu as pltpu
from jax.scipy.linalg import block_diag

f32 = jnp.float32
bf16 = jnp.bfloat16
i32 = jnp.int32

D_MODEL = 1024
POOL_WIDTH = 256
POOL_WINDOWS = (2, 4, 8, 16)
LRU_WIDTH = 256
LRU_C = 8.0
ATTN_WIDTH = 512
HEAD_DIM = 64
DILATIONS = (1, 4, 16)
ATTN_BLOCK = 128
ROPE_THETA = 10000.0
N_EXPERTS = 32
TOP_K = 4
EXPERT_FF = 256
SWIGLU_LIMIT = 7.0
SWIGLU_ALPHA = 1.702
NORM_EPS = 1e-6
NEG_INF = -1e30

LANES = 128
POOL_HALO = 16
CONV_HALO = 8
VMEM_LIMIT = 56 * 1024 * 1024

PROJ_TILE = 512
MIX_TILE = 512
MOE_TILE = 2048
MOE_CHUNK = 256
ATTN_INTERLEAVE = 4


def _rms(x, g):
    return x * lax.rsqrt(jnp.mean(x * x, axis=-1, keepdims=True) + NORM_EPS) * g


_IN_SPLITS = (0, 256, 512, 768, 1280, 1792, 2304)


def _in_proj_kernel(x_ref, g_ref, w_ref, *out_refs):
    h = _rms(x_ref[...], g_ref[...]).astype(bf16)
    for o_ref, c0, c1 in zip(out_refs, _IN_SPLITS[:-1], _IN_SPLITS[1:]):
        o_ref[...] = jnp.dot(h, w_ref[:, c0:c1], preferred_element_type=f32)


def _in_proj(x, g, w):
    t = x.shape[0]
    tm = PROJ_TILE
    widths = [c1 - c0 for c0, c1 in zip(_IN_SPLITS[:-1], _IN_SPLITS[1:])]
    return pl.pallas_call(
        _in_proj_kernel,
        grid=(t // tm,),
        in_specs=[pl.BlockSpec((tm, D_MODEL), lambda i: (i, 0)),
                  pl.BlockSpec((1, D_MODEL), lambda i: (0, 0)),
                  pl.BlockSpec((D_MODEL, _IN_SPLITS[-1]), lambda i: (0, 0))],
        out_specs=[pl.BlockSpec((tm, w_), lambda i: (i, 0)) for w_ in widths],
        out_shape=[jax.ShapeDtypeStruct((t, w_), f32) for w_ in widths],
        compiler_params=pltpu.CompilerParams(dimension_semantics=("arbitrary",),
                                             vmem_limit_bytes=VMEM_LIMIT),
        name="in_proj",
    )(x, g, w)


def _mixer_kernel(pool_ref, lru_ref, gate_ref, pw_ref, ps_ref, cw_ref, cb_ref, wri_ref, bri_ref, lam_ref,
                  ypool_ref, ylru_ref,
                  b0, b1, b2, b3, cbuf, a_s, b_s, h_s, a2_s, b2_s, e2_s, e3_s, hc, *, ts):
    j = pl.program_id(1)
    n1 = ts // 8
    n2 = n1 // 8

    @pl.when(j == 0)
    def _():
        for buf in (b0, b1, b2, b3):
            buf[pl.ds(0, POOL_HALO), :] = jnp.zeros((POOL_HALO, POOL_WIDTH), f32)
        cbuf[pl.ds(0, CONV_HALO), :] = jnp.zeros((CONV_HALO, LRU_WIDTH), f32)
        hc[...] = jnp.zeros_like(hc)

    row = lax.broadcasted_iota(i32, (ts, POOL_WIDTH), 0) + j * ts
    grp = lax.broadcasted_iota(i32, (ts, POOL_WIDTH), 1) // (POOL_WIDTH // len(POOL_WINDOWS))

    u = pool_ref[...]
    b0[pl.ds(POOL_HALO, ts), :] = u
    s1 = u + b0[pl.ds(POOL_HALO - 1, ts), :]
    b1[pl.ds(POOL_HALO, ts), :] = s1
    s2 = s1 + b1[pl.ds(POOL_HALO - 2, ts), :]
    b2[pl.ds(POOL_HALO, ts), :] = s2
    s3 = s2 + b2[pl.ds(POOL_HALO - 4, ts), :]
    b3[pl.ds(POOL_HALO, ts), :] = s3
    s4 = s3 + b3[pl.ds(POOL_HALO - 8, ts), :]
    for buf in (b0, b1, b2, b3):
        buf[pl.ds(0, POOL_HALO), :] = buf[pl.ds(ts, POOL_HALO), :]
    sums = jnp.where(grp == 0, s1, jnp.where(grp == 1, s2, jnp.where(grp == 2, s3, s4)))
    win = jnp.where(grp == 0, float(POOL_WINDOWS[0]),
                    jnp.where(grp == 1, float(POOL_WINDOWS[1]),
                              jnp.where(grp == 2, float(POOL_WINDOWS[2]), float(POOL_WINDOWS[3]))))
    cnt = jnp.minimum((row + 1).astype(f32), win)
    pooled = sums / cnt - u
    ypool_ref[...] = jnp.dot(pooled.astype(bf16), pw_ref[...], preferred_element_type=f32) * ps_ref[...]

    lu = lru_ref[...]
    cbuf[pl.ds(CONV_HALO, ts), :] = lu
    xc = (cb_ref[...] + cw_ref[3:4, :] * lu
          + cw_ref[2:3, :] * cbuf[pl.ds(CONV_HALO - 1, ts), :]
          + cw_ref[1:2, :] * cbuf[pl.ds(CONV_HALO - 2, ts), :]
          + cw_ref[0:1, :] * cbuf[pl.ds(CONV_HALO - 3, ts), :])
    cbuf[pl.ds(0, CONV_HALO), :] = cbuf[pl.ds(ts, CONV_HALO), :]
    ri = jnp.dot(xc.astype(bf16), wri_ref[...], preferred_element_type=f32) + bri_ref[...]
    r = jax.nn.sigmoid(ri[:, :LRU_WIDTH])
    ig = jax.nn.sigmoid(ri[:, LRU_WIDTH:])
    log_a = -LRU_C * r * jax.nn.softplus(-lam_ref[...])
    a = jnp.exp(log_a)
    th = jnp.tanh(log_a)
    mult = jnp.sqrt(-2.0 * th / (1.0 - th))
    mult = jnp.where(row == 0, 1.0, mult)
    bb = mult * (ig * xc)

    def rows(ref, hf, s, n):
        return ref[hf, pl.ds(s, n, stride=8), :]

    for hf in range(LRU_WIDTH // LANES):
        a_s[hf] = a[:, hf * LANES:(hf + 1) * LANES]
        b_s[hf] = bb[:, hf * LANES:(hf + 1) * LANES]
        pa, pb = rows(a_s, hf, 0, n1), rows(b_s, hf, 0, n1)
        for s in range(1, 8):
            as_ = rows(a_s, hf, s, n1)
            pb = as_ * pb + rows(b_s, hf, s, n1)
            pa = as_ * pa
        a2_s[hf] = pa
        b2_s[hf] = pb
        pa, pb = rows(a2_s, hf, 0, n2), rows(b2_s, hf, 0, n2)
        for s in range(1, 8):
            as_ = rows(a2_s, hf, s, n2)
            pb = as_ * pb + rows(b2_s, hf, s, n2)
            pa = as_ * pa
        h = hc[hf, 0:1, :]
        for k in range(8):
            e3_s[hf, k:k + 1, :] = h
            h = pa[k:k + 1, :] * h + pb[k:k + 1, :]
        hc[hf, 0:1, :] = h
        prev = e3_s[hf]
        for s in range(8):
            e2_s[hf, pl.ds(s, n2, stride=8), :] = prev
            prev = rows(a2_s, hf, s, n2) * prev + rows(b2_s, hf, s, n2)
        prev = e2_s[hf]
        for s in range(8):
            prev = rows(a_s, hf, s, n1) * prev + rows(b_s, hf, s, n1)
            h_s[hf, pl.ds(s, n1, stride=8), :] = prev

    hfull = jnp.concatenate([h_s[hf] for hf in range(LRU_WIDTH // LANES)], axis=1)
    ylru_ref[...] = hfull * jax.nn.gelu(gate_ref[...])


def _mixer(pool_u, lru_u, gate, pw, ps, cw, cb, wri, bri, lam, batch, seq):
    ts = MIX_TILE
    nt = seq // ts
    t = batch * seq
    nh = LRU_WIDTH // LANES
    tile = lambda w_: pl.BlockSpec((ts, w_), lambda b, j: (b * nt + j, 0))
    full = lambda shape: pl.BlockSpec(shape, lambda b, j: (0,) * len(shape))
    return pl.pallas_call(
        functools.partial(_mixer_kernel, ts=ts),
        grid=(batch, nt),
        in_specs=[tile(POOL_WIDTH), tile(LRU_WIDTH), tile(LRU_WIDTH),
                  full((POOL_WIDTH, POOL_WIDTH)), full((1, POOL_WIDTH)),
                  full((4, LRU_WIDTH)), full((1, LRU_WIDTH)),
                  full((LRU_WIDTH, 2 * LRU_WIDTH)), full((1, 2 * LRU_WIDTH)), full((1, LRU_WIDTH))],
        out_specs=[tile(POOL_WIDTH), tile(LRU_WIDTH)],
        out_shape=[jax.ShapeDtypeStruct((t, POOL_WIDTH), f32), jax.ShapeDtypeStruct((t, LRU_WIDTH), f32)],
        scratch_shapes=[pltpu.VMEM((ts + POOL_HALO, POOL_WIDTH), f32)] * 4
        + [pltpu.VMEM((ts + CONV_HALO, LRU_WIDTH), f32)]
        + [pltpu.VMEM((nh, ts, LANES), f32)] * 3
        + [pltpu.VMEM((nh, ts // 8, LANES), f32)] * 3
        + [pltpu.VMEM((nh, 8, LANES), f32)] * 2,
        compiler_params=pltpu.CompilerParams(dimension_semantics=("arbitrary", "arbitrary"),
                                             vmem_limit_bytes=VMEM_LIMIT),
        name="pool_lru_mixer",
    )(pool_u, lru_u, gate, pw, ps, cw, cb, wri, bri, lam)


MAX_DIL = DILATIONS[-1]


def _attn_kernel(q_ref, k_ref, v_ref, cos_ref, sin_ref, qg_ref, kg_ref, o_ref,
                 qs, ks, vs, m_s, l_s, acc_s, *, seq):
    blk = ATTN_BLOCK
    cls = seq // MAX_DIL
    lane = lax.broadcasted_iota(i32, (cls, LANES), 1)
    head0 = lane < HEAD_DIM
    first_half = (lane % HEAD_DIM) < (HEAD_DIM // 2)

    def prep(r, carry):
        src_rows = pl.ds(r, cls, stride=MAX_DIL)
        dst_rows = pl.ds(pl.multiple_of(r * cls, cls), cls)
        cs = cos_ref[dst_rows, :]
        sn = sin_ref[dst_rows, :]
        for src, dst, g_ref, scale in ((q_ref, qs, qg_ref, HEAD_DIM ** -0.5), (k_ref, ks, kg_ref, 1.0)):
            x = src[src_rows, :]
            xx = x * x
            ms0 = jnp.sum(jnp.where(head0, xx, 0.0), axis=-1, keepdims=True)
            ms1 = jnp.sum(jnp.where(head0, 0.0, xx), axis=-1, keepdims=True)
            ms = jnp.where(head0, ms0, ms1) * (1.0 / HEAD_DIM)
            xn = x * lax.rsqrt(ms + NORM_EPS) * g_ref[...]
            partner = jnp.where(first_half,
                                pltpu.roll(xn, LANES - HEAD_DIM // 2, 1),
                                pltpu.roll(xn, HEAD_DIM // 2, 1))
            dst[dst_rows, :] = (xn * cs + partner * sn) * scale
        vs[dst_rows, :] = v_ref[src_rows, :]
        return carry

    lax.fori_loop(0, MAX_DIL, prep, 0)

    h0 = lax.broadcasted_iota(i32, (blk, LANES), 1) < HEAD_DIM
    state = (m_s, l_s, acc_s)
    qi = lax.broadcasted_iota(i32, (2 * blk, 2 * blk), 0) % blk
    kj = lax.broadcasted_iota(i32, (2 * blk, 2 * blk), 1)

    for bi, d in enumerate(DILATIONS):
        nb = seq // (blk * d)
        runs = MAX_DIL // d
        run = blk // runs

        def member(i, runs=runs, run=run):
            return runs * (i % run) + i // run

        jq = member(qi)
        jk = member(kj % blk)
        mask_cur = (kj >= blk) & (jk <= jq)
        mask_prev = (kj < blk) & (jk >= jq)

        def starts(res, n, d=d, runs=runs, run=run):
            return [pl.multiple_of((res + d * b) * cls + run * n, 8) for b in range(runs)]

        def ld(ref, row_starts, run=run):
            return jnp.concatenate([ref[pl.ds(r0, run), :] for r0 in row_starts], axis=0)

        def st(ref, row_starts, val, run=run):
            for b, r0 in enumerate(row_starts):
                ref[pl.ds(r0, run), :] = val[b * run:(b + 1) * run, :]

        def body(it, carry, nb=nb, bi=bi, starts=starts, ld=ld, st=st, mask_cur=mask_cur, mask_prev=mask_prev):
            loaded = []
            for u_ in range(ATTN_INTERLEAVE):
                idx = it * ATTN_INTERLEAVE + u_
                res = idx // nb
                n = idx % nb
                cur = starts(res, n)
                prev = starts(res, jnp.maximum(n - 1, 0))
                qb = ld(qs, cur)
                kb = jnp.concatenate([ld(ks, prev), ld(ks, cur)], axis=0).astype(bf16)
                vb = jnp.concatenate([ld(vs, prev), ld(vs, cur)], axis=0).astype(bf16)
                old = [ld(ref, cur) for ref in state] if bi > 0 else None
                loaded.append((n, cur, qb, kb, vb, old))
            results = []
            for n, cur, qb, kb, vb, old in loaded:
                valid = mask_cur | (mask_prev & (n > 0))
                q2 = jnp.concatenate([jnp.where(h0, qb, 0.0), jnp.where(h0, 0.0, qb)], axis=0).astype(bf16)
                s = lax.dot_general(q2, kb, (((1,), (1,)), ((), ())), preferred_element_type=f32)
                s = jnp.where(valid, s, NEG_INF)
                mb = jnp.max(s, axis=-1, keepdims=True)
                p = jnp.exp(s - mb)
                lb = jnp.sum(p, axis=-1, keepdims=True)
                c = jnp.dot(p.astype(bf16), vb, preferred_element_type=f32)
                m_blk = jnp.where(h0, mb[:blk], mb[blk:])
                l_blk = jnp.where(h0, lb[:blk], lb[blk:])
                n_blk = jnp.where(h0, c[:blk], c[blk:])
                if old is None:
                    results.append((cur, m_blk, l_blk, n_blk))
                else:
                    m_old, l_old, a_old = old
                    m_new = jnp.maximum(m_old, m_blk)
                    w_old = jnp.exp(m_old - m_new)
                    w_blk = jnp.exp(m_blk - m_new)
                    results.append((cur, m_new, w_old * l_old + w_blk * l_blk, w_old * a_old + w_blk * n_blk))
            for cur, *vals in results:
                for ref, val in zip(state, vals):
                    st(ref, cur, val)
            return carry

        lax.fori_loop(0, seq // (blk * ATTN_INTERLEAVE), body, 0)

    def fin(r, carry):
        rows = pl.ds(pl.multiple_of(r * cls, cls), cls)
        o_ref[pl.ds(r, cls, stride=MAX_DIL), :] = acc_s[rows, :] / l_s[rows, :]
        return carry

    lax.fori_loop(0, MAX_DIL, fin, 0)


def _attention(q, k, v, cos_t, sin_t, qg, kg, batch, seq):
    t = batch * seq
    npair = ATTN_WIDTH // LANES
    tile = pl.BlockSpec((seq, LANES), lambda b, p: (b, p))
    table = pl.BlockSpec((seq, LANES), lambda b, p: (0, 0))
    gain = pl.BlockSpec((1, LANES), lambda b, p: (0, 0))
    return pl.pallas_call(
        functools.partial(_attn_kernel, seq=seq),
        grid=(batch, npair),
        in_specs=[tile, tile, tile, table, table, gain, gain],
        out_specs=tile,
        out_shape=jax.ShapeDtypeStruct((t, ATTN_WIDTH), f32),
        scratch_shapes=[pltpu.VMEM((seq, LANES), f32)] * 6,
        compiler_params=pltpu.CompilerParams(dimension_semantics=("arbitrary", "arbitrary"),
                                             vmem_limit_bytes=VMEM_LIMIT),
        name="dilated_attention",
    )(q, k, v, cos_t, sin_t, qg, kg)


def _out_proj_kernel(x_ref, yp_ref, yl_ref, ya_ref, og_ref, wo_ref, g2_ref, rw_ref, rb_ref,
                     xo_ref, idx_ref, gate_ref, rank_ref, cnt_ref, cnt_s, *, tm, steps_per_tile):
    i = pl.program_id(0)
    c0, c1 = POOL_WIDTH, POOL_WIDTH + LRU_WIDTH
    acc = x_ref[...]
    for y_ref, lo, hi in ((yp_ref, 0, c0), (yl_ref, c0, c1), (ya_ref, c1, D_MODEL)):
        mix = _rms(y_ref[...], og_ref[:, lo:hi]).astype(bf16)
        acc = acc + jnp.dot(mix, wo_ref[lo:hi, :], preferred_element_type=f32)
    xo_ref[...] = acc

    h2 = _rms(acc, g2_ref[...])
    logits = lax.dot_general(rw_ref[...], h2, (((1,), (1,)), ((), ())), precision=lax.Precision.HIGHEST,
                             preferred_element_type=f32) + rb_ref[...]
    eio = lax.broadcasted_iota(i32, (N_EXPERTS, tm), 0)
    cur = logits
    vals, idxs = [], []
    for _ in range(TOP_K):
        mx = jnp.max(cur, axis=0, keepdims=True)
        ix = jnp.min(jnp.where(cur == mx, eio, N_EXPERTS), axis=0, keepdims=True)
        vals.append(mx)
        idxs.append(ix)
        cur = jnp.where(eio == ix, -jnp.inf, cur)
    ex = [jnp.exp(v - vals[0]) for v in vals]
    den = ex[0] + ex[1] + ex[2] + ex[3]

    @pl.when(i % steps_per_tile == 0)
    def _():
        cnt_s[...] = jnp.zeros_like(cnt_s)

    onehot = jnp.zeros((N_EXPERTS, tm), f32)
    for ix in idxs:
        onehot = onehot + (eio == ix).astype(f32)
    before = (lax.broadcasted_iota(i32, (tm, tm), 0) < lax.broadcasted_iota(i32, (tm, tm), 1)).astype(bf16)
    prefix = jnp.dot(onehot.astype(bf16), before, preferred_element_type=f32) + cnt_s[:, 0:1]
    for k in range(TOP_K):
        idx_ref[k:k + 1, :] = idxs[k]
        gate_ref[k:k + 1, :] = ex[k] / den
        rank_ref[k:k + 1, :] = jnp.sum(jnp.where(eio == idxs[k], prefix, 0.0), axis=0, keepdims=True).astype(i32)
    cnt_s[...] = cnt_s[...] + jnp.sum(onehot, axis=1, keepdims=True)
    cnt_ref[0] = cnt_s[...]


def _out_proj(x, yp, yl, ya, og, wo, g2, rw_t, rb):
    t = x.shape[0]
    tm = PROJ_TILE
    steps = t // tm
    tile = lambda w_: pl.BlockSpec((tm, w_), lambda i: (i, 0))
    full = lambda shape: pl.BlockSpec(shape, lambda i: (0,) * len(shape))
    route = pl.BlockSpec((TOP_K, tm), lambda i: (0, i))
    return pl.pallas_call(
        functools.partial(_out_proj_kernel, tm=tm, steps_per_tile=MOE_TILE // tm),
        grid=(steps,),
        in_specs=[tile(D_MODEL), tile(POOL_WIDTH), tile(LRU_WIDTH), tile(ATTN_WIDTH),
                  full((1, D_MODEL)), full((D_MODEL, D_MODEL)), full((1, D_MODEL)),
                  full((N_EXPERTS, D_MODEL)), full((N_EXPERTS, 1))],
        out_specs=[tile(D_MODEL), route, route, route,
                   pl.BlockSpec((1, N_EXPERTS, LANES), lambda i: (i, 0, 0))],
        out_shape=[jax.ShapeDtypeStruct((t, D_MODEL), f32),
                   jax.ShapeDtypeStruct((TOP_K, t), i32),
                   jax.ShapeDtypeStruct((TOP_K, t), f32),
                   jax.ShapeDtypeStruct((TOP_K, t), i32),
                   jax.ShapeDtypeStruct((steps, N_EXPERTS, LANES), f32)],
        scratch_shapes=[pltpu.VMEM((N_EXPERTS, LANES), f32)],
        compiler_params=pltpu.CompilerParams(dimension_semantics=("arbitrary",),
                                             vmem_limit_bytes=VMEM_LIMIT),
        name="out_proj_router",
    )(x, yp, yl, ya, og, wo, g2, rw_t, rb)


ROW_SLAB = 8


def _slab_cols(ref, first_row, n_rows, col):
    return ref[pl.ds(first_row * ROW_SLAB + col, n_rows, stride=ROW_SLAB), :]


def _moe_kernel(off_ref, pos_ref, gate_ref, x_ref, g2_ref, wgu_ref, bgu_ref, wd_ref, bd_ref,
                o_ref, sorted_s, stage_s, *, tt, ch, io_rows):
    i = pl.program_id(0)
    s = pl.program_id(1)
    n_io = tt // io_rows
    base = i * (N_EXPERTS + 1)
    unroll = 8
    n_col = D_MODEL // LANES

    def slab(ref, r):
        return ref.at[pl.ds(pl.multiple_of(r * ROW_SLAB, ROW_SLAB), ROW_SLAB), :]

    @pl.when(s < n_io)
    def _dispatch():
        @pl.when(s == 0)
        def _():
            sorted_s[pl.ds(TOP_K * tt * ROW_SLAB, ch * ROW_SLAB), :] = jnp.zeros((ch * ROW_SLAB, LANES), f32)

        h = _rms(x_ref[...], g2_ref[...])
        for j in range(n_col):
            stage_s[pl.ds(j, io_rows, stride=ROW_SLAB), :] = h[:, j * LANES:(j + 1) * LANES]

        def scatter(c, carry):
            for u_ in range(unroll):
                tok = c * unroll + u_
                row = slab(stage_s, tok)[...]
                for k in range(TOP_K):
                    slab(sorted_s, pos_ref[tok * TOP_K + k])[...] = row
            return carry

        lax.fori_loop(0, io_rows // unroll, scatter, 0)

    @pl.when((s >= n_io) & (s < n_io + N_EXPERTS))
    def _experts():
        e = s - n_io
        lo = off_ref[base + e]
        hi = off_ref[base + e + 1]

        def chunk(c, carry):
            start = lo + c * ch
            xc = jnp.concatenate([_slab_cols(sorted_s, start, ch, j) for j in range(n_col)], axis=1)
            gu = jnp.dot(xc.astype(bf16), wgu_ref[0], preferred_element_type=f32) + bgu_ref[0]
            gate = jnp.minimum(gu[:, :EXPERT_FF], SWIGLU_LIMIT)
            up = jnp.clip(gu[:, EXPERT_FF:], -SWIGLU_LIMIT, SWIGLU_LIMIT)
            act = (up + 1.0) * gate * jax.nn.sigmoid(SWIGLU_ALPHA * gate)
            y = jnp.dot(act.astype(bf16), wd_ref[0], preferred_element_type=f32) + bd_ref[0]
            mine = (lax.broadcasted_iota(i32, (ch, 1), 0) + start) < hi
            y = jnp.where(mine, y, xc)
            for j in range(n_col):
                sorted_s[pl.ds(start * ROW_SLAB + j, ch, stride=ROW_SLAB), :] = y[:, j * LANES:(j + 1) * LANES]
            return carry

        lax.fori_loop(0, (hi - lo + ch - 1) // ch, chunk, 0)

    @pl.when(s >= n_io + N_EXPERTS)
    def _combine():
        def gather(c, carry):
            for u_ in range(unroll):
                tok = c * unroll + u_
                acc = gate_ref[tok * TOP_K] * slab(sorted_s, pos_ref[tok * TOP_K])[...]
                for k in range(1, TOP_K):
                    acc = acc + gate_ref[tok * TOP_K + k] * slab(sorted_s, pos_ref[tok * TOP_K + k])[...]
                slab(stage_s, tok)[...] = acc
            return carry

        lax.fori_loop(0, io_rows // unroll, gather, 0)
        moe = jnp.concatenate([_slab_cols(stage_s, 0, io_rows, j) for j in range(n_col)], axis=1)
        o_ref[...] = x_ref[...] + moe


def _moe(x, pos, gates, offsets, g2, wgu, bgu, wd, bd):
    t = x.shape[0]
    tt = MOE_TILE
    ch = MOE_CHUNK
    io_rows = PROJ_TILE
    n_io = tt // io_rows
    n_steps = 2 * n_io + N_EXPERTS

    def tok_idx(i, s, off):
        in_combine = s >= n_io + N_EXPERTS
        return i * n_io + jnp.where(s < n_io, s, jnp.where(in_combine, s - n_io - N_EXPERTS, n_io - 1))

    def out_idx(i, s, off):
        return i * n_io + jnp.maximum(s - n_io - N_EXPERTS, 0)

    def expert_idx(i, s, off):
        return jnp.clip(s - n_io, 0, N_EXPERTS - 1)

    route = pl.BlockSpec((TOP_K * io_rows,), lambda i, s, off: (tok_idx(i, s, off),), memory_space=pltpu.SMEM)
    return pl.pallas_call(
        functools.partial(_moe_kernel, tt=tt, ch=ch, io_rows=io_rows),
        grid_spec=pltpu.PrefetchScalarGridSpec(
            num_scalar_prefetch=1,
            grid=(t // tt, n_steps),
            in_specs=[route, route,
                      pl.BlockSpec((io_rows, D_MODEL), lambda i, s, off: (tok_idx(i, s, off), 0)),
                      pl.BlockSpec((1, D_MODEL), lambda i, s, off: (0, 0)),
                      pl.BlockSpec((1, D_MODEL, 2 * EXPERT_FF), lambda i, s, off: (expert_idx(i, s, off), 0, 0)),
                      pl.BlockSpec((1, 1, 2 * EXPERT_FF), lambda i, s, off: (expert_idx(i, s, off), 0, 0)),
                      pl.BlockSpec((1, EXPERT_FF, D_MODEL), lambda i, s, off: (expert_idx(i, s, off), 0, 0)),
                      pl.BlockSpec((1, 1, D_MODEL), lambda i, s, off: (expert_idx(i, s, off), 0, 0))],
            out_specs=pl.BlockSpec((io_rows, D_MODEL), lambda i, s, off: (out_idx(i, s, off), 0)),
            scratch_shapes=[pltpu.VMEM(((TOP_K * tt + ch) * ROW_SLAB, LANES), f32),
                            pltpu.VMEM((io_rows * ROW_SLAB, LANES), f32)]),
        out_shape=jax.ShapeDtypeStruct((t, D_MODEL), f32),
        compiler_params=pltpu.CompilerParams(dimension_semantics=("arbitrary", "arbitrary"),
                                             vmem_limit_bytes=VMEM_LIMIT),
        name="moe_experts",
    )(offsets, pos, gates, x, g2, wgu, bgu, wd, bd)


def _rope_tables(seq):
    pos = jnp.arange(seq, dtype=f32)
    inv_freq = ROPE_THETA ** (-jnp.arange(0, HEAD_DIM, 2, dtype=f32) / HEAD_DIM)
    ang = pos[:, None] * inv_freq[None, :]
    cos, sin = jnp.cos(ang), jnp.sin(ang)
    reps = LANES // HEAD_DIM
    cos_t = jnp.tile(jnp.concatenate([cos, cos], axis=1), (1, reps))
    sin_t = jnp.tile(jnp.concatenate([-sin, sin], axis=1), (1, reps))
    to_cls = lambda tab: tab.reshape(seq // MAX_DIL, MAX_DIL, LANES).transpose(1, 0, 2).reshape(seq, LANES)
    return to_cls(cos_t), to_cls(sin_t)


def _layer(x, cos_t, sin_t, batch, seq, p):
    t = batch * seq
    row = lambda v: v.reshape(1, -1)
    pool_u, lru_u, gate, q, k, v = _in_proj(x, row(p["norm1_g"]), p["w_in"].astype(bf16))
    wri = jnp.concatenate([block_diag(*p["lru_wa"]), block_diag(*p["lru_wx"])], axis=1).astype(bf16)
    bri = jnp.concatenate([p["lru_ba"], p["lru_bx"]]).reshape(1, -1)
    y_pool, y_lru = _mixer(pool_u, lru_u, gate, block_diag(*p["pool_w"]).astype(bf16), row(p["pool_scale"]),
                           p["conv_w"], row(p["conv_b"]), wri, bri, row(p["lru_lambda"]), batch, seq)
    reps = LANES // HEAD_DIM
    y_attn = _attention(q, k, v, cos_t, sin_t, row(jnp.tile(p["q_norm_g"], reps)),
                        row(jnp.tile(p["k_norm_g"], reps)), batch, seq)
    x, idx, gates, rank, cnt = _out_proj(x, y_pool, y_lru, y_attn, row(p["out_norm_g"]), p["w_out"].astype(bf16),
                                         row(p["norm2_g"]), p["router_w"].T, p["router_b"].reshape(-1, 1))
    steps_per_tile = MOE_TILE // PROJ_TILE
    n_tiles = t // MOE_TILE
    counts = cnt[steps_per_tile - 1::steps_per_tile, :, 0].astype(i32)
    offsets = jnp.concatenate([jnp.zeros((n_tiles, 1), i32), jnp.cumsum(counts, axis=1)], axis=1)
    idx_t = idx.reshape(TOP_K, n_tiles, MOE_TILE)
    seg_start = jnp.zeros_like(idx_t)
    for e in range(N_EXPERTS):
        seg_start = jnp.where(idx_t == e, offsets[None, :, e, None], seg_start)
    pos = rank + seg_start.reshape(TOP_K, t)
    return _moe(x, pos.T.reshape(-1), gates.T.reshape(-1), offsets.reshape(-1), row(p["norm2_g"]),
                p["w_gate_up"].astype(bf16), p["b_gate_up"][:, None, :],
                p["w_down"].astype(bf16), p["b_down"][:, None, :])


def kernel(x, norm1_g, w_in, pool_w, pool_scale, conv_w, conv_b, lru_wa, lru_ba, lru_wx, lru_bx, lru_lambda,
           q_norm_g, k_norm_g, out_norm_g, w_out, norm2_g, router_w, router_b, w_gate_up, b_gate_up,
           w_down, b_down):
    batch, seq, d = x.shape
    params = dict(norm1_g=norm1_g, w_in=w_in, pool_w=pool_w, pool_scale=pool_scale, conv_w=conv_w, conv_b=conv_b,
                  lru_wa=lru_wa, lru_ba=lru_ba, lru_wx=lru_wx, lru_bx=lru_bx, lru_lambda=lru_lambda,
                  q_norm_g=q_norm_g, k_norm_g=k_norm_g, out_norm_g=out_norm_g, w_out=w_out, norm2_g=norm2_g,
                  router_w=router_w, router_b=router_b, w_gate_up=w_gate_up, b_gate_up=b_gate_up,
                  w_down=w_down, b_down=b_down)
    cos_t, sin_t = _rope_tables(seq)
    xt = x.reshape(batch * seq, d)
    for layer in range(norm1_g.shape[0]):
        xt = _layer(xt, cos_t, sin_t, batch, seq, {name: val[layer] for name, val in params.items()})
    return xt.reshape(batch, seq, d)
```

```python
import functools

import jax
import jax.numpy as jnp
from jax import lax
from jax.experimental import pallas as pl
from jax.experimental.pallas import tpu as pltpu
from jax.scipy.linalg import block_diag

f32 = jnp.float32
bf16 = jnp.bfloat16
i32 = jnp.int32

D_MODEL = 1024
POOL_WIDTH = 256
POOL_WINDOWS = (2, 4, 8, 16)
LRU_WIDTH = 256
LRU_C = 8.0
ATTN_WIDTH = 512
HEAD_DIM = 64
DILATIONS = (1, 4, 16)
ATTN_BLOCK = 128
ROPE_THETA = 10000.0
N_EXPERTS = 32
TOP_K = 4
EXPERT_FF = 256
SWIGLU_LIMIT = 7.0
SWIGLU_ALPHA = 1.702
NORM_EPS = 1e-6
NEG_INF = -1e30

LANES = 128
POOL_HALO = 16
CONV_HALO = 8
VMEM_LIMIT = 56 * 1024 * 1024

PROJ_TILE = 512
MIX_TILE = 512
MOE_TILE = 2048
MOE_CHUNK = 256
ATTN_INTERLEAVE = 4


def _rms(x, g):
    return x * lax.rsqrt(jnp.mean(x * x, axis=-1, keepdims=True) + NORM_EPS) * g


_IN_SPLITS = (0, 256, 512, 768, 1280, 1792, 2304)


def _qk_norm_rope(y, cs, sn, gain, scale):
    lane = lax.broadcasted_iota(i32, y.shape, 1)
    head0 = lane < HEAD_DIM
    first_half = (lane % HEAD_DIM) < (HEAD_DIM // 2)
    yy = y * y
    ms0 = jnp.sum(jnp.where(head0, yy, 0.0), axis=-1, keepdims=True)
    ms1 = jnp.sum(jnp.where(head0, 0.0, yy), axis=-1, keepdims=True)
    ms = jnp.where(head0, ms0, ms1) * (1.0 / HEAD_DIM)
    yn = y * lax.rsqrt(ms + NORM_EPS) * gain
    partner = jnp.where(first_half, pltpu.roll(yn, LANES - HEAD_DIM // 2, 1), pltpu.roll(yn, HEAD_DIM // 2, 1))
    return (yn * cs + partner * sn) * scale


def _in_proj_kernel(x_ref, g_ref, w_ref, cos_ref, sin_ref, qg_ref, kg_ref, *out_refs):
    h = _rms(x_ref[...], g_ref[...]).astype(bf16)

    def proj(n):
        return jnp.dot(h, w_ref[:, _IN_SPLITS[n]:_IN_SPLITS[n + 1]], preferred_element_type=f32)

    def norm_rope(n, y, gain_ref, scale):
        for c in range(0, y.shape[1], LANES):
            out_refs[n][:, c:c + LANES] = _qk_norm_rope(y[:, c:c + LANES], cos_ref[...], sin_ref[...],
                                                        gain_ref[...], scale)

    q = proj(3)
    k = proj(4)
    norm_rope(3, q, qg_ref, HEAD_DIM ** -0.5)
    out_refs[5][...] = proj(5)
    out_refs[0][...] = proj(0)
    norm_rope(4, k, kg_ref, 1.0)
    out_refs[1][...] = proj(1)
    out_refs[2][...] = proj(2)


def _in_proj(x, g, w, cos_t, sin_t, qg, kg, seq):
    t = x.shape[0]
    tm = PROJ_TILE
    widths = [c1 - c0 for c0, c1 in zip(_IN_SPLITS[:-1], _IN_SPLITS[1:])]
    table = pl.BlockSpec((tm, LANES), lambda i: (i % (seq // tm), 0))
    gain = pl.BlockSpec((1, LANES), lambda i: (0, 0))
    return pl.pallas_call(
        _in_proj_kernel,
        grid=(t // tm,),
        in_specs=[pl.BlockSpec((tm, D_MODEL), lambda i: (i, 0)),
                  pl.BlockSpec((1, D_MODEL), lambda i: (0, 0)),
                  pl.BlockSpec((D_MODEL, _IN_SPLITS[-1]), lambda i: (0, 0)),
                  table, table, gain, gain],
        out_specs=[pl.BlockSpec((tm, w_), lambda i: (i, 0)) for w_ in widths],
        out_shape=[jax.ShapeDtypeStruct((t, w_), f32) for w_ in widths],
        compiler_params=pltpu.CompilerParams(dimension_semantics=("arbitrary",),
                                             vmem_limit_bytes=VMEM_LIMIT),
        name="in_proj",
    )(x, g, w, cos_t, sin_t, qg, kg)


def _mixer_kernel(pool_ref, lru_ref, gate_ref, pw_ref, ps_ref, cw_ref, cb_ref, wri_ref, bri_ref, lam_ref,
                  ypool_ref, ylru_ref,
                  b0, b1, b2, b3, cbuf, a_s, b_s, h_s, a2_s, b2_s, e2_s, e3_s, hc, *, ts):
    j = pl.program_id(1)
    n1 = ts // 8
    n2 = n1 // 8

    @pl.when(j == 0)
    def _():
        for buf in (b0, b1, b2, b3):
            buf[pl.ds(0, POOL_HALO), :] = jnp.zeros((POOL_HALO, POOL_WIDTH), f32)
        cbuf[pl.ds(0, CONV_HALO), :] = jnp.zeros((CONV_HALO, LRU_WIDTH), f32)
        hc[...] = jnp.zeros_like(hc)

    row = lax.broadcasted_iota(i32, (ts, POOL_WIDTH), 0) + j * ts
    grp = lax.broadcasted_iota(i32, (ts, POOL_WIDTH), 1) // (POOL_WIDTH // len(POOL_WINDOWS))

    u = pool_ref[...]
    b0[pl.ds(POOL_HALO, ts), :] = u
    s1 = u + b0[pl.ds(POOL_HALO - 1, ts), :]
    b1[pl.ds(POOL_HALO, ts), :] = s1
    s2 = s1 + b1[pl.ds(POOL_HALO - 2, ts), :]
    b2[pl.ds(POOL_HALO, ts), :] = s2
    s3 = s2 + b2[pl.ds(POOL_HALO - 4, ts), :]
    b3[pl.ds(POOL_HALO, ts), :] = s3
    s4 = s3 + b3[pl.ds(POOL_HALO - 8, ts), :]
    for buf in (b0, b1, b2, b3):
        buf[pl.ds(0, POOL_HALO), :] = buf[pl.ds(ts, POOL_HALO), :]
    sums = jnp.where(grp == 0, s1, jnp.where(grp == 1, s2, jnp.where(grp == 2, s3, s4)))
    win = jnp.where(grp == 0, float(POOL_WINDOWS[0]),
                    jnp.where(grp == 1, float(POOL_WINDOWS[1]),
                              jnp.where(grp == 2, float(POOL_WINDOWS[2]), float(POOL_WINDOWS[3]))))
    cnt = jnp.minimum((row + 1).astype(f32), win)
    pooled = sums / cnt - u
    ypool_ref[...] = jnp.dot(pooled.astype(bf16), pw_ref[...], preferred_element_type=f32) * ps_ref[...]

    lu = lru_ref[...]
    cbuf[pl.ds(CONV_HALO, ts), :] = lu
    xc = (cb_ref[...] + cw_ref[3:4, :] * lu
          + cw_ref[2:3, :] * cbuf[pl.ds(CONV_HALO - 1, ts), :]
          + cw_ref[1:2, :] * cbuf[pl.ds(CONV_HALO - 2, ts), :]
          + cw_ref[0:1, :] * cbuf[pl.ds(CONV_HALO - 3, ts), :])
    cbuf[pl.ds(0, CONV_HALO), :] = cbuf[pl.ds(ts, CONV_HALO), :]
    ri = jnp.dot(xc.astype(bf16), wri_ref[...], preferred_element_type=f32) + bri_ref[...]
    r = jax.nn.sigmoid(ri[:, :LRU_WIDTH])
    ig = jax.nn.sigmoid(ri[:, LRU_WIDTH:])
    log_a = -LRU_C * r * jax.nn.softplus(-lam_ref[...])
    a = jnp.exp(log_a)
    th = jnp.tanh(log_a)
    mult = jnp.sqrt(-2.0 * th / (1.0 - th))
    mult = jnp.where(row == 0, 1.0, mult)
    bb = mult * (ig * xc)

    def rows(ref, hf, s, n):
        return ref[hf, pl.ds(s, n, stride=8), :]

    for hf in range(LRU_WIDTH // LANES):
        a_s[hf] = a[:, hf * LANES:(hf + 1) * LANES]
        b_s[hf] = bb[:, hf * LANES:(hf + 1) * LANES]
        pa, pb = rows(a_s, hf, 0, n1), rows(b_s, hf, 0, n1)
        for s in range(1, 8):
            as_ = rows(a_s, hf, s, n1)
            pb = as_ * pb + rows(b_s, hf, s, n1)
            pa = as_ * pa
        a2_s[hf] = pa
        b2_s[hf] = pb
        pa, pb = rows(a2_s, hf, 0, n2), rows(b2_s, hf, 0, n2)
        for s in range(1, 8):
            as_ = rows(a2_s, hf, s, n2)
            pb = as_ * pb + rows(b2_s, hf, s, n2)
            pa = as_ * pa
        h = hc[hf, 0:1, :]
        for k in range(8):
            e3_s[hf, k:k + 1, :] = h
            h = pa[k:k + 1, :] * h + pb[k:k + 1, :]
        hc[hf, 0:1, :] = h
        prev = e3_s[hf]
        for s in range(8):
            e2_s[hf, pl.ds(s, n2, stride=8), :] = prev
            prev = rows(a2_s, hf, s, n2) * prev + rows(b2_s, hf, s, n2)
        prev = e2_s[hf]
        for s in range(8):
            prev = rows(a_s, hf, s, n1) * prev + rows(b_s, hf, s, n1)
            h_s[hf, pl.ds(s, n1, stride=8), :] = prev

    hfull = jnp.concatenate([h_s[hf] for hf in range(LRU_WIDTH // LANES)], axis=1)
    ylru_ref[...] = hfull * jax.nn.gelu(gate_ref[...])


def _mixer(pool_u, lru_u, gate, pw, ps, cw, cb, wri, bri, lam, batch, seq):
    ts = MIX_TILE
    nt = seq // ts
    t = batch * seq
    nh = LRU_WIDTH // LANES
    tile = lambda w_: pl.BlockSpec((ts, w_), lambda b, j: (b * nt + j, 0))
    full = lambda shape: pl.BlockSpec(shape, lambda b, j: (0,) * len(shape))
    return pl.pallas_call(
        functools.partial(_mixer_kernel, ts=ts),
        grid=(batch, nt),
        in_specs=[tile(POOL_WIDTH), tile(LRU_WIDTH), tile(LRU_WIDTH),
                  full((POOL_WIDTH, POOL_WIDTH)), full((1, POOL_WIDTH)),
                  full((4, LRU_WIDTH)), full((1, LRU_WIDTH)),
                  full((LRU_WIDTH, 2 * LRU_WIDTH)), full((1, 2 * LRU_WIDTH)), full((1, LRU_WIDTH))],
        out_specs=[tile(POOL_WIDTH), tile(LRU_WIDTH)],
        out_shape=[jax.ShapeDtypeStruct((t, POOL_WIDTH), f32), jax.ShapeDtypeStruct((t, LRU_WIDTH), f32)],
        scratch_shapes=[pltpu.VMEM((ts + POOL_HALO, POOL_WIDTH), f32)] * 4
        + [pltpu.VMEM((ts + CONV_HALO, LRU_WIDTH), f32)]
        + [pltpu.VMEM((nh, ts, LANES), f32)] * 3
        + [pltpu.VMEM((nh, ts // 8, LANES), f32)] * 3
        + [pltpu.VMEM((nh, 8, LANES), f32)] * 2,
        compiler_params=pltpu.CompilerParams(dimension_semantics=("arbitrary", "arbitrary"),
                                             vmem_limit_bytes=VMEM_LIMIT),
        name="pool_lru_mixer",
    )(pool_u, lru_u, gate, pw, ps, cw, cb, wri, bri, lam)


MAX_DIL = DILATIONS[-1]


def _attn_kernel(q_ref, k_ref, v_ref, o_ref, qs, ks, vs, m_s, l_s, acc_s, *, seq):
    blk = ATTN_BLOCK
    cls = seq // MAX_DIL

    def prep(r, carry):
        dst_rows = pl.ds(pl.multiple_of(r * cls, cls), cls)
        for src, dst in ((q_ref, qs), (k_ref, ks), (v_ref, vs)):
            dst[dst_rows, :] = src[pl.ds(r, cls, stride=MAX_DIL), :]
        return carry

    lax.fori_loop(0, MAX_DIL, prep, 0)

    h0 = lax.broadcasted_iota(i32, (blk, LANES), 1) < HEAD_DIM
    state = (m_s, l_s, acc_s)
    qi = lax.broadcasted_iota(i32, (2 * blk, 2 * blk), 0) % blk
    kj = lax.broadcasted_iota(i32, (2 * blk, 2 * blk), 1)

    for bi, d in enumerate(DILATIONS):
        nb = seq // (blk * d)
        runs = MAX_DIL // d
        run = blk // runs

        def member(i, runs=runs, run=run):
            return runs * (i % run) + i // run

        jq = member(qi)
        jk = member(kj % blk)
        mask_cur = (kj >= blk) & (jk <= jq)
        mask_prev = (kj < blk) & (jk >= jq)

        def starts(res, n, d=d, runs=runs, run=run):
            return [pl.multiple_of((res + d * b) * cls + run * n, 8) for b in range(runs)]

        def ld(ref, row_starts, run=run):
            return jnp.concatenate([ref[pl.ds(r0, run), :] for r0 in row_starts], axis=0)

        def st(ref, row_starts, val, run=run):
            for b, r0 in enumerate(row_starts):
                ref[pl.ds(r0, run), :] = val[b * run:(b + 1) * run, :]

        def body(it, carry, nb=nb, bi=bi, starts=starts, ld=ld, st=st, mask_cur=mask_cur, mask_prev=mask_prev):
            loaded = []
            for u_ in range(ATTN_INTERLEAVE):
                idx = it * ATTN_INTERLEAVE + u_
                res = idx // nb
                n = idx % nb
                cur = starts(res, n)
                prev = starts(res, jnp.maximum(n - 1, 0))
                qb = ld(qs, cur)
                kb = jnp.concatenate([ld(ks, prev), ld(ks, cur)], axis=0).astype(bf16)
                vb = jnp.concatenate([ld(vs, prev), ld(vs, cur)], axis=0).astype(bf16)
                loaded.append((n, cur, qb, kb, vb))
            scores = []
            for n, cur, qb, kb, vb in loaded:
                q2 = jnp.concatenate([jnp.where(h0, qb, 0.0), jnp.where(h0, 0.0, qb)], axis=0).astype(bf16)
                scores.append(lax.dot_general(q2, kb, (((1,), (1,)), ((), ())), preferred_element_type=f32))
            results = []
            for (n, cur, qb, kb, vb), s in zip(loaded, scores):
                valid = mask_cur | (mask_prev & (n > 0))
                s = jnp.where(valid, s, NEG_INF)
                mb = jnp.max(s, axis=-1, keepdims=True)
                p = jnp.exp(s - mb)
                lb = jnp.sum(p, axis=-1, keepdims=True)
                c = jnp.dot(p.astype(bf16), vb, preferred_element_type=f32)
                m_blk = jnp.where(h0, mb[:blk], mb[blk:])
                l_blk = jnp.where(h0, lb[:blk], lb[blk:])
                n_blk = jnp.where(h0, c[:blk], c[blk:])
                if bi == 0:
                    results.append((cur, m_blk, l_blk, n_blk))
                else:
                    m_old, l_old, a_old = [ld(ref, cur) for ref in state]
                    m_new = jnp.maximum(m_old, m_blk)
                    w_old = jnp.exp(m_old - m_new)
                    w_blk = jnp.exp(m_blk - m_new)
                    results.append((cur, m_new, w_old * l_old + w_blk * l_blk, w_old * a_old + w_blk * n_blk))
            for cur, *vals in results:
                for ref, val in zip(state, vals):
                    st(ref, cur, val)
            return carry

        lax.fori_loop(0, seq // (blk * ATTN_INTERLEAVE), body, 0)

    def fin(r, carry):
        rows = pl.ds(pl.multiple_of(r * cls, cls), cls)
        o_ref[pl.ds(r, cls, stride=MAX_DIL), :] = acc_s[rows, :] / l_s[rows, :]
        return carry

    lax.fori_loop(0, MAX_DIL, fin, 0)


def _attention(q, k, v, batch, seq):
    t = batch * seq
    npair = ATTN_WIDTH // LANES
    tile = pl.BlockSpec((seq, LANES), lambda b, p: (b, p))
    return pl.pallas_call(
        functools.partial(_attn_kernel, seq=seq),
        grid=(batch, npair),
        in_specs=[tile, tile, tile],
        out_specs=tile,
        out_shape=jax.ShapeDtypeStruct((t, ATTN_WIDTH), f32),
        scratch_shapes=[pltpu.VMEM((seq, LANES), f32)] * 6,
        compiler_params=pltpu.CompilerParams(dimension_semantics=("arbitrary", "arbitrary"),
                                             vmem_limit_bytes=VMEM_LIMIT),
        name="dilated_attention",
    )(q, k, v)


def _out_proj_kernel(x_ref, yp_ref, yl_ref, ya_ref, og_ref, wo_ref, g2_ref, rw_ref, rb_ref,
                     xo_ref, idx_ref, gate_ref, rank_ref, cnt_ref, cnt_s, before_s, *, tm, steps_per_tile):
    i = pl.program_id(0)
    c0, c1 = POOL_WIDTH, POOL_WIDTH + LRU_WIDTH
    acc = x_ref[...]
    for y_ref, lo, hi in ((yp_ref, 0, c0), (yl_ref, c0, c1), (ya_ref, c1, D_MODEL)):
        mix = _rms(y_ref[...], og_ref[:, lo:hi]).astype(bf16)
        acc = acc + jnp.dot(mix, wo_ref[lo:hi, :], preferred_element_type=f32)
    xo_ref[...] = acc

    h2 = _rms(acc, g2_ref[...])
    h_hi = h2.astype(bf16)
    h_lo = (h2 - h_hi.astype(f32)).astype(bf16)
    nt = (((1,), (1,)), ((), ()))
    both = lax.dot_general(rw_ref[...], h_hi, nt, preferred_element_type=f32)
    logits = (both[:N_EXPERTS] + both[N_EXPERTS:]
              + lax.dot_general(rw_ref[:N_EXPERTS, :], h_lo, nt, preferred_element_type=f32)
              + rb_ref[...])
    eio = lax.broadcasted_iota(i32, (N_EXPERTS, tm), 0)
    cur = logits
    vals, idxs = [], []
    for _ in range(TOP_K):
        mx = jnp.max(cur, axis=0, keepdims=True)
        ix = jnp.min(jnp.where(cur == mx, eio, N_EXPERTS), axis=0, keepdims=True)
        vals.append(mx)
        idxs.append(ix)
        cur = jnp.where(eio == ix, -jnp.inf, cur)
    ex = [jnp.exp(v - vals[0]) for v in vals]
    den = ex[0] + ex[1] + ex[2] + ex[3]

    @pl.when(i % steps_per_tile == 0)
    def _():
        cnt_s[...] = jnp.zeros_like(cnt_s)

    onehot = jnp.zeros((N_EXPERTS, tm), f32)
    for ix in idxs:
        onehot = onehot + (eio == ix).astype(f32)
    @pl.when(i == 0)
    def _():
        before_s[...] = (lax.broadcasted_iota(i32, (tm, tm), 0) < lax.broadcasted_iota(i32, (tm, tm), 1)).astype(bf16)

    prefix = jnp.dot(onehot.astype(bf16), before_s[...], preferred_element_type=f32) + cnt_s[:, 0:1]
    for k in range(TOP_K):
        idx_ref[k:k + 1, :] = idxs[k]
        gate_ref[k:k + 1, :] = ex[k] / den
        rank_ref[k:k + 1, :] = jnp.sum(jnp.where(eio == idxs[k], prefix, 0.0), axis=0, keepdims=True).astype(i32)
    cnt_s[...] = cnt_s[...] + jnp.sum(onehot, axis=1, keepdims=True)
    cnt_ref[0] = cnt_s[...]


def _out_proj(x, yp, yl, ya, og, wo, g2, rw_t, rb):
    t = x.shape[0]
    tm = PROJ_TILE
    steps = t // tm
    tile = lambda w_: pl.BlockSpec((tm, w_), lambda i: (i, 0))
    full = lambda shape: pl.BlockSpec(shape, lambda i: (0,) * len(shape))
    route = pl.BlockSpec((TOP_K, tm), lambda i: (0, i))
    return pl.pallas_call(
        functools.partial(_out_proj_kernel, tm=tm, steps_per_tile=MOE_TILE // tm),
        grid=(steps,),
        in_specs=[tile(D_MODEL), tile(POOL_WIDTH), tile(LRU_WIDTH), tile(ATTN_WIDTH),
                  full((1, D_MODEL)), full((D_MODEL, D_MODEL)), full((1, D_MODEL)),
                  full((2 * N_EXPERTS, D_MODEL)), full((N_EXPERTS, 1))],
        out_specs=[tile(D_MODEL), route, route, route,
                   pl.BlockSpec((1, N_EXPERTS, LANES), lambda i: (i, 0, 0))],
        out_shape=[jax.ShapeDtypeStruct((t, D_MODEL), f32),
                   jax.ShapeDtypeStruct((TOP_K, t), i32),
                   jax.ShapeDtypeStruct((TOP_K, t), f32),
                   jax.ShapeDtypeStruct((TOP_K, t), i32),
                   jax.ShapeDtypeStruct((steps, N_EXPERTS, LANES), f32)],
        scratch_shapes=[pltpu.VMEM((N_EXPERTS, LANES), f32), pltpu.VMEM((tm, tm), bf16)],
        compiler_params=pltpu.CompilerParams(dimension_semantics=("arbitrary",),
                                             vmem_limit_bytes=VMEM_LIMIT),
        name="out_proj_router",
    )(x, yp, yl, ya, og, wo, g2, rw_t, rb)


ROW_SLAB = 8
ROUTE_GROUP = 8
MOE_EXPERTS_PER_STEP = 2


def _slab_cols(ref, first_row, n_rows, col):
    return ref[pl.ds(first_row * ROW_SLAB + col, n_rows, stride=ROW_SLAB), :]


def _moe_kernel(off_ref, pos_ref, gate_ref, x_ref, g2_ref, wgu_ref, bgu_ref, wd_ref, bd_ref,
                o_ref, sorted_s, stage_s, *, tt, ch, io_rows):
    i = pl.program_id(0)
    s = pl.program_id(1)
    n_io = tt // io_rows
    n_exp_steps = N_EXPERTS // MOE_EXPERTS_PER_STEP
    base = i * (N_EXPERTS + 1)
    n_col = D_MODEL // LANES

    def slab(ref, first_sublane):
        return ref.at[pl.ds(pl.multiple_of(first_sublane, ROW_SLAB), ROW_SLAB), :]

    @pl.when(s < n_io)
    def _dispatch():
        @pl.when(s == 0)
        def _():
            sorted_s[pl.ds(TOP_K * tt * ROW_SLAB, ch * ROW_SLAB), :] = jnp.zeros((ch * ROW_SLAB, LANES), f32)

        h = _rms(x_ref[...], g2_ref[...])
        for j in range(n_col):
            stage_s[pl.ds(j, io_rows, stride=ROW_SLAB), :] = h[:, j * LANES:(j + 1) * LANES]

        def scatter(c, carry):
            for u_ in range(ROUTE_GROUP):
                row = slab(stage_s, (c * ROUTE_GROUP + u_) * ROW_SLAB)[...]
                for k in range(TOP_K):
                    slab(sorted_s, pos_ref[c, u_ * TOP_K + k])[...] = row
            return carry

        lax.fori_loop(0, io_rows // ROUTE_GROUP, scatter, 0)

    @pl.when((s >= n_io) & (s < n_io + n_exp_steps))
    def _experts():
        for sub in range(MOE_EXPERTS_PER_STEP):
            e = (s - n_io) * MOE_EXPERTS_PER_STEP + sub
            lo = off_ref[base + e]
            hi = off_ref[base + e + 1]

            def chunk(c, carry, sub=sub, lo=lo, hi=hi):
                start = lo + c * ch
                xc = jnp.concatenate([_slab_cols(sorted_s, start, ch, j) for j in range(n_col)], axis=1)
                gu = jnp.dot(xc.astype(bf16), wgu_ref[sub], preferred_element_type=f32) + bgu_ref[sub]
                gate = jnp.minimum(gu[:, :EXPERT_FF], SWIGLU_LIMIT)
                up = jnp.clip(gu[:, EXPERT_FF:], -SWIGLU_LIMIT, SWIGLU_LIMIT)
                act = (up + 1.0) * gate * jax.nn.sigmoid(SWIGLU_ALPHA * gate)
                y = jnp.dot(act.astype(bf16), wd_ref[sub], preferred_element_type=f32) + bd_ref[sub]
                mine = (lax.broadcasted_iota(i32, (ch, 1), 0) + start) < hi
                y = jnp.where(mine, y, xc)
                for j in range(n_col):
                    sorted_s[pl.ds(start * ROW_SLAB + j, ch, stride=ROW_SLAB), :] = y[:, j * LANES:(j + 1) * LANES]
                return carry

            lax.fori_loop(0, (hi - lo + ch - 1) // ch, chunk, 0)

    @pl.when(s >= n_io + n_exp_steps)
    def _combine():
        def gather(c, carry):
            for u_ in range(ROUTE_GROUP):
                acc = gate_ref[c, u_ * TOP_K] * slab(sorted_s, pos_ref[c, u_ * TOP_K])[...]
                for k in range(1, TOP_K):
                    acc = acc + gate_ref[c, u_ * TOP_K + k] * slab(sorted_s, pos_ref[c, u_ * TOP_K + k])[...]
                slab(stage_s, (c * ROUTE_GROUP + u_) * ROW_SLAB)[...] = acc
            return carry

        lax.fori_loop(0, io_rows // ROUTE_GROUP, gather, 0)
        moe = jnp.concatenate([_slab_cols(stage_s, 0, io_rows, j) for j in range(n_col)], axis=1)
        o_ref[...] = x_ref[...] + moe


def _moe(x, pos, gates, offsets, g2, wgu, bgu, wd, bd):
    t = x.shape[0]
    tt = MOE_TILE
    ch = MOE_CHUNK
    io_rows = PROJ_TILE
    n_io = tt // io_rows
    eps = MOE_EXPERTS_PER_STEP
    n_exp_steps = N_EXPERTS // eps
    n_steps = 2 * n_io + n_exp_steps

    def tok_idx(i, s, off):
        in_combine = s >= n_io + n_exp_steps
        return i * n_io + jnp.where(s < n_io, s, jnp.where(in_combine, s - n_io - n_exp_steps, n_io - 1))

    def out_idx(i, s, off):
        return i * n_io + jnp.maximum(s - n_io - n_exp_steps, 0)

    def expert_idx(i, s, off):
        return jnp.clip(s - n_io, 0, n_exp_steps - 1)

    route = pl.BlockSpec((io_rows // ROUTE_GROUP, ROUTE_GROUP * TOP_K), lambda i, s, off: (tok_idx(i, s, off), 0),
                         memory_space=pltpu.SMEM)
    return pl.pallas_call(
        functools.partial(_moe_kernel, tt=tt, ch=ch, io_rows=io_rows),
        grid_spec=pltpu.PrefetchScalarGridSpec(
            num_scalar_prefetch=1,
            grid=(t // tt, n_steps),
            in_specs=[route, route,
                      pl.BlockSpec((io_rows, D_MODEL), lambda i, s, off: (tok_idx(i, s, off), 0)),
                      pl.BlockSpec((1, D_MODEL), lambda i, s, off: (0, 0)),
                      pl.BlockSpec((eps, D_MODEL, 2 * EXPERT_FF), lambda i, s, off: (expert_idx(i, s, off), 0, 0)),
                      pl.BlockSpec((eps, 1, 2 * EXPERT_FF), lambda i, s, off: (expert_idx(i, s, off), 0, 0)),
                      pl.BlockSpec((eps, EXPERT_FF, D_MODEL), lambda i, s, off: (expert_idx(i, s, off), 0, 0)),
                      pl.BlockSpec((eps, 1, D_MODEL), lambda i, s, off: (expert_idx(i, s, off), 0, 0))],
            out_specs=pl.BlockSpec((io_rows, D_MODEL), lambda i, s, off: (out_idx(i, s, off), 0)),
            scratch_shapes=[pltpu.VMEM(((TOP_K * tt + ch) * ROW_SLAB, LANES), f32),
                            pltpu.VMEM((io_rows * ROW_SLAB, LANES), f32)]),
        out_shape=jax.ShapeDtypeStruct((t, D_MODEL), f32),
        compiler_params=pltpu.CompilerParams(dimension_semantics=("arbitrary", "arbitrary"),
                                             vmem_limit_bytes=VMEM_LIMIT),
        name="moe_experts",
    )(offsets, pos, gates, x, g2, wgu, bgu, wd, bd)


def _rope_tables(seq):
    pos = jnp.arange(seq, dtype=f32)
    inv_freq = ROPE_THETA ** (-jnp.arange(0, HEAD_DIM, 2, dtype=f32) / HEAD_DIM)
    ang = pos[:, None] * inv_freq[None, :]
    cos, sin = jnp.cos(ang), jnp.sin(ang)
    reps = LANES // HEAD_DIM
    cos_t = jnp.tile(jnp.concatenate([cos, cos], axis=1), (1, reps))
    sin_t = jnp.tile(jnp.concatenate([-sin, sin], axis=1), (1, reps))
    return cos_t, sin_t


def _layer(x, cos_t, sin_t, batch, seq, p):
    t = batch * seq
    row = lambda v: v.reshape(1, -1)
    reps = LANES // HEAD_DIM
    pool_u, lru_u, gate, q, k, v = _in_proj(x, row(p["norm1_g"]), p["w_in"].astype(bf16), cos_t, sin_t,
                                            row(jnp.tile(p["q_norm_g"], reps)), row(jnp.tile(p["k_norm_g"], reps)), seq)
    wri = jnp.concatenate([block_diag(*p["lru_wa"]), block_diag(*p["lru_wx"])], axis=1).astype(bf16)
    bri = jnp.concatenate([p["lru_ba"], p["lru_bx"]]).reshape(1, -1)
    y_pool, y_lru = _mixer(pool_u, lru_u, gate, block_diag(*p["pool_w"]).astype(bf16), row(p["pool_scale"]),
                           p["conv_w"], row(p["conv_b"]), wri, bri, row(p["lru_lambda"]), batch, seq)
    y_attn = _attention(q, k, v, batch, seq)
    rw_hi = p["router_w"].T.astype(bf16)
    rw_split = jnp.concatenate([rw_hi, (p["router_w"].T - rw_hi.astype(f32)).astype(bf16)], axis=0)
    x, idx, gates, rank, cnt = _out_proj(x, y_pool, y_lru, y_attn, row(p["out_norm_g"]), p["w_out"].astype(bf16),
                                         row(p["norm2_g"]), rw_split, p["router_b"].reshape(-1, 1))
    steps_per_tile = MOE_TILE // PROJ_TILE
    n_tiles = t // MOE_TILE
    counts = cnt[steps_per_tile - 1::steps_per_tile, :, 0].astype(i32)
    offsets = jnp.concatenate([jnp.zeros((n_tiles, 1), i32), jnp.cumsum(counts, axis=1)], axis=1)
    idx_t = idx.reshape(TOP_K, n_tiles, MOE_TILE)
    seg_start = jnp.zeros_like(idx_t)
    for e in range(N_EXPERTS):
        seg_start = jnp.where(idx_t == e, offsets[None, :, e, None], seg_start)
    pos = (rank + seg_start.reshape(TOP_K, t)) * ROW_SLAB
    to_smem = lambda a: a.T.reshape(t // ROUTE_GROUP, ROUTE_GROUP * TOP_K)
    return _moe(x, to_smem(pos), to_smem(gates), offsets.reshape(-1), row(p["norm2_g"]),
                p["w_gate_up"].astype(bf16), p["b_gate_up"][:, None, :],
                p["w_down"].astype(bf16), p["b_down"][:, None, :])


def kernel(x, norm1_g, w_in, pool_w, pool_scale, conv_w, conv_b, lru_wa, lru_ba, lru_wx, lru_bx, lru_lambda,
           q_norm_g, k_norm_g, out_norm_g, w_out, norm2_g, router_w, router_b, w_gate_up, b_gate_up,
           w_down, b_down):
    batch, seq, d = x.shape
    params = dict(norm1_g=norm1_g, w_in=w_in, pool_w=pool_w, pool_scale=pool_scale, conv_w=conv_w, conv_b=conv_b,
                  lru_wa=lru_wa, lru_ba=lru_ba, lru_wx=lru_wx, lru_bx=lru_bx, lru_lambda=lru_lambda,
                  q_norm_g=q_norm_g, k_norm_g=k_norm_g, out_norm_g=out_norm_g, w_out=w_out, norm2_g=norm2_g,
                  router_w=router_w, router_b=router_b, w_gate_up=w_gate_up, b_gate_up=b_gate_up,
                  w_down=w_down, b_down=b_down)
    cos_t, sin_t = _rope_tables(seq)
    xt = x.reshape(batch * seq, d)
    for layer in range(norm1_g.shape[0]):
        xt = _layer(xt, cos_t, sin_t, batch, seq, {name: val[layer] for name, val in params.items()})
    return xt.reshape(batch, seq, d)
```

```python
import functools

import jax
import jax.numpy as jnp
from jax import lax
from jax.experimental import pallas as pl
from jax.experimental.pallas import tpu as pltpu
from jax.scipy.linalg import block_diag

f32 = jnp.float32
bf16 = jnp.bfloat16
i32 = jnp.int32

D_MODEL = 1024
POOL_WIDTH = 256
POOL_WINDOWS = (2, 4, 8, 16)
LRU_WIDTH = 256
LRU_C = 8.0
ATTN_WIDTH = 512
HEAD_DIM = 64
DILATIONS = (1, 4, 16)
MAX_DIL = DILATIONS[-1]
ATTN_BLOCK = 128
ROPE_THETA = 10000.0
N_EXPERTS = 32
TOP_K = 4
EXPERT_FF = 256
SWIGLU_LIMIT = 7.0
SWIGLU_ALPHA = 1.702
NORM_EPS = 1e-6
NEG_INF = -1e30

LANES = 128
POOL_HALO = 16
CONV_HALO = 8
VMEM_LIMIT = 56 * 1024 * 1024

PROJ_TILE = 512
MIX_TILE = 512
MOE_TILE = 2048
MOE_CHUNK = 288
ATTN_INTERLEAVE = 4


def _rms(x, g):
    return x * lax.rsqrt(jnp.mean(x * x, axis=-1, keepdims=True) + NORM_EPS) * g


_IN_SPLITS = (0, 256, 512, 768, 1280, 1792, 2304)


QK_QUARTER = HEAD_DIM // 2


def _pair_head0(shape):
    return (lax.broadcasted_iota(i32, shape, 1) % HEAD_DIM) < QK_QUARTER


def _qk_norm_rope(y, cs, sn, gain, scale):
    head0 = _pair_head0(y.shape)
    yy = y * y
    ms0 = jnp.sum(jnp.where(head0, yy, 0.0), axis=-1, keepdims=True)
    ms1 = jnp.sum(jnp.where(head0, 0.0, yy), axis=-1, keepdims=True)
    ms = jnp.where(head0, ms0, ms1) * (1.0 / HEAD_DIM)
    yn = y * lax.rsqrt(ms + NORM_EPS) * gain
    return (yn * cs + pltpu.roll(yn, HEAD_DIM, 1) * sn) * scale


def _in_proj_kernel(x_ref, g_ref, w_ref, cos_ref, sin_ref, qg_ref, kg_ref, *refs):
    out_refs, stage_s = refs[:-1], refs[-1]
    tm = x_ref.shape[0]
    h = _rms(x_ref[...], g_ref[...]).astype(bf16)

    def proj(n):
        return jnp.dot(h, w_ref[:, _IN_SPLITS[n]:_IN_SPLITS[n + 1]], preferred_element_type=f32)

    def emit_class_major(n, y, post=None):
        for p_, c in enumerate(range(0, y.shape[1], LANES)):
            yp = y[:, c:c + LANES]
            stage_s[n - 3, p_] = yp if post is None else _qk_norm_rope(yp, cos_ref[...], sin_ref[...], *post)
            for r in range(MAX_DIL):
                out_refs[n][0, r, :, c:c + LANES] = stage_s[n - 3, p_, pl.ds(r, tm // MAX_DIL, stride=MAX_DIL), :]

    q = proj(3)
    k = proj(4)
    emit_class_major(3, q, (qg_ref[...], HEAD_DIM ** -0.5))
    v = proj(5)
    out_refs[0][...] = proj(0)
    emit_class_major(4, k, (kg_ref[...], 1.0))
    out_refs[1][...] = proj(1)
    emit_class_major(5, v)
    out_refs[2][...] = proj(2)


def _in_proj(x, g, w, cos_t, sin_t, qg, kg, batch, seq):
    t = x.shape[0]
    tm = PROJ_TILE
    nt = seq // tm
    widths = [c1 - c0 for c0, c1 in zip(_IN_SPLITS[:-1], _IN_SPLITS[1:])]
    table = pl.BlockSpec((tm, LANES), lambda i: (i % nt, 0))
    gain = pl.BlockSpec((1, LANES), lambda i: (0, 0))
    flat = lambda w_: (pl.BlockSpec((tm, w_), lambda i: (i, 0)), jax.ShapeDtypeStruct((t, w_), f32))
    cls_major = lambda w_: (pl.BlockSpec((1, MAX_DIL, tm // MAX_DIL, w_), lambda i: (i // nt, 0, i % nt, 0)),
                            jax.ShapeDtypeStruct((batch, MAX_DIL, seq // MAX_DIL, w_), f32))
    outs = [flat(w_) for w_ in widths[:3]] + [cls_major(w_) for w_ in widths[3:]]
    return pl.pallas_call(
        _in_proj_kernel,
        grid=(t // tm,),
        in_specs=[pl.BlockSpec((tm, D_MODEL), lambda i: (i, 0)),
                  pl.BlockSpec((1, D_MODEL), lambda i: (0, 0)),
                  pl.BlockSpec((D_MODEL, _IN_SPLITS[-1]), lambda i: (0, 0)),
                  table, table, gain, gain],
        out_specs=[o[0] for o in outs],
        out_shape=[o[1] for o in outs],
        scratch_shapes=[pltpu.VMEM((3, ATTN_WIDTH // LANES, tm, LANES), f32)],
        compiler_params=pltpu.CompilerParams(dimension_semantics=("arbitrary",),
                                             vmem_limit_bytes=VMEM_LIMIT),
        name="in_proj",
    )(x, g, w, cos_t, sin_t, qg, kg)


def _mixer_kernel(pool_ref, lru_ref, gate_ref, pw_ref, ps_ref, cw_ref, cb_ref, wri_ref, bri_ref, lam_ref,
                  ypool_ref, ylru_ref,
                  b0, b1, b2, b3, cbuf, a_s, b_s, h_s, a2_s, b2_s, e2_s, e3_s, hc, *, ts):
    j = pl.program_id(1)
    n1 = ts // 8
    n2 = n1 // 8

    @pl.when(j == 0)
    def _():
        for buf in (b0, b1, b2, b3):
            buf[pl.ds(0, POOL_HALO), :] = jnp.zeros((POOL_HALO, POOL_WIDTH), f32)
        cbuf[pl.ds(0, CONV_HALO), :] = jnp.zeros((CONV_HALO, LRU_WIDTH), f32)
        hc[...] = jnp.zeros_like(hc)

    row = lax.broadcasted_iota(i32, (ts, POOL_WIDTH), 0) + j * ts
    grp = lax.broadcasted_iota(i32, (ts, POOL_WIDTH), 1) // (POOL_WIDTH // len(POOL_WINDOWS))

    u = pool_ref[...]
    b0[pl.ds(POOL_HALO, ts), :] = u
    s1 = u + b0[pl.ds(POOL_HALO - 1, ts), :]
    b1[pl.ds(POOL_HALO, ts), :] = s1
    s2 = s1 + b1[pl.ds(POOL_HALO - 2, ts), :]
    b2[pl.ds(POOL_HALO, ts), :] = s2
    s3 = s2 + b2[pl.ds(POOL_HALO - 4, ts), :]
    b3[pl.ds(POOL_HALO, ts), :] = s3
    s4 = s3 + b3[pl.ds(POOL_HALO - 8, ts), :]
    for buf in (b0, b1, b2, b3):
        buf[pl.ds(0, POOL_HALO), :] = buf[pl.ds(ts, POOL_HALO), :]
    sums = jnp.where(grp == 0, s1, jnp.where(grp == 1, s2, jnp.where(grp == 2, s3, s4)))
    win = jnp.where(grp == 0, float(POOL_WINDOWS[0]),
                    jnp.where(grp == 1, float(POOL_WINDOWS[1]),
                              jnp.where(grp == 2, float(POOL_WINDOWS[2]), float(POOL_WINDOWS[3]))))
    cnt = jnp.minimum((row + 1).astype(f32), win)
    pooled = sums / cnt - u
    ypool_ref[...] = jnp.dot(pooled.astype(bf16), pw_ref[...], preferred_element_type=f32) * ps_ref[...]

    lu = lru_ref[...]
    cbuf[pl.ds(CONV_HALO, ts), :] = lu
    xc = (cb_ref[...] + cw_ref[3:4, :] * lu
          + cw_ref[2:3, :] * cbuf[pl.ds(CONV_HALO - 1, ts), :]
          + cw_ref[1:2, :] * cbuf[pl.ds(CONV_HALO - 2, ts), :]
          + cw_ref[0:1, :] * cbuf[pl.ds(CONV_HALO - 3, ts), :])
    cbuf[pl.ds(0, CONV_HALO), :] = cbuf[pl.ds(ts, CONV_HALO), :]
    ri = jnp.dot(xc.astype(bf16), wri_ref[...], preferred_element_type=f32) + bri_ref[...]
    r = jax.nn.sigmoid(ri[:, :LRU_WIDTH])
    ig = jax.nn.sigmoid(ri[:, LRU_WIDTH:])
    log_a = -LRU_C * r * jax.nn.softplus(-lam_ref[...])
    a = jnp.exp(log_a)
    th = jnp.tanh(log_a)
    mult = jnp.sqrt(-2.0 * th / (1.0 - th))
    mult = jnp.where(row == 0, 1.0, mult)
    bb = mult * (ig * xc)

    def rows(ref, hf, s, n):
        return ref[hf, pl.ds(s, n, stride=8), :]

    for hf in range(LRU_WIDTH // LANES):
        a_s[hf] = a[:, hf * LANES:(hf + 1) * LANES]
        b_s[hf] = bb[:, hf * LANES:(hf + 1) * LANES]
        pa, pb = rows(a_s, hf, 0, n1), rows(b_s, hf, 0, n1)
        for s in range(1, 8):
            as_ = rows(a_s, hf, s, n1)
            pb = as_ * pb + rows(b_s, hf, s, n1)
            pa = as_ * pa
        a2_s[hf] = pa
        b2_s[hf] = pb
        pa, pb = rows(a2_s, hf, 0, n2), rows(b2_s, hf, 0, n2)
        for s in range(1, 8):
            as_ = rows(a2_s, hf, s, n2)
            pb = as_ * pb + rows(b2_s, hf, s, n2)
            pa = as_ * pa
        h = hc[hf, 0:1, :]
        for k in range(8):
            e3_s[hf, k:k + 1, :] = h
            h = pa[k:k + 1, :] * h + pb[k:k + 1, :]
        hc[hf, 0:1, :] = h
        prev = e3_s[hf]
        for s in range(8):
            e2_s[hf, pl.ds(s, n2, stride=8), :] = prev
            prev = rows(a2_s, hf, s, n2) * prev + rows(b2_s, hf, s, n2)
        prev = e2_s[hf]
        for s in range(8):
            prev = rows(a_s, hf, s, n1) * prev + rows(b_s, hf, s, n1)
            h_s[hf, pl.ds(s, n1, stride=8), :] = prev

    hfull = jnp.concatenate([h_s[hf] for hf in range(LRU_WIDTH // LANES)], axis=1)
    ylru_ref[...] = hfull * jax.nn.gelu(gate_ref[...])


def _mixer(pool_u, lru_u, gate, pw, ps, cw, cb, wri, bri, lam, batch, seq):
    ts = MIX_TILE
    nt = seq // ts
    t = batch * seq
    nh = LRU_WIDTH // LANES
    tile = lambda w_: pl.BlockSpec((ts, w_), lambda b, j: (b * nt + j, 0))
    full = lambda shape: pl.BlockSpec(shape, lambda b, j: (0,) * len(shape))
    return pl.pallas_call(
        functools.partial(_mixer_kernel, ts=ts),
        grid=(batch, nt),
        in_specs=[tile(POOL_WIDTH), tile(LRU_WIDTH), tile(LRU_WIDTH),
                  full((POOL_WIDTH, POOL_WIDTH)), full((1, POOL_WIDTH)),
                  full((4, LRU_WIDTH)), full((1, LRU_WIDTH)),
                  full((LRU_WIDTH, 2 * LRU_WIDTH)), full((1, 2 * LRU_WIDTH)), full((1, LRU_WIDTH))],
        out_specs=[tile(POOL_WIDTH), tile(LRU_WIDTH)],
        out_shape=[jax.ShapeDtypeStruct((t, POOL_WIDTH), f32), jax.ShapeDtypeStruct((t, LRU_WIDTH), f32)],
        scratch_shapes=[pltpu.VMEM((ts + POOL_HALO, POOL_WIDTH), f32)] * 4
        + [pltpu.VMEM((ts + CONV_HALO, LRU_WIDTH), f32)]
        + [pltpu.VMEM((nh, ts, LANES), f32)] * 3
        + [pltpu.VMEM((nh, ts // 8, LANES), f32)] * 3
        + [pltpu.VMEM((nh, 8, LANES), f32)] * 2,
        compiler_params=pltpu.CompilerParams(dimension_semantics=("arbitrary", "arbitrary"),
                                             vmem_limit_bytes=VMEM_LIMIT),
        name="pool_lru_mixer",
    )(pool_u, lru_u, gate, pw, ps, cw, cb, wri, bri, lam)


def _attn_kernel(q_ref, k_ref, v_ref, o_ref, m_s, l_s, acc_s, *, seq):
    blk = ATTN_BLOCK
    cls = seq // MAX_DIL
    qs, ks, vs = q_ref.at[0], k_ref.at[0], v_ref.at[0]

    h0 = lax.broadcasted_iota(i32, (blk, LANES), 1) < HEAD_DIM
    q_h0 = _pair_head0((blk, LANES))
    state = (m_s, l_s, acc_s)
    qi = lax.broadcasted_iota(i32, (2 * blk, 2 * blk), 0) % blk
    kj = lax.broadcasted_iota(i32, (2 * blk, 2 * blk), 1)

    for bi, d in enumerate(DILATIONS):
        nb = seq // (blk * d)
        runs = MAX_DIL // d
        run = blk // runs

        def member(i, runs=runs, run=run):
            return runs * (i % run) + i // run

        jq = member(qi)
        jk = member(kj % blk)
        mask_cur = (kj >= blk) & (jk <= jq)
        mask_prev = (kj < blk) & (jk >= jq)

        def starts(res, n, d=d, runs=runs, run=run):
            return [(res + d * b, pl.multiple_of(run * n, 8)) for b in range(runs)]

        def ld(ref, row_starts, run=run):
            return jnp.concatenate([ref[c, pl.ds(r0, run), :] for c, r0 in row_starts], axis=0)

        def st(ref, row_starts, val, run=run):
            for b, (c, r0) in enumerate(row_starts):
                ref[c, pl.ds(r0, run), :] = val[b * run:(b + 1) * run, :]

        def body(it, carry, nb=nb, bi=bi, starts=starts, ld=ld, st=st, mask_cur=mask_cur, mask_prev=mask_prev):
            loaded = []
            for u_ in range(ATTN_INTERLEAVE):
                idx = it * ATTN_INTERLEAVE + u_
                res = idx // nb
                n = idx % nb
                cur = starts(res, n)
                prev = starts(res, jnp.maximum(n - 1, 0))
                qb = ld(qs, cur)
                kb = jnp.concatenate([ld(ks, prev), ld(ks, cur)], axis=0).astype(bf16)
                vb = jnp.concatenate([ld(vs, prev), ld(vs, cur)], axis=0).astype(bf16)
                loaded.append((n, cur, qb, kb, vb))
            scores = []
            for n, cur, qb, kb, vb in loaded:
                q2 = jnp.concatenate([jnp.where(q_h0, qb, 0.0), jnp.where(q_h0, 0.0, qb)], axis=0).astype(bf16)
                scores.append(lax.dot_general(q2, kb, (((1,), (1,)), ((), ())), preferred_element_type=f32))
            results = []
            for (n, cur, qb, kb, vb), s in zip(loaded, scores):
                valid = mask_cur | (mask_prev & (n > 0))
                s = jnp.where(valid, s, NEG_INF)
                mb = jnp.max(s, axis=-1, keepdims=True)
                p = jnp.exp(s - mb)
                lb = jnp.sum(p, axis=-1, keepdims=True)
                c = jnp.dot(p.astype(bf16), vb, preferred_element_type=f32)
                m_blk = jnp.where(h0, mb[:blk], mb[blk:])
                l_blk = jnp.where(h0, lb[:blk], lb[blk:])
                n_blk = jnp.where(h0, c[:blk], c[blk:])
                if bi == 0:
                    results.append((cur, m_blk, l_blk, n_blk))
                else:
                    m_old, l_old, a_old = [ld(ref, cur) for ref in state]
                    m_new = jnp.maximum(m_old, m_blk)
                    w_old = jnp.exp(m_old - m_new)
                    w_blk = jnp.exp(m_blk - m_new)
                    results.append((cur, m_new, w_old * l_old + w_blk * l_blk, w_old * a_old + w_blk * n_blk))
            for cur, *vals in results:
                for ref, val in zip(state, vals):
                    st(ref, cur, val)
            return carry

        lax.fori_loop(0, seq // (blk * ATTN_INTERLEAVE), body, 0)

    def fin(r, carry):
        o_ref[0, r] = acc_s[r] / l_s[r]
        return carry

    lax.fori_loop(0, MAX_DIL, fin, 0)


def _attention(q, k, v, batch, seq):
    npair = ATTN_WIDTH // LANES
    cls = seq // MAX_DIL
    tile = pl.BlockSpec((1, MAX_DIL, cls, LANES), lambda b, p: (b, 0, 0, p))
    return pl.pallas_call(
        functools.partial(_attn_kernel, seq=seq),
        grid=(batch, npair),
        in_specs=[tile, tile, tile],
        out_specs=tile,
        out_shape=jax.ShapeDtypeStruct((batch, MAX_DIL, cls, ATTN_WIDTH), f32),
        scratch_shapes=[pltpu.VMEM((MAX_DIL, cls, LANES), f32)] * 3,
        compiler_params=pltpu.CompilerParams(dimension_semantics=("arbitrary", "arbitrary"),
                                             vmem_limit_bytes=VMEM_LIMIT),
        name="dilated_attention",
    )(q, k, v)


def _out_proj_kernel(x_ref, yp_ref, yl_ref, ya_ref, og_ref, wo_ref, g2_ref, rw_ref, rb_ref,
                     xo_ref, idx_ref, gate_ref, rank_ref, cnt_ref, cnt_s, before_s, ya_s, *, tm, steps_per_tile):
    i = pl.program_id(0)
    c0, c1 = POOL_WIDTH, POOL_WIDTH + LRU_WIDTH
    for r in range(MAX_DIL):
        for p_ in range(ATTN_WIDTH // LANES):
            ya_s[p_, pl.ds(r, tm // MAX_DIL, stride=MAX_DIL), :] = ya_ref[0, r, :, p_ * LANES:(p_ + 1) * LANES]
    ya = jnp.concatenate([ya_s[p_] for p_ in range(ATTN_WIDTH // LANES)], axis=1)
    acc = x_ref[...]
    for y, lo, hi in ((yp_ref[...], 0, c0), (yl_ref[...], c0, c1), (ya, c1, D_MODEL)):
        mix = _rms(y, og_ref[:, lo:hi]).astype(bf16)
        acc = acc + jnp.dot(mix, wo_ref[lo:hi, :], preferred_element_type=f32)
    xo_ref[...] = acc

    h2 = _rms(acc, g2_ref[...])
    h_hi = h2.astype(bf16)
    h_lo = (h2 - h_hi.astype(f32)).astype(bf16)
    nt = (((1,), (1,)), ((), ()))
    both = lax.dot_general(rw_ref[...], h_hi, nt, preferred_element_type=f32)
    logits = (both[:N_EXPERTS] + both[N_EXPERTS:]
              + lax.dot_general(rw_ref[:N_EXPERTS, :], h_lo, nt, preferred_element_type=f32)
              + rb_ref[...])
    eio = lax.broadcasted_iota(i32, (N_EXPERTS, tm), 0)
    cur = logits
    vals, idxs = [], []
    for _ in range(TOP_K):
        mx = jnp.max(cur, axis=0, keepdims=True)
        ix = jnp.min(jnp.where(cur == mx, eio, N_EXPERTS), axis=0, keepdims=True)
        vals.append(mx)
        idxs.append(ix)
        cur = jnp.where(eio == ix, -jnp.inf, cur)
    ex = [jnp.exp(v - vals[0]) for v in vals]
    den = ex[0] + ex[1] + ex[2] + ex[3]

    @pl.when(i % steps_per_tile == 0)
    def _():
        cnt_s[...] = jnp.zeros_like(cnt_s)

    onehot = jnp.zeros((N_EXPERTS, tm), f32)
    for ix in idxs:
        onehot = onehot + (eio == ix).astype(f32)
    @pl.when(i == 0)
    def _():
        before_s[...] = (lax.broadcasted_iota(i32, (tm, tm), 0) < lax.broadcasted_iota(i32, (tm, tm), 1)).astype(bf16)

    prefix = jnp.dot(onehot.astype(bf16), before_s[...], preferred_element_type=f32) + cnt_s[:, 0:1]
    for k in range(TOP_K):
        idx_ref[k:k + 1, :] = idxs[k]
        gate_ref[k:k + 1, :] = ex[k] / den
        rank_ref[k:k + 1, :] = jnp.sum(jnp.where(eio == idxs[k], prefix, 0.0), axis=0, keepdims=True).astype(i32)
    cnt_s[...] = cnt_s[...] + jnp.sum(onehot, axis=1, keepdims=True)
    cnt_ref[0] = cnt_s[...]


def _out_proj(x, yp, yl, ya, og, wo, g2, rw_t, rb, seq):
    t = x.shape[0]
    tm = PROJ_TILE
    steps = t // tm
    nt = seq // tm
    tile = lambda w_: pl.BlockSpec((tm, w_), lambda i: (i, 0))
    full = lambda shape: pl.BlockSpec(shape, lambda i: (0,) * len(shape))
    route = pl.BlockSpec((TOP_K, tm), lambda i: (0, i))
    return pl.pallas_call(
        functools.partial(_out_proj_kernel, tm=tm, steps_per_tile=MOE_TILE // tm),
        grid=(steps,),
        in_specs=[tile(D_MODEL), tile(POOL_WIDTH), tile(LRU_WIDTH),
                  pl.BlockSpec((1, MAX_DIL, tm // MAX_DIL, ATTN_WIDTH), lambda i: (i // nt, 0, i % nt, 0)),
                  full((1, D_MODEL)), full((D_MODEL, D_MODEL)), full((1, D_MODEL)),
                  full((2 * N_EXPERTS, D_MODEL)), full((N_EXPERTS, 1))],
        out_specs=[tile(D_MODEL), route, route, route,
                   pl.BlockSpec((1, N_EXPERTS, LANES), lambda i: (i, 0, 0))],
        out_shape=[jax.ShapeDtypeStruct((t, D_MODEL), f32),
                   jax.ShapeDtypeStruct((TOP_K, t), i32),
                   jax.ShapeDtypeStruct((TOP_K, t), f32),
                   jax.ShapeDtypeStruct((TOP_K, t), i32),
                   jax.ShapeDtypeStruct((steps, N_EXPERTS, LANES), f32)],
        scratch_shapes=[pltpu.VMEM((N_EXPERTS, LANES), f32), pltpu.VMEM((tm, tm), bf16),
                        pltpu.VMEM((ATTN_WIDTH // LANES, tm, LANES), f32)],
        compiler_params=pltpu.CompilerParams(dimension_semantics=("arbitrary",),
                                             vmem_limit_bytes=VMEM_LIMIT),
        name="out_proj_router",
    )(x, yp, yl, ya, og, wo, g2, rw_t, rb)


ROW_SLAB = 8
ROUTE_GROUP = 8
MOE_EXPERTS_PER_STEP = 2


def _slab_cols(ref, first_row, n_rows, col):
    return ref[pl.ds(first_row * ROW_SLAB + col, n_rows, stride=ROW_SLAB), :]


def _moe_kernel(off_ref, pos_ref, gate_ref, x_ref, g2_ref, wgu_ref, bgu_ref, wd_ref, bd_ref,
                o_ref, sorted_s, stage_s, *, tt, ch, io_rows):
    i = pl.program_id(0)
    s = pl.program_id(1)
    n_io = tt // io_rows
    n_exp_steps = N_EXPERTS // MOE_EXPERTS_PER_STEP
    base = i * (N_EXPERTS + 1)
    n_col = D_MODEL // LANES

    def slab(ref, first_sublane):
        return ref.at[pl.ds(pl.multiple_of(first_sublane, ROW_SLAB), ROW_SLAB), :]

    @pl.when(s < n_io)
    def _dispatch():
        @pl.when(s == 0)
        def _():
            sorted_s[pl.ds(TOP_K * tt * ROW_SLAB, ch * ROW_SLAB), :] = jnp.zeros((ch * ROW_SLAB, LANES), f32)

        h = _rms(x_ref[...], g2_ref[...])
        for j in range(n_col):
            stage_s[pl.ds(j, io_rows, stride=ROW_SLAB), :] = h[:, j * LANES:(j + 1) * LANES]

        def scatter(c, carry):
            for u_ in range(ROUTE_GROUP):
                row = slab(stage_s, (c * ROUTE_GROUP + u_) * ROW_SLAB)[...]
                for k in range(TOP_K):
                    slab(sorted_s, pos_ref[c, u_ * TOP_K + k])[...] = row
            return carry

        lax.fori_loop(0, io_rows // ROUTE_GROUP, scatter, 0)

    @pl.when((s >= n_io) & (s < n_io + n_exp_steps))
    def _experts():
        for sub in range(MOE_EXPERTS_PER_STEP):
            e = (s - n_io) * MOE_EXPERTS_PER_STEP + sub
            lo = off_ref[base + e]
            hi = off_ref[base + e + 1]

            def chunk(c, carry, sub=sub, lo=lo, hi=hi):
                start = lo + c * ch
                xc = jnp.concatenate([_slab_cols(sorted_s, start, ch, j) for j in range(n_col)], axis=1)
                gu = jnp.dot(xc.astype(bf16), wgu_ref[sub], preferred_element_type=f32) + bgu_ref[sub]
                gate = jnp.minimum(gu[:, :EXPERT_FF], SWIGLU_LIMIT)
                up = jnp.clip(gu[:, EXPERT_FF:], -SWIGLU_LIMIT, SWIGLU_LIMIT)
                act = (up + 1.0) * gate * jax.nn.sigmoid(SWIGLU_ALPHA * gate)
                y = jnp.dot(act.astype(bf16), wd_ref[sub], preferred_element_type=f32) + bd_ref[sub]
                mine = (lax.broadcasted_iota(i32, (ch, 1), 0) + start) < hi
                y = jnp.where(mine, y, xc)
                for j in range(n_col):
                    sorted_s[pl.ds(start * ROW_SLAB + j, ch, stride=ROW_SLAB), :] = y[:, j * LANES:(j + 1) * LANES]
                return carry

            lax.fori_loop(0, (hi - lo + ch - 1) // ch, chunk, 0)

    @pl.when(s >= n_io + n_exp_steps)
    def _combine():
        def gather(c, carry):
            for u_ in range(ROUTE_GROUP):
                acc = gate_ref[c, u_ * TOP_K] * slab(sorted_s, pos_ref[c, u_ * TOP_K])[...]
                for k in range(1, TOP_K):
                    acc = acc + gate_ref[c, u_ * TOP_K + k] * slab(sorted_s, pos_ref[c, u_ * TOP_K + k])[...]
                slab(stage_s, (c * ROUTE_GROUP + u_) * ROW_SLAB)[...] = acc
            return carry

        lax.fori_loop(0, io_rows // ROUTE_GROUP, gather, 0)
        moe = jnp.concatenate([_slab_cols(stage_s, 0, io_rows, j) for j in range(n_col)], axis=1)
        o_ref[...] = x_ref[...] + moe


def _moe(x, pos, gates, offsets, g2, wgu, bgu, wd, bd):
    t = x.shape[0]
    tt = MOE_TILE
    ch = MOE_CHUNK
    io_rows = PROJ_TILE
    n_io = tt // io_rows
    eps = MOE_EXPERTS_PER_STEP
    n_exp_steps = N_EXPERTS // eps
    n_steps = 2 * n_io + n_exp_steps

    def tok_idx(i, s, off):
        in_combine = s >= n_io + n_exp_steps
        return i * n_io + jnp.where(s < n_io, s, jnp.where(in_combine, s - n_io - n_exp_steps, n_io - 1))

    def out_idx(i, s, off):
        return i * n_io + jnp.maximum(s - n_io - n_exp_steps, 0)

    def expert_idx(i, s, off):
        return jnp.clip(s - n_io, 0, n_exp_steps - 1)

    route = pl.BlockSpec((io_rows // ROUTE_GROUP, ROUTE_GROUP * TOP_K), lambda i, s, off: (tok_idx(i, s, off), 0),
                         memory_space=pltpu.SMEM)
    return pl.pallas_call(
        functools.partial(_moe_kernel, tt=tt, ch=ch, io_rows=io_rows),
        grid_spec=pltpu.PrefetchScalarGridSpec(
            num_scalar_prefetch=1,
            grid=(t // tt, n_steps),
            in_specs=[route, route,
                      pl.BlockSpec((io_rows, D_MODEL), lambda i, s, off: (tok_idx(i, s, off), 0)),
                      pl.BlockSpec((1, D_MODEL), lambda i, s, off: (0, 0)),
                      pl.BlockSpec((eps, D_MODEL, 2 * EXPERT_FF), lambda i, s, off: (expert_idx(i, s, off), 0, 0)),
                      pl.BlockSpec((eps, 1, 2 * EXPERT_FF), lambda i, s, off: (expert_idx(i, s, off), 0, 0)),
                      pl.BlockSpec((eps, EXPERT_FF, D_MODEL), lambda i, s, off: (expert_idx(i, s, off), 0, 0)),
                      pl.BlockSpec((eps, 1, D_MODEL), lambda i, s, off: (expert_idx(i, s, off), 0, 0))],
            out_specs=pl.BlockSpec((io_rows, D_MODEL), lambda i, s, off: (out_idx(i, s, off), 0)),
            scratch_shapes=[pltpu.VMEM(((TOP_K * tt + ch) * ROW_SLAB, LANES), f32),
                            pltpu.VMEM((io_rows * ROW_SLAB, LANES), f32)]),
        out_shape=jax.ShapeDtypeStruct((t, D_MODEL), f32),
        compiler_params=pltpu.CompilerParams(dimension_semantics=("arbitrary", "arbitrary"),
                                             vmem_limit_bytes=VMEM_LIMIT),
        name="moe_experts",
    )(offsets, pos, gates, x, g2, wgu, bgu, wd, bd)


def _pair_layout(v):
    first, second = v[..., :QK_QUARTER], v[..., QK_QUARTER:]
    return jnp.concatenate([first, first, second, second], axis=-1)


def _qk_column_order():
    pair = jnp.arange(LANES).reshape(2, 2, QK_QUARTER).transpose(1, 0, 2).reshape(-1)
    cols = jnp.arange(_IN_SPLITS[-1])
    qk = cols[_IN_SPLITS[3]:_IN_SPLITS[5]].reshape(-1, LANES)[:, pair].reshape(-1)
    return jnp.concatenate([cols[:_IN_SPLITS[3]], qk, cols[_IN_SPLITS[5]:]])


def _rope_tables(seq):
    pos = jnp.arange(seq, dtype=f32)
    inv_freq = ROPE_THETA ** (-jnp.arange(0, HEAD_DIM, 2, dtype=f32) / HEAD_DIM)
    ang = pos[:, None] * inv_freq[None, :]
    cos, sin = jnp.cos(ang), jnp.sin(ang)
    return _pair_layout(jnp.concatenate([cos, cos], axis=1)), _pair_layout(jnp.concatenate([-sin, sin], axis=1))


def _layer(x, cos_t, sin_t, batch, seq, p):
    t = batch * seq
    row = lambda v: v.reshape(1, -1)
    pool_u, lru_u, gate, q, k, v = _in_proj(x, row(p["norm1_g"]), p["w_in"][:, _qk_column_order()].astype(bf16),
                                            cos_t, sin_t, row(_pair_layout(p["q_norm_g"])),
                                            row(_pair_layout(p["k_norm_g"])), batch, seq)
    wri = jnp.concatenate([block_diag(*p["lru_wa"]), block_diag(*p["lru_wx"])], axis=1).astype(bf16)
    bri = jnp.concatenate([p["lru_ba"], p["lru_bx"]]).reshape(1, -1)
    y_pool, y_lru = _mixer(pool_u, lru_u, gate, block_diag(*p["pool_w"]).astype(bf16), row(p["pool_scale"]),
                           p["conv_w"], row(p["conv_b"]), wri, bri, row(p["lru_lambda"]), batch, seq)
    y_attn = _attention(q, k, v, batch, seq)
    rw_hi = p["router_w"].T.astype(bf16)
    rw_split = jnp.concatenate([rw_hi, (p["router_w"].T - rw_hi.astype(f32)).astype(bf16)], axis=0)
    x, idx, gates, rank, cnt = _out_proj(x, y_pool, y_lru, y_attn, row(p["out_norm_g"]), p["w_out"].astype(bf16),
                                         row(p["norm2_g"]), rw_split, p["router_b"].reshape(-1, 1), seq)
    steps_per_tile = MOE_TILE // PROJ_TILE
    n_tiles = t // MOE_TILE
    counts = cnt[steps_per_tile - 1::steps_per_tile, :, 0].astype(i32)
    offsets = jnp.concatenate([jnp.zeros((n_tiles, 1), i32), jnp.cumsum(counts, axis=1)], axis=1)
    idx_t = idx.reshape(TOP_K, n_tiles, MOE_TILE)
    seg_start = jnp.zeros_like(idx_t)
    for e in range(N_EXPERTS):
        seg_start = jnp.where(idx_t == e, offsets[None, :, e, None], seg_start)
    pos = (rank + seg_start.reshape(TOP_K, t)) * ROW_SLAB
    to_smem = lambda a: a.T.reshape(t // ROUTE_GROUP, ROUTE_GROUP * TOP_K)
    return _moe(x, to_smem(pos), to_smem(gates), offsets.reshape(-1), row(p["norm2_g"]),
                p["w_gate_up"].astype(bf16), p["b_gate_up"][:, None, :],
                p["w_down"].astype(bf16), p["b_down"][:, None, :])


def kernel(x, norm1_g, w_in, pool_w, pool_scale, conv_w, conv_b, lru_wa, lru_ba, lru_wx, lru_bx, lru_lambda,
           q_norm_g, k_norm_g, out_norm_g, w_out, norm2_g, router_w, router_b, w_gate_up, b_gate_up,
           w_down, b_down):
    batch, seq, d = x.shape
    params = dict(norm1_g=norm1_g, w_in=w_in, pool_w=pool_w, pool_scale=pool_scale, conv_w=conv_w, conv_b=conv_b,
                  lru_wa=lru_wa, lru_ba=lru_ba, lru_wx=lru_wx, lru_bx=lru_bx, lru_lambda=lru_lambda,
                  q_norm_g=q_norm_g, k_norm_g=k_norm_g, out_norm_g=out_norm_g, w_out=w_out, norm2_g=norm2_g,
                  router_w=router_w, router_b=router_b, w_gate_up=w_gate_up, b_gate_up=b_gate_up,
                  w_down=w_down, b_down=b_down)
    cos_t, sin_t = _rope_tables(seq)
    xt = x.reshape(batch * seq, d)
    for layer in range(norm1_g.shape[0]):
        xt = _layer(xt, cos_t, sin_t, batch, seq, {name: val[layer] for name, val in params.items()})
    return xt.reshape(batch, seq, d)
```

```python
import functools

import jax
import jax.numpy as jnp
from jax import lax
from jax.experimental import pallas as pl
from jax.experimental.pallas import tpu as pltpu
from jax.scipy.linalg import block_diag

f32 = jnp.float32
bf16 = jnp.bfloat16
i32 = jnp.int32

D_MODEL = 1024
POOL_WIDTH = 256
POOL_WINDOWS = (2, 4, 8, 16)
LRU_WIDTH = 256
LRU_C = 8.0
ATTN_WIDTH = 512
HEAD_DIM = 64
DILATIONS = (1, 4, 16)
MAX_DIL = DILATIONS[-1]
ATTN_BLOCK = 128
ROPE_THETA = 10000.0
N_EXPERTS = 32
TOP_K = 4
EXPERT_FF = 256
SWIGLU_LIMIT = 7.0
SWIGLU_ALPHA = 1.702
NORM_EPS = 1e-6
NEG_INF = -1e30

LANES = 128
POOL_HALO = 16
CONV_HALO = 8
VMEM_LIMIT = 56 * 1024 * 1024

PROJ_TILE = 512
MIX_TILE = 512
MOE_TILE = 2048
MOE_CHUNK = 512
MOE_TAIL_SIZES = (128, 256, 384, MOE_CHUNK)
ATTN_INTERLEAVE = 4


def _rms(x, g):
    return x * lax.rsqrt(jnp.mean(x * x, axis=-1, keepdims=True) + NORM_EPS) * g


_IN_SPLITS = (0, 256, 512, 768, 1280, 1792, 2304)


QK_QUARTER = HEAD_DIM // 2


def _pair_head0(shape):
    return (lax.broadcasted_iota(i32, shape, 1) % HEAD_DIM) < QK_QUARTER


def _qk_norm_rope(y, cs, sn, gain, scale):
    head0 = _pair_head0(y.shape)
    yy = y * y
    ms0 = jnp.sum(jnp.where(head0, yy, 0.0), axis=-1, keepdims=True)
    ms1 = jnp.sum(jnp.where(head0, 0.0, yy), axis=-1, keepdims=True)
    ms = jnp.where(head0, ms0, ms1) * (1.0 / HEAD_DIM)
    yn = y * lax.rsqrt(ms + NORM_EPS) * gain
    return (yn * cs + pltpu.roll(yn, HEAD_DIM, 1) * sn) * scale


def _in_proj_kernel(x_ref, g_ref, w_ref, cos_ref, sin_ref, qg_ref, kg_ref, *refs):
    out_refs, stage_s = refs[:-1], refs[-1]
    tm = x_ref.shape[0]
    h = _rms(x_ref[...], g_ref[...]).astype(bf16)

    def proj(n):
        return jnp.dot(h, w_ref[:, _IN_SPLITS[n]:_IN_SPLITS[n + 1]], preferred_element_type=f32)

    def emit_class_major(n, y, post=None):
        for p_, c in enumerate(range(0, y.shape[1], LANES)):
            yp = y[:, c:c + LANES]
            stage_s[n - 3, p_] = yp if post is None else _qk_norm_rope(yp, cos_ref[...], sin_ref[...], *post)
            for r in range(MAX_DIL):
                out_refs[n][0, r, :, c:c + LANES] = stage_s[n - 3, p_, pl.ds(r, tm // MAX_DIL, stride=MAX_DIL), :]

    q = proj(3)
    k = proj(4)
    emit_class_major(3, q, (qg_ref[...], HEAD_DIM ** -0.5))
    v = proj(5)
    out_refs[0][...] = proj(0)
    emit_class_major(4, k, (kg_ref[...], 1.0))
    out_refs[1][...] = proj(1)
    emit_class_major(5, v)
    out_refs[2][...] = proj(2)


def _in_proj(x, g, w, cos_t, sin_t, qg, kg, batch, seq):
    t = x.shape[0]
    tm = PROJ_TILE
    nt = seq // tm
    widths = [c1 - c0 for c0, c1 in zip(_IN_SPLITS[:-1], _IN_SPLITS[1:])]
    table = pl.BlockSpec((tm, LANES), lambda i: (i % nt, 0))
    gain = pl.BlockSpec((1, LANES), lambda i: (0, 0))
    flat = lambda w_: (pl.BlockSpec((tm, w_), lambda i: (i, 0)), jax.ShapeDtypeStruct((t, w_), f32))
    cls_major = lambda w_: (pl.BlockSpec((1, MAX_DIL, tm // MAX_DIL, w_), lambda i: (i // nt, 0, i % nt, 0)),
                            jax.ShapeDtypeStruct((batch, MAX_DIL, seq // MAX_DIL, w_), f32))
    outs = [flat(w_) for w_ in widths[:3]] + [cls_major(w_) for w_ in widths[3:]]
    return pl.pallas_call(
        _in_proj_kernel,
        grid=(t // tm,),
        in_specs=[pl.BlockSpec((tm, D_MODEL), lambda i: (i, 0)),
                  pl.BlockSpec((1, D_MODEL), lambda i: (0, 0)),
                  pl.BlockSpec((D_MODEL, _IN_SPLITS[-1]), lambda i: (0, 0)),
                  table, table, gain, gain],
        out_specs=[o[0] for o in outs],
        out_shape=[o[1] for o in outs],
        scratch_shapes=[pltpu.VMEM((3, ATTN_WIDTH // LANES, tm, LANES), f32)],
        compiler_params=pltpu.CompilerParams(dimension_semantics=("arbitrary",),
                                             vmem_limit_bytes=VMEM_LIMIT),
        name="in_proj",
    )(x, g, w, cos_t, sin_t, qg, kg)


def _mixer_kernel(pool_ref, lru_ref, gate_ref, pw_ref, ps_ref, cw_ref, cb_ref, wri_ref, bri_ref, lam_ref,
                  ypool_ref, ylru_ref,
                  b0, b1, b2, b3, cbuf, a_s, b_s, h_s, a2_s, b2_s, e2_s, e3_s, hc, *, ts):
    j = pl.program_id(1)
    n1 = ts // 8
    n2 = n1 // 8

    @pl.when(j == 0)
    def _():
        for buf in (b0, b1, b2, b3):
            buf[pl.ds(0, POOL_HALO), :] = jnp.zeros((POOL_HALO, POOL_WIDTH), f32)
        cbuf[pl.ds(0, CONV_HALO), :] = jnp.zeros((CONV_HALO, LRU_WIDTH), f32)
        hc[...] = jnp.zeros_like(hc)

    row = lax.broadcasted_iota(i32, (ts, POOL_WIDTH), 0) + j * ts
    grp = lax.broadcasted_iota(i32, (ts, POOL_WIDTH), 1) // (POOL_WIDTH // len(POOL_WINDOWS))

    u = pool_ref[...]
    b0[pl.ds(POOL_HALO, ts), :] = u
    s1 = u + b0[pl.ds(POOL_HALO - 1, ts), :]
    b1[pl.ds(POOL_HALO, ts), :] = s1
    s2 = s1 + b1[pl.ds(POOL_HALO - 2, ts), :]
    b2[pl.ds(POOL_HALO, ts), :] = s2
    s3 = s2 + b2[pl.ds(POOL_HALO - 4, ts), :]
    b3[pl.ds(POOL_HALO, ts), :] = s3
    s4 = s3 + b3[pl.ds(POOL_HALO - 8, ts), :]
    for buf in (b0, b1, b2, b3):
        buf[pl.ds(0, POOL_HALO), :] = buf[pl.ds(ts, POOL_HALO), :]
    sums = jnp.where(grp == 0, s1, jnp.where(grp == 1, s2, jnp.where(grp == 2, s3, s4)))
    win = jnp.where(grp == 0, float(POOL_WINDOWS[0]),
                    jnp.where(grp == 1, float(POOL_WINDOWS[1]),
                              jnp.where(grp == 2, float(POOL_WINDOWS[2]), float(POOL_WINDOWS[3]))))
    cnt = jnp.minimum((row + 1).astype(f32), win)
    pooled = sums / cnt - u
    ypool_ref[...] = jnp.dot(pooled.astype(bf16), pw_ref[...], preferred_element_type=f32) * ps_ref[...]

    lu = lru_ref[...]
    cbuf[pl.ds(CONV_HALO, ts), :] = lu
    xc = (cb_ref[...] + cw_ref[3:4, :] * lu
          + cw_ref[2:3, :] * cbuf[pl.ds(CONV_HALO - 1, ts), :]
          + cw_ref[1:2, :] * cbuf[pl.ds(CONV_HALO - 2, ts), :]
          + cw_ref[0:1, :] * cbuf[pl.ds(CONV_HALO - 3, ts), :])
    cbuf[pl.ds(0, CONV_HALO), :] = cbuf[pl.ds(ts, CONV_HALO), :]
    ri = jnp.dot(xc.astype(bf16), wri_ref[...], preferred_element_type=f32) + bri_ref[...]
    r = jax.nn.sigmoid(ri[:, :LRU_WIDTH])
    ig = jax.nn.sigmoid(ri[:, LRU_WIDTH:])
    log_a = -LRU_C * r * jax.nn.softplus(-lam_ref[...])
    a = jnp.exp(log_a)
    th = jnp.tanh(log_a)
    mult = jnp.sqrt(-2.0 * th / (1.0 - th))
    mult = jnp.where(row == 0, 1.0, mult)
    bb = mult * (ig * xc)

    def rows(ref, hf, s, n):
        return ref[hf, pl.ds(s, n, stride=8), :]

    for hf in range(LRU_WIDTH // LANES):
        a_s[hf] = a[:, hf * LANES:(hf + 1) * LANES]
        b_s[hf] = bb[:, hf * LANES:(hf + 1) * LANES]
        pa, pb = rows(a_s, hf, 0, n1), rows(b_s, hf, 0, n1)
        for s in range(1, 8):
            as_ = rows(a_s, hf, s, n1)
            pb = as_ * pb + rows(b_s, hf, s, n1)
            pa = as_ * pa
        a2_s[hf] = pa
        b2_s[hf] = pb
        pa, pb = rows(a2_s, hf, 0, n2), rows(b2_s, hf, 0, n2)
        for s in range(1, 8):
            as_ = rows(a2_s, hf, s, n2)
            pb = as_ * pb + rows(b2_s, hf, s, n2)
            pa = as_ * pa
        h = hc[hf, 0:1, :]
        for k in range(8):
            e3_s[hf, k:k + 1, :] = h
            h = pa[k:k + 1, :] * h + pb[k:k + 1, :]
        hc[hf, 0:1, :] = h
        prev = e3_s[hf]
        for s in range(8):
            e2_s[hf, pl.ds(s, n2, stride=8), :] = prev
            prev = rows(a2_s, hf, s, n2) * prev + rows(b2_s, hf, s, n2)
        prev = e2_s[hf]
        for s in range(8):
            prev = rows(a_s, hf, s, n1) * prev + rows(b_s, hf, s, n1)
            h_s[hf, pl.ds(s, n1, stride=8), :] = prev

    hfull = jnp.concatenate([h_s[hf] for hf in range(LRU_WIDTH // LANES)], axis=1)
    ylru_ref[...] = hfull * jax.nn.gelu(gate_ref[...])


def _mixer(pool_u, lru_u, gate, pw, ps, cw, cb, wri, bri, lam, batch, seq):
    ts = MIX_TILE
    nt = seq // ts
    t = batch * seq
    nh = LRU_WIDTH // LANES
    tile = lambda w_: pl.BlockSpec((ts, w_), lambda b, j: (b * nt + j, 0))
    full = lambda shape: pl.BlockSpec(shape, lambda b, j: (0,) * len(shape))
    return pl.pallas_call(
        functools.partial(_mixer_kernel, ts=ts),
        grid=(batch, nt),
        in_specs=[tile(POOL_WIDTH), tile(LRU_WIDTH), tile(LRU_WIDTH),
                  full((POOL_WIDTH, POOL_WIDTH)), full((1, POOL_WIDTH)),
                  full((4, LRU_WIDTH)), full((1, LRU_WIDTH)),
                  full((LRU_WIDTH, 2 * LRU_WIDTH)), full((1, 2 * LRU_WIDTH)), full((1, LRU_WIDTH))],
        out_specs=[tile(POOL_WIDTH), tile(LRU_WIDTH)],
        out_shape=[jax.ShapeDtypeStruct((t, POOL_WIDTH), f32), jax.ShapeDtypeStruct((t, LRU_WIDTH), f32)],
        scratch_shapes=[pltpu.VMEM((ts + POOL_HALO, POOL_WIDTH), f32)] * 4
        + [pltpu.VMEM((ts + CONV_HALO, LRU_WIDTH), f32)]
        + [pltpu.VMEM((nh, ts, LANES), f32)] * 3
        + [pltpu.VMEM((nh, ts // 8, LANES), f32)] * 3
        + [pltpu.VMEM((nh, 8, LANES), f32)] * 2,
        compiler_params=pltpu.CompilerParams(dimension_semantics=("arbitrary", "arbitrary"),
                                             vmem_limit_bytes=VMEM_LIMIT),
        name="pool_lru_mixer",
    )(pool_u, lru_u, gate, pw, ps, cw, cb, wri, bri, lam)


def _attn_kernel(q_ref, k_ref, v_ref, o_ref, m_s, l_s, acc_s, *, seq):
    blk = ATTN_BLOCK
    cls = seq // MAX_DIL
    qs, ks, vs = q_ref.at[0], k_ref.at[0], v_ref.at[0]

    h0 = lax.broadcasted_iota(i32, (blk, LANES), 1) < HEAD_DIM
    q_h0 = _pair_head0((blk, LANES))
    state = (m_s, l_s, acc_s)
    qi = lax.broadcasted_iota(i32, (2 * blk, 2 * blk), 0) % blk
    kj = lax.broadcasted_iota(i32, (2 * blk, 2 * blk), 1)

    for bi, d in enumerate(DILATIONS):
        nb = seq // (blk * d)
        runs = MAX_DIL // d
        run = blk // runs

        def member(i, runs=runs, run=run):
            return runs * (i % run) + i // run

        jq = member(qi)
        jk = member(kj % blk)
        mask_cur = (kj >= blk) & (jk <= jq)
        mask_prev = (kj < blk) & (jk >= jq)

        def starts(res, n, d=d, runs=runs, run=run):
            return [(res + d * b, pl.multiple_of(run * n, 8)) for b in range(runs)]

        def ld(ref, row_starts, run=run):
            return jnp.concatenate([ref[c, pl.ds(r0, run), :] for c, r0 in row_starts], axis=0)

        def st(ref, row_starts, val, run=run):
            for b, (c, r0) in enumerate(row_starts):
                ref[c, pl.ds(r0, run), :] = val[b * run:(b + 1) * run, :]

        def body(it, carry, nb=nb, bi=bi, starts=starts, ld=ld, st=st, mask_cur=mask_cur, mask_prev=mask_prev):
            loaded = []
            for u_ in range(ATTN_INTERLEAVE):
                idx = it * ATTN_INTERLEAVE + u_
                res = idx // nb
                n = idx % nb
                cur = starts(res, n)
                prev = starts(res, jnp.maximum(n - 1, 0))
                qb = ld(qs, cur)
                kb = jnp.concatenate([ld(ks, prev), ld(ks, cur)], axis=0).astype(bf16)
                vb = jnp.concatenate([ld(vs, prev), ld(vs, cur)], axis=0).astype(bf16)
                loaded.append((n, cur, qb, kb, vb))
            scores = []
            for n, cur, qb, kb, vb in loaded:
                q2 = jnp.concatenate([jnp.where(q_h0, qb, 0.0), jnp.where(q_h0, 0.0, qb)], axis=0).astype(bf16)
                scores.append(lax.dot_general(q2, kb, (((1,), (1,)), ((), ())), preferred_element_type=f32))
            results = []
            for (n, cur, qb, kb, vb), s in zip(loaded, scores):
                valid = mask_cur | (mask_prev & (n > 0))
                s = jnp.where(valid, s, NEG_INF)
                mb = jnp.max(s, axis=-1, keepdims=True)
                p = jnp.exp(s - mb)
                lb = jnp.sum(p, axis=-1, keepdims=True)
                c = jnp.dot(p.astype(bf16), vb, preferred_element_type=f32)
                m_blk = jnp.where(h0, mb[:blk], mb[blk:])
                l_blk = jnp.where(h0, lb[:blk], lb[blk:])
                n_blk = jnp.where(h0, c[:blk], c[blk:])
                if bi == 0:
                    results.append((cur, m_blk, l_blk, n_blk))
                else:
                    m_old, l_old, a_old = [ld(ref, cur) for ref in state]
                    m_new = jnp.maximum(m_old, m_blk)
                    w_old = jnp.exp(m_old - m_new)
                    w_blk = jnp.exp(m_blk - m_new)
                    results.append((cur, m_new, w_old * l_old + w_blk * l_blk, w_old * a_old + w_blk * n_blk))
            for cur, *vals in results:
                for ref, val in zip(state, vals):
                    st(ref, cur, val)
            return carry

        lax.fori_loop(0, seq // (blk * ATTN_INTERLEAVE), body, 0)

    def fin(r, carry):
        o_ref[0, r] = acc_s[r] / l_s[r]
        return carry

    lax.fori_loop(0, MAX_DIL, fin, 0)


def _attention(q, k, v, batch, seq):
    npair = ATTN_WIDTH // LANES
    cls = seq // MAX_DIL
    tile = pl.BlockSpec((1, MAX_DIL, cls, LANES), lambda b, p: (b, 0, 0, p))
    return pl.pallas_call(
        functools.partial(_attn_kernel, seq=seq),
        grid=(batch, npair),
        in_specs=[tile, tile, tile],
        out_specs=tile,
        out_shape=jax.ShapeDtypeStruct((batch, MAX_DIL, cls, ATTN_WIDTH), f32),
        scratch_shapes=[pltpu.VMEM((MAX_DIL, cls, LANES), f32)] * 3,
        compiler_params=pltpu.CompilerParams(dimension_semantics=("arbitrary", "arbitrary"),
                                             vmem_limit_bytes=VMEM_LIMIT),
        name="dilated_attention",
    )(q, k, v)


def _out_proj_kernel(x_ref, yp_ref, yl_ref, ya_ref, og_ref, wo_ref, g2_ref, rw_ref, rb_ref,
                     xo_ref, idx_ref, gate_ref, rank_ref, cnt_ref, cnt_s, before_s, ya_s, *, tm, steps_per_tile):
    i = pl.program_id(0)
    c0, c1 = POOL_WIDTH, POOL_WIDTH + LRU_WIDTH
    for r in range(MAX_DIL):
        for p_ in range(ATTN_WIDTH // LANES):
            ya_s[p_, pl.ds(r, tm // MAX_DIL, stride=MAX_DIL), :] = ya_ref[0, r, :, p_ * LANES:(p_ + 1) * LANES]
    ya = jnp.concatenate([ya_s[p_] for p_ in range(ATTN_WIDTH // LANES)], axis=1)
    acc = x_ref[...]
    for y, lo, hi in ((yp_ref[...], 0, c0), (yl_ref[...], c0, c1), (ya, c1, D_MODEL)):
        mix = _rms(y, og_ref[:, lo:hi]).astype(bf16)
        acc = acc + jnp.dot(mix, wo_ref[lo:hi, :], preferred_element_type=f32)
    xo_ref[...] = acc

    h2 = _rms(acc, g2_ref[...])
    h_hi = h2.astype(bf16)
    h_lo = (h2 - h_hi.astype(f32)).astype(bf16)
    nt = (((1,), (1,)), ((), ()))
    both = lax.dot_general(rw_ref[...], h_hi, nt, preferred_element_type=f32)
    logits = (both[:N_EXPERTS] + both[N_EXPERTS:]
              + lax.dot_general(rw_ref[:N_EXPERTS, :], h_lo, nt, preferred_element_type=f32)
              + rb_ref[...])
    eio = lax.broadcasted_iota(i32, (N_EXPERTS, tm), 0)
    cur = logits
    vals, idxs = [], []
    for _ in range(TOP_K):
        mx = jnp.max(cur, axis=0, keepdims=True)
        ix = jnp.min(jnp.where(cur == mx, eio, N_EXPERTS), axis=0, keepdims=True)
        vals.append(mx)
        idxs.append(ix)
        cur = jnp.where(eio == ix, -jnp.inf, cur)
    ex = [jnp.exp(v - vals[0]) for v in vals]
    den = ex[0] + ex[1] + ex[2] + ex[3]

    @pl.when(i % steps_per_tile == 0)
    def _():
        cnt_s[...] = jnp.zeros_like(cnt_s)

    onehot = jnp.zeros((N_EXPERTS, tm), f32)
    for ix in idxs:
        onehot = onehot + (eio == ix).astype(f32)
    @pl.when(i == 0)
    def _():
        before_s[...] = (lax.broadcasted_iota(i32, (tm, tm), 0) < lax.broadcasted_iota(i32, (tm, tm), 1)).astype(bf16)

    prefix = jnp.dot(onehot.astype(bf16), before_s[...], preferred_element_type=f32) + cnt_s[:, 0:1]
    for k in range(TOP_K):
        idx_ref[k:k + 1, :] = idxs[k]
        gate_ref[k:k + 1, :] = ex[k] / den
        rank_ref[k:k + 1, :] = jnp.sum(jnp.where(eio == idxs[k], prefix, 0.0), axis=0, keepdims=True).astype(i32)
    cnt_s[...] = cnt_s[...] + jnp.sum(onehot, axis=1, keepdims=True)
    cnt_ref[0] = cnt_s[...]


def _out_proj(x, yp, yl, ya, og, wo, g2, rw_t, rb, seq):
    t = x.shape[0]
    tm = PROJ_TILE
    steps = t // tm
    nt = seq // tm
    tile = lambda w_: pl.BlockSpec((tm, w_), lambda i: (i, 0))
    full = lambda shape: pl.BlockSpec(shape, lambda i: (0,) * len(shape))
    route = pl.BlockSpec((TOP_K, tm), lambda i: (0, i))
    return pl.pallas_call(
        functools.partial(_out_proj_kernel, tm=tm, steps_per_tile=MOE_TILE // tm),
        grid=(steps,),
        in_specs=[tile(D_MODEL), tile(POOL_WIDTH), tile(LRU_WIDTH),
                  pl.BlockSpec((1, MAX_DIL, tm // MAX_DIL, ATTN_WIDTH), lambda i: (i // nt, 0, i % nt, 0)),
                  full((1, D_MODEL)), full((D_MODEL, D_MODEL)), full((1, D_MODEL)),
                  full((2 * N_EXPERTS, D_MODEL)), full((N_EXPERTS, 1))],
        out_specs=[tile(D_MODEL), route, route, route,
                   pl.BlockSpec((1, N_EXPERTS, LANES), lambda i: (i, 0, 0))],
        out_shape=[jax.ShapeDtypeStruct((t, D_MODEL), f32),
                   jax.ShapeDtypeStruct((TOP_K, t), i32),
                   jax.ShapeDtypeStruct((TOP_K, t), f32),
                   jax.ShapeDtypeStruct((TOP_K, t), i32),
                   jax.ShapeDtypeStruct((steps, N_EXPERTS, LANES), f32)],
        scratch_shapes=[pltpu.VMEM((N_EXPERTS, LANES), f32), pltpu.VMEM((tm, tm), bf16),
                        pltpu.VMEM((ATTN_WIDTH // LANES, tm, LANES), f32)],
        compiler_params=pltpu.CompilerParams(dimension_semantics=("arbitrary",),
                                             vmem_limit_bytes=VMEM_LIMIT),
        name="out_proj_router",
    )(x, yp, yl, ya, og, wo, g2, rw_t, rb)


ROW_SLAB = 8
ROUTE_GROUP = 8
MOE_EXPERTS_PER_STEP = 2


def _slab_cols(ref, first_row, n_rows, col):
    return ref[pl.ds(first_row * ROW_SLAB + col, n_rows, stride=ROW_SLAB), :]


def _moe_kernel(off_ref, pos_ref, gate_ref, x_ref, g2_ref, wgu_ref, bgu_ref, wd_ref, bd_ref,
                o_ref, sorted_s, stage_s, *, tt, ch, io_rows):
    i = pl.program_id(0)
    s = pl.program_id(1)
    n_io = tt // io_rows
    n_exp_steps = N_EXPERTS // MOE_EXPERTS_PER_STEP
    base = i * (N_EXPERTS + 1)
    n_col = D_MODEL // LANES

    def slab(ref, first_sublane):
        return ref.at[pl.ds(pl.multiple_of(first_sublane, ROW_SLAB), ROW_SLAB), :]

    @pl.when(s < n_io)
    def _dispatch():
        @pl.when(s == 0)
        def _():
            sorted_s[pl.ds(TOP_K * tt * ROW_SLAB, ch * ROW_SLAB), :] = jnp.zeros((ch * ROW_SLAB, LANES), f32)

        h = _rms(x_ref[...], g2_ref[...])
        for j in range(n_col):
            stage_s[pl.ds(j, io_rows, stride=ROW_SLAB), :] = h[:, j * LANES:(j + 1) * LANES]

        def scatter(c, carry):
            for u_ in range(ROUTE_GROUP):
                row = slab(stage_s, (c * ROUTE_GROUP + u_) * ROW_SLAB)[...]
                for k in range(TOP_K):
                    slab(sorted_s, pos_ref[c, u_ * TOP_K + k])[...] = row
            return carry

        lax.fori_loop(0, io_rows // ROUTE_GROUP, scatter, 0)

    @pl.when((s >= n_io) & (s < n_io + n_exp_steps))
    def _experts():
        def run_chunk(sub, start, hi, rows):
            xc = jnp.concatenate([_slab_cols(sorted_s, start, rows, j) for j in range(n_col)], axis=1)
            gu = jnp.dot(xc.astype(bf16), wgu_ref[sub], preferred_element_type=f32) + bgu_ref[sub]
            gate = jnp.minimum(gu[:, :EXPERT_FF], SWIGLU_LIMIT)
            up = jnp.clip(gu[:, EXPERT_FF:], -SWIGLU_LIMIT, SWIGLU_LIMIT)
            act = (up + 1.0) * gate * jax.nn.sigmoid(SWIGLU_ALPHA * gate)
            y = jnp.dot(act.astype(bf16), wd_ref[sub], preferred_element_type=f32) + bd_ref[sub]
            mine = (lax.broadcasted_iota(i32, (rows, 1), 0) + start) < hi
            y = jnp.where(mine, y, xc)
            for j in range(n_col):
                sorted_s[pl.ds(start * ROW_SLAB + j, rows, stride=ROW_SLAB), :] = y[:, j * LANES:(j + 1) * LANES]

        def expert(sub, carry):
            e = (s - n_io) * MOE_EXPERTS_PER_STEP + sub
            lo = off_ref[base + e]
            hi = off_ref[base + e + 1]
            n_full = jnp.maximum(hi - lo - 1, 0) // ch

            def full(c, carry2):
                run_chunk(sub, lo + c * ch, hi, ch)
                return carry2

            lax.fori_loop(0, n_full, full, 0)
            start = lo + n_full * ch
            rest = hi - start
            for size_lo, size in zip((0,) + MOE_TAIL_SIZES[:-1], MOE_TAIL_SIZES):
                @pl.when((rest > size_lo) & (rest <= size))
                def _(size=size):
                    run_chunk(sub, start, hi, size)
            return carry

        lax.fori_loop(0, MOE_EXPERTS_PER_STEP, expert, 0)

    @pl.when(s >= n_io + n_exp_steps)
    def _combine():
        def gather(c, carry):
            for u_ in range(ROUTE_GROUP):
                acc = gate_ref[c, u_ * TOP_K] * slab(sorted_s, pos_ref[c, u_ * TOP_K])[...]
                for k in range(1, TOP_K):
                    acc = acc + gate_ref[c, u_ * TOP_K + k] * slab(sorted_s, pos_ref[c, u_ * TOP_K + k])[...]
                slab(stage_s, (c * ROUTE_GROUP + u_) * ROW_SLAB)[...] = acc
            return carry

        lax.fori_loop(0, io_rows // ROUTE_GROUP, gather, 0)
        moe = jnp.concatenate([_slab_cols(stage_s, 0, io_rows, j) for j in range(n_col)], axis=1)
        o_ref[...] = x_ref[...] + moe


def _moe(x, pos, gates, offsets, g2, wgu, bgu, wd, bd):
    t = x.shape[0]
    tt = MOE_TILE
    ch = MOE_CHUNK
    io_rows = PROJ_TILE
    n_io = tt // io_rows
    eps = MOE_EXPERTS_PER_STEP
    n_exp_steps = N_EXPERTS // eps
    n_steps = 2 * n_io + n_exp_steps

    def tok_idx(i, s, off):
        in_combine = s >= n_io + n_exp_steps
        return i * n_io + jnp.where(s < n_io, s, jnp.where(in_combine, s - n_io - n_exp_steps, n_io - 1))

    def out_idx(i, s, off):
        return i * n_io + jnp.maximum(s - n_io - n_exp_steps, 0)

    def expert_idx(i, s, off):
        return jnp.clip(s - n_io, 0, n_exp_steps - 1)

    route = pl.BlockSpec((io_rows // ROUTE_GROUP, ROUTE_GROUP * TOP_K), lambda i, s, off: (tok_idx(i, s, off), 0),
                         memory_space=pltpu.SMEM)
    return pl.pallas_call(
        functools.partial(_moe_kernel, tt=tt, ch=ch, io_rows=io_rows),
        grid_spec=pltpu.PrefetchScalarGridSpec(
            num_scalar_prefetch=1,
            grid=(t // tt, n_steps),
            in_specs=[route, route,
                      pl.BlockSpec((io_rows, D_MODEL), lambda i, s, off: (tok_idx(i, s, off), 0)),
                      pl.BlockSpec((1, D_MODEL), lambda i, s, off: (0, 0)),
                      pl.BlockSpec((eps, D_MODEL, 2 * EXPERT_FF), lambda i, s, off: (expert_idx(i, s, off), 0, 0)),
                      pl.BlockSpec((eps, 1, 2 * EXPERT_FF), lambda i, s, off: (expert_idx(i, s, off), 0, 0)),
                      pl.BlockSpec((eps, EXPERT_FF, D_MODEL), lambda i, s, off: (expert_idx(i, s, off), 0, 0)),
                      pl.BlockSpec((eps, 1, D_MODEL), lambda i, s, off: (expert_idx(i, s, off), 0, 0))],
            out_specs=pl.BlockSpec((io_rows, D_MODEL), lambda i, s, off: (out_idx(i, s, off), 0)),
            scratch_shapes=[pltpu.VMEM(((TOP_K * tt + ch) * ROW_SLAB, LANES), f32),
                            pltpu.VMEM((io_rows * ROW_SLAB, LANES), f32)]),
        out_shape=jax.ShapeDtypeStruct((t, D_MODEL), f32),
        compiler_params=pltpu.CompilerParams(dimension_semantics=("arbitrary", "arbitrary"),
                                             vmem_limit_bytes=VMEM_LIMIT),
        name="moe_experts",
    )(offsets, pos, gates, x, g2, wgu, bgu, wd, bd)


def _pair_layout(v):
    first, second = v[..., :QK_QUARTER], v[..., QK_QUARTER:]
    return jnp.concatenate([first, first, second, second], axis=-1)


def _qk_column_order():
    pair = jnp.arange(LANES).reshape(2, 2, QK_QUARTER).transpose(1, 0, 2).reshape(-1)
    cols = jnp.arange(_IN_SPLITS[-1])
    qk = cols[_IN_SPLITS[3]:_IN_SPLITS[5]].reshape(-1, LANES)[:, pair].reshape(-1)
    return jnp.concatenate([cols[:_IN_SPLITS[3]], qk, cols[_IN_SPLITS[5]:]])


def _rope_tables(seq):
    pos = jnp.arange(seq, dtype=f32)
    inv_freq = ROPE_THETA ** (-jnp.arange(0, HEAD_DIM, 2, dtype=f32) / HEAD_DIM)
    ang = pos[:, None] * inv_freq[None, :]
    cos, sin = jnp.cos(ang), jnp.sin(ang)
    return _pair_layout(jnp.concatenate([cos, cos], axis=1)), _pair_layout(jnp.concatenate([-sin, sin], axis=1))


def _layer(x, cos_t, sin_t, batch, seq, p):
    t = batch * seq
    row = lambda v: v.reshape(1, -1)
    pool_u, lru_u, gate, q, k, v = _in_proj(x, row(p["norm1_g"]), p["w_in"][:, _qk_column_order()].astype(bf16),
                                            cos_t, sin_t, row(_pair_layout(p["q_norm_g"])),
                                            row(_pair_layout(p["k_norm_g"])), batch, seq)
    wri = jnp.concatenate([block_diag(*p["lru_wa"]), block_diag(*p["lru_wx"])], axis=1).astype(bf16)
    bri = jnp.concatenate([p["lru_ba"], p["lru_bx"]]).reshape(1, -1)
    y_pool, y_lru = _mixer(pool_u, lru_u, gate, block_diag(*p["pool_w"]).astype(bf16), row(p["pool_scale"]),
                           p["conv_w"], row(p["conv_b"]), wri, bri, row(p["lru_lambda"]), batch, seq)
    y_attn = _attention(q, k, v, batch, seq)
    rw_hi = p["router_w"].T.astype(bf16)
    rw_split = jnp.concatenate([rw_hi, (p["router_w"].T - rw_hi.astype(f32)).astype(bf16)], axis=0)
    x, idx, gates, rank, cnt = _out_proj(x, y_pool, y_lru, y_attn, row(p["out_norm_g"]), p["w_out"].astype(bf16),
                                         row(p["norm2_g"]), rw_split, p["router_b"].reshape(-1, 1), seq)
    steps_per_tile = MOE_TILE // PROJ_TILE
    n_tiles = t // MOE_TILE
    counts = cnt[steps_per_tile - 1::steps_per_tile, :, 0].astype(i32)
    offsets = jnp.concatenate([jnp.zeros((n_tiles, 1), i32), jnp.cumsum(counts, axis=1)], axis=1)
    idx_t = idx.reshape(TOP_K, n_tiles, MOE_TILE)
    seg_start = jnp.zeros_like(idx_t)
    for e in range(N_EXPERTS):
        seg_start = jnp.where(idx_t == e, offsets[None, :, e, None], seg_start)
    pos = (rank + seg_start.reshape(TOP_K, t)) * ROW_SLAB
    to_smem = lambda a: a.T.reshape(t // ROUTE_GROUP, ROUTE_GROUP * TOP_K)
    return _moe(x, to_smem(pos), to_smem(gates), offsets.reshape(-1), row(p["norm2_g"]),
                p["w_gate_up"].astype(bf16), p["b_gate_up"][:, None, :],
                p["w_down"].astype(bf16), p["b_down"][:, None, :])


def kernel(x, norm1_g, w_in, pool_w, pool_scale, conv_w, conv_b, lru_wa, lru_ba, lru_wx, lru_bx, lru_lambda,
           q_norm_g, k_norm_g, out_norm_g, w_out, norm2_g, router_w, router_b, w_gate_up, b_gate_up,
           w_down, b_down):
    batch, seq, d = x.shape
    params = dict(norm1_g=norm1_g, w_in=w_in, pool_w=pool_w, pool_scale=pool_scale, conv_w=conv_w, conv_b=conv_b,
                  lru_wa=lru_wa, lru_ba=lru_ba, lru_wx=lru_wx, lru_bx=lru_bx, lru_lambda=lru_lambda,
                  q_norm_g=q_norm_g, k_norm_g=k_norm_g, out_norm_g=out_norm_g, w_out=w_out, norm2_g=norm2_g,
                  router_w=router_w, router_b=router_b, w_gate_up=w_gate_up, b_gate_up=b_gate_up,
                  w_down=w_down, b_down=b_down)
    cos_t, sin_t = _rope_tables(seq)
    xt = x.reshape(batch * seq, d)
    for layer in range(norm1_g.shape[0]):
        xt = _layer(xt, cos_t, sin_t, batch, seq, {name: val[layer] for name, val in params.items()})
    return xt.reshape(batch, seq, d)
```

```python
import functools

import jax
import jax.numpy as jnp
from jax import lax
from jax.experimental import pallas as pl
from jax.experimental.pallas import tpu as pltpu
from jax.scipy.linalg import block_diag

f32 = jnp.float32
bf16 = jnp.bfloat16
i32 = jnp.int32

D_MODEL = 1024
POOL_WIDTH = 256
POOL_WINDOWS = (2, 4, 8, 16)
LRU_WIDTH = 256
LRU_C = 8.0
ATTN_WIDTH = 512
HEAD_DIM = 64
DILATIONS = (1, 4, 16)
MAX_DIL = DILATIONS[-1]
ATTN_BLOCK = 128
ROPE_THETA = 10000.0
N_EXPERTS = 32
TOP_K = 4
EXPERT_FF = 256
SWIGLU_LIMIT = 7.0
SWIGLU_ALPHA = 1.702
NORM_EPS = 1e-6
NEG_INF = -1e30

LANES = 128
POOL_HALO = 16
CONV_HALO = 8
VMEM_LIMIT = 56 * 1024 * 1024

PROJ_TILE = 512
MIX_TILE = 512
MOE_TILE = 2048
MOE_CHUNK = 512
MOE_TAIL_SIZES = (64, 128, 192, 256, 320, 384, 448, MOE_CHUNK)
ATTN_INTERLEAVE = (8, 4, 4)


def _rms(x, g):
    return x * lax.rsqrt(jnp.mean(x * x, axis=-1, keepdims=True) + NORM_EPS) * g


_IN_SPLITS = (0, 256, 512, 768, 1280, 1792, 2304)


QK_QUARTER = HEAD_DIM // 2


def _pair_head0(shape):
    return (lax.broadcasted_iota(i32, shape, 1) % HEAD_DIM) < QK_QUARTER


def _qk_norm_rope(y, cs, sn, gain, scale):
    head0 = _pair_head0(y.shape)
    yy = y * y
    ms0 = jnp.sum(jnp.where(head0, yy, 0.0), axis=-1, keepdims=True)
    ms1 = jnp.sum(jnp.where(head0, 0.0, yy), axis=-1, keepdims=True)
    ms = jnp.where(head0, ms0, ms1) * (1.0 / HEAD_DIM)
    yn = y * lax.rsqrt(ms + NORM_EPS) * gain
    return (yn * cs + pltpu.roll(yn, HEAD_DIM, 1) * sn) * scale


def _in_proj_kernel(x_ref, g_ref, w_ref, cos_ref, sin_ref, qg_ref, kg_ref, *refs):
    out_refs, stage_s = refs[:-1], refs[-1]
    tm = x_ref.shape[0]
    h = _rms(x_ref[...], g_ref[...]).astype(bf16)

    def proj(n):
        return jnp.dot(h, w_ref[:, _IN_SPLITS[n]:_IN_SPLITS[n + 1]], preferred_element_type=f32)

    def emit_class_major(n, y, post=None):
        for p_, c in enumerate(range(0, y.shape[1], LANES)):
            yp = y[:, c:c + LANES]
            stage_s[n - 3, p_] = yp if post is None else _qk_norm_rope(yp, cos_ref[...], sin_ref[...], *post)
            for r in range(MAX_DIL):
                out_refs[n][0, r, :, c:c + LANES] = stage_s[n - 3, p_, pl.ds(r, tm // MAX_DIL, stride=MAX_DIL), :]

    q = proj(3)
    k = proj(4)
    emit_class_major(3, q, (qg_ref[...], HEAD_DIM ** -0.5))
    v = proj(5)
    out_refs[0][...] = proj(0)
    emit_class_major(4, k, (kg_ref[...], 1.0))
    out_refs[1][...] = proj(1)
    emit_class_major(5, v)
    out_refs[2][...] = proj(2)


def _in_proj(x, g, w, cos_t, sin_t, qg, kg, batch, seq):
    t = x.shape[0]
    tm = PROJ_TILE
    nt = seq // tm
    widths = [c1 - c0 for c0, c1 in zip(_IN_SPLITS[:-1], _IN_SPLITS[1:])]
    table = pl.BlockSpec((tm, LANES), lambda i: (i % nt, 0))
    gain = pl.BlockSpec((1, LANES), lambda i: (0, 0))
    flat = lambda w_: (pl.BlockSpec((tm, w_), lambda i: (i, 0)), jax.ShapeDtypeStruct((t, w_), f32))
    cls_major = lambda w_: (pl.BlockSpec((1, MAX_DIL, tm // MAX_DIL, w_), lambda i: (i // nt, 0, i % nt, 0)),
                            jax.ShapeDtypeStruct((batch, MAX_DIL, seq // MAX_DIL, w_), f32))
    outs = [flat(w_) for w_ in widths[:3]] + [cls_major(w_) for w_ in widths[3:]]
    return pl.pallas_call(
        _in_proj_kernel,
        grid=(t // tm,),
        in_specs=[pl.BlockSpec((tm, D_MODEL), lambda i: (i, 0)),
                  pl.BlockSpec((1, D_MODEL), lambda i: (0, 0)),
                  pl.BlockSpec((D_MODEL, _IN_SPLITS[-1]), lambda i: (0, 0)),
                  table, table, gain, gain],
        out_specs=[o[0] for o in outs],
        out_shape=[o[1] for o in outs],
        scratch_shapes=[pltpu.VMEM((3, ATTN_WIDTH // LANES, tm, LANES), f32)],
        compiler_params=pltpu.CompilerParams(dimension_semantics=("arbitrary",),
                                             vmem_limit_bytes=VMEM_LIMIT),
        name="in_proj",
    )(x, g, w, cos_t, sin_t, qg, kg)


def _mixer_kernel(pool_ref, lru_ref, gate_ref, pw_ref, ps_ref, cw_ref, cb_ref, wri_ref, bri_ref, lam_ref,
                  ypool_ref, ylru_ref,
                  b0, b1, b2, b3, cbuf, a_s, b_s, h_s, a2_s, b2_s, e2_s, e3_s, hc, *, ts):
    j = pl.program_id(1)
    n1 = ts // 8
    n2 = n1 // 8

    @pl.when(j == 0)
    def _():
        for buf in (b0, b1, b2, b3):
            buf[pl.ds(0, POOL_HALO), :] = jnp.zeros((POOL_HALO, POOL_WIDTH), f32)
        cbuf[pl.ds(0, CONV_HALO), :] = jnp.zeros((CONV_HALO, LRU_WIDTH), f32)
        hc[...] = jnp.zeros_like(hc)

    row = lax.broadcasted_iota(i32, (ts, POOL_WIDTH), 0) + j * ts
    grp = lax.broadcasted_iota(i32, (ts, POOL_WIDTH), 1) // (POOL_WIDTH // len(POOL_WINDOWS))

    u = pool_ref[...]
    b0[pl.ds(POOL_HALO, ts), :] = u
    s1 = u + b0[pl.ds(POOL_HALO - 1, ts), :]
    b1[pl.ds(POOL_HALO, ts), :] = s1
    s2 = s1 + b1[pl.ds(POOL_HALO - 2, ts), :]
    b2[pl.ds(POOL_HALO, ts), :] = s2
    s3 = s2 + b2[pl.ds(POOL_HALO - 4, ts), :]
    b3[pl.ds(POOL_HALO, ts), :] = s3
    s4 = s3 + b3[pl.ds(POOL_HALO - 8, ts), :]
    for buf in (b0, b1, b2, b3):
        buf[pl.ds(0, POOL_HALO), :] = buf[pl.ds(ts, POOL_HALO), :]
    sums = jnp.where(grp == 0, s1, jnp.where(grp == 1, s2, jnp.where(grp == 2, s3, s4)))
    win = jnp.where(grp == 0, float(POOL_WINDOWS[0]),
                    jnp.where(grp == 1, float(POOL_WINDOWS[1]),
                              jnp.where(grp == 2, float(POOL_WINDOWS[2]), float(POOL_WINDOWS[3]))))
    cnt = jnp.minimum((row + 1).astype(f32), win)
    pooled = sums / cnt - u
    ypool_ref[...] = jnp.dot(pooled.astype(bf16), pw_ref[...], preferred_element_type=f32) * ps_ref[...]

    lu = lru_ref[...]
    cbuf[pl.ds(CONV_HALO, ts), :] = lu
    xc = (cb_ref[...] + cw_ref[3:4, :] * lu
          + cw_ref[2:3, :] * cbuf[pl.ds(CONV_HALO - 1, ts), :]
          + cw_ref[1:2, :] * cbuf[pl.ds(CONV_HALO - 2, ts), :]
          + cw_ref[0:1, :] * cbuf[pl.ds(CONV_HALO - 3, ts), :])
    cbuf[pl.ds(0, CONV_HALO), :] = cbuf[pl.ds(ts, CONV_HALO), :]
    ri = jnp.dot(xc.astype(bf16), wri_ref[...], preferred_element_type=f32) + bri_ref[...]
    r = jax.nn.sigmoid(ri[:, :LRU_WIDTH])
    ig = jax.nn.sigmoid(ri[:, LRU_WIDTH:])
    log_a = -LRU_C * r * jax.nn.softplus(-lam_ref[...])
    a = jnp.exp(log_a)
    th = jnp.tanh(log_a)
    mult = jnp.sqrt(-2.0 * th / (1.0 - th))
    mult = jnp.where(row == 0, 1.0, mult)
    bb = mult * (ig * xc)

    def rows(ref, hf, s, n):
        return ref[hf, pl.ds(s, n, stride=8), :]

    for hf in range(LRU_WIDTH // LANES):
        a_s[hf] = a[:, hf * LANES:(hf + 1) * LANES]
        b_s[hf] = bb[:, hf * LANES:(hf + 1) * LANES]
        pa, pb = rows(a_s, hf, 0, n1), rows(b_s, hf, 0, n1)
        for s in range(1, 8):
            as_ = rows(a_s, hf, s, n1)
            pb = as_ * pb + rows(b_s, hf, s, n1)
            pa = as_ * pa
        a2_s[hf] = pa
        b2_s[hf] = pb
        pa, pb = rows(a2_s, hf, 0, n2), rows(b2_s, hf, 0, n2)
        for s in range(1, 8):
            as_ = rows(a2_s, hf, s, n2)
            pb = as_ * pb + rows(b2_s, hf, s, n2)
            pa = as_ * pa
        h = hc[hf, 0:1, :]
        for k in range(8):
            e3_s[hf, k:k + 1, :] = h
            h = pa[k:k + 1, :] * h + pb[k:k + 1, :]
        hc[hf, 0:1, :] = h
        prev = e3_s[hf]
        for s in range(8):
            e2_s[hf, pl.ds(s, n2, stride=8), :] = prev
            prev = rows(a2_s, hf, s, n2) * prev + rows(b2_s, hf, s, n2)
        prev = e2_s[hf]
        for s in range(8):
            prev = rows(a_s, hf, s, n1) * prev + rows(b_s, hf, s, n1)
            h_s[hf, pl.ds(s, n1, stride=8), :] = prev

    hfull = jnp.concatenate([h_s[hf] for hf in range(LRU_WIDTH // LANES)], axis=1)
    ylru_ref[...] = hfull * jax.nn.gelu(gate_ref[...])


def _mixer(pool_u, lru_u, gate, pw, ps, cw, cb, wri, bri, lam, batch, seq):
    ts = MIX_TILE
    nt = seq // ts
    t = batch * seq
    nh = LRU_WIDTH // LANES
    tile = lambda w_: pl.BlockSpec((ts, w_), lambda b, j: (b * nt + j, 0))
    full = lambda shape: pl.BlockSpec(shape, lambda b, j: (0,) * len(shape))
    return pl.pallas_call(
        functools.partial(_mixer_kernel, ts=ts),
        grid=(batch, nt),
        in_specs=[tile(POOL_WIDTH), tile(LRU_WIDTH), tile(LRU_WIDTH),
                  full((POOL_WIDTH, POOL_WIDTH)), full((1, POOL_WIDTH)),
                  full((4, LRU_WIDTH)), full((1, LRU_WIDTH)),
                  full((LRU_WIDTH, 2 * LRU_WIDTH)), full((1, 2 * LRU_WIDTH)), full((1, LRU_WIDTH))],
        out_specs=[tile(POOL_WIDTH), tile(LRU_WIDTH)],
        out_shape=[jax.ShapeDtypeStruct((t, POOL_WIDTH), f32), jax.ShapeDtypeStruct((t, LRU_WIDTH), f32)],
        scratch_shapes=[pltpu.VMEM((ts + POOL_HALO, POOL_WIDTH), f32)] * 4
        + [pltpu.VMEM((ts + CONV_HALO, LRU_WIDTH), f32)]
        + [pltpu.VMEM((nh, ts, LANES), f32)] * 3
        + [pltpu.VMEM((nh, ts // 8, LANES), f32)] * 3
        + [pltpu.VMEM((nh, 8, LANES), f32)] * 2,
        compiler_params=pltpu.CompilerParams(dimension_semantics=("arbitrary", "arbitrary"),
                                             vmem_limit_bytes=VMEM_LIMIT),
        name="pool_lru_mixer",
    )(pool_u, lru_u, gate, pw, ps, cw, cb, wri, bri, lam)


def _attn_kernel(q_ref, k_ref, v_ref, o_ref, m_s, l_s, acc_s, *, seq):
    blk = ATTN_BLOCK
    cls = seq // MAX_DIL
    qs, ks, vs = q_ref.at[0], k_ref.at[0], v_ref.at[0]

    h0 = lax.broadcasted_iota(i32, (blk, LANES), 1) < HEAD_DIM
    q_h0 = _pair_head0((blk, LANES))
    state = (m_s, l_s, acc_s)
    qi = lax.broadcasted_iota(i32, (2 * blk, 2 * blk), 0) % blk
    kj = lax.broadcasted_iota(i32, (2 * blk, 2 * blk), 1)

    for bi, d in enumerate(DILATIONS):
        nb = seq // (blk * d)
        runs = MAX_DIL // d
        run = blk // runs

        def member(i, runs=runs, run=run):
            return runs * (i % run) + i // run

        jq = member(qi)
        jk = member(kj % blk)
        mask_cur = (kj >= blk) & (jk <= jq)
        mask_prev = (kj < blk) & (jk >= jq)

        def starts(res, n, d=d, runs=runs, run=run):
            return [(res + d * b, pl.multiple_of(run * n, 8)) for b in range(runs)]

        def ld(ref, row_starts, run=run):
            return jnp.concatenate([ref[c, pl.ds(r0, run), :] for c, r0 in row_starts], axis=0)

        def st(ref, row_starts, val, run=run):
            for b, (c, r0) in enumerate(row_starts):
                ref[c, pl.ds(r0, run), :] = val[b * run:(b + 1) * run, :]

        def body(it, carry, nb=nb, bi=bi, starts=starts, ld=ld, st=st, mask_cur=mask_cur, mask_prev=mask_prev):
            loaded = []
            for u_ in range(ATTN_INTERLEAVE[bi]):
                idx = it * ATTN_INTERLEAVE[bi] + u_
                res = idx // nb
                n = idx % nb
                cur = starts(res, n)
                prev = starts(res, jnp.maximum(n - 1, 0))
                qb = ld(qs, cur)
                kb = jnp.concatenate([ld(ks, prev), ld(ks, cur)], axis=0).astype(bf16)
                vb = jnp.concatenate([ld(vs, prev), ld(vs, cur)], axis=0).astype(bf16)
                loaded.append((n, cur, qb, kb, vb))
            scores = []
            for n, cur, qb, kb, vb in loaded:
                q2 = jnp.concatenate([jnp.where(q_h0, qb, 0.0), jnp.where(q_h0, 0.0, qb)], axis=0).astype(bf16)
                scores.append(lax.dot_general(q2, kb, (((1,), (1,)), ((), ())), preferred_element_type=f32))
            results = []
            for (n, cur, qb, kb, vb), s in zip(loaded, scores):
                valid = mask_cur | (mask_prev & (n > 0))
                s = jnp.where(valid, s, NEG_INF)
                mb = jnp.max(s, axis=-1, keepdims=True)
                p = jnp.exp(s - mb)
                lb = jnp.sum(p, axis=-1, keepdims=True)
                c = jnp.dot(p.astype(bf16), vb, preferred_element_type=f32)
                m_blk = jnp.where(h0, mb[:blk], mb[blk:])
                l_blk = jnp.where(h0, lb[:blk], lb[blk:])
                n_blk = jnp.where(h0, c[:blk], c[blk:])
                if bi == 0:
                    results.append((cur, m_blk, l_blk, n_blk))
                else:
                    m_old, l_old, a_old = [ld(ref, cur) for ref in state]
                    m_new = jnp.maximum(m_old, m_blk)
                    w_old = jnp.exp(m_old - m_new)
                    w_blk = jnp.exp(m_blk - m_new)
                    results.append((cur, m_new, w_old * l_old + w_blk * l_blk, w_old * a_old + w_blk * n_blk))
            for cur, *vals in results:
                for ref, val in zip(state, vals):
                    st(ref, cur, val)
            return carry

        lax.fori_loop(0, seq // (blk * ATTN_INTERLEAVE[bi]), body, 0)

    def fin(r, carry):
        o_ref[0, r] = acc_s[r] / l_s[r]
        return carry

    lax.fori_loop(0, MAX_DIL, fin, 0)


def _attention(q, k, v, batch, seq):
    npair = ATTN_WIDTH // LANES
    cls = seq // MAX_DIL
    tile = pl.BlockSpec((1, MAX_DIL, cls, LANES), lambda b, p: (b, 0, 0, p))
    return pl.pallas_call(
        functools.partial(_attn_kernel, seq=seq),
        grid=(batch, npair),
        in_specs=[tile, tile, tile],
        out_specs=tile,
        out_shape=jax.ShapeDtypeStruct((batch, MAX_DIL, cls, ATTN_WIDTH), f32),
        scratch_shapes=[pltpu.VMEM((MAX_DIL, cls, LANES), f32)] * 3,
        compiler_params=pltpu.CompilerParams(dimension_semantics=("arbitrary", "arbitrary"),
                                             vmem_limit_bytes=VMEM_LIMIT),
        name="dilated_attention",
    )(q, k, v)


def _out_proj_kernel(x_ref, yp_ref, yl_ref, ya_ref, og_ref, wo_ref, g2_ref, rw_ref, rb_ref,
                     xo_ref, idx_ref, gate_ref, rank_ref, cnt_ref, cnt_s, before_s, ya_s, *, tm, steps_per_tile):
    i = pl.program_id(0)
    c0, c1 = POOL_WIDTH, POOL_WIDTH + LRU_WIDTH
    for r in range(MAX_DIL):
        for p_ in range(ATTN_WIDTH // LANES):
            ya_s[p_, pl.ds(r, tm // MAX_DIL, stride=MAX_DIL), :] = ya_ref[0, r, :, p_ * LANES:(p_ + 1) * LANES]
    ya = jnp.concatenate([ya_s[p_] for p_ in range(ATTN_WIDTH // LANES)], axis=1)
    acc = x_ref[...]
    for y, lo, hi in ((yp_ref[...], 0, c0), (yl_ref[...], c0, c1), (ya, c1, D_MODEL)):
        mix = _rms(y, og_ref[:, lo:hi]).astype(bf16)
        acc = acc + jnp.dot(mix, wo_ref[lo:hi, :], preferred_element_type=f32)
    xo_ref[...] = acc

    h2 = _rms(acc, g2_ref[...])
    h_hi = h2.astype(bf16)
    h_lo = (h2 - h_hi.astype(f32)).astype(bf16)
    nt = (((1,), (1,)), ((), ()))
    both = lax.dot_general(rw_ref[...], h_hi, nt, preferred_element_type=f32)
    logits = (both[:N_EXPERTS] + both[N_EXPERTS:]
              + lax.dot_general(rw_ref[:N_EXPERTS, :], h_lo, nt, preferred_element_type=f32)
              + rb_ref[...])
    eio = lax.broadcasted_iota(i32, (N_EXPERTS, tm), 0)
    cur = logits
    vals, idxs = [], []
    for _ in range(TOP_K):
        mx = jnp.max(cur, axis=0, keepdims=True)
        ix = jnp.min(jnp.where(cur == mx, eio, N_EXPERTS), axis=0, keepdims=True)
        vals.append(mx)
        idxs.append(ix)
        cur = jnp.where(eio == ix, -jnp.inf, cur)
    ex = [jnp.exp(v - vals[0]) for v in vals]
    den = ex[0] + ex[1] + ex[2] + ex[3]

    @pl.when(i % steps_per_tile == 0)
    def _():
        cnt_s[...] = jnp.zeros_like(cnt_s)

    onehot = jnp.zeros((N_EXPERTS, tm), f32)
    for ix in idxs:
        onehot = onehot + (eio == ix).astype(f32)
    @pl.when(i == 0)
    def _():
        before_s[...] = (lax.broadcasted_iota(i32, (tm, tm), 0) < lax.broadcasted_iota(i32, (tm, tm), 1)).astype(bf16)

    prefix = jnp.dot(onehot.astype(bf16), before_s[...], preferred_element_type=f32) + cnt_s[:, 0:1]
    for k in range(TOP_K):
        idx_ref[k:k + 1, :] = idxs[k]
        gate_ref[k:k + 1, :] = ex[k] / den
        rank_ref[k:k + 1, :] = jnp.sum(jnp.where(eio == idxs[k], prefix, 0.0), axis=0, keepdims=True).astype(i32)
    cnt_s[...] = cnt_s[...] + jnp.sum(onehot, axis=1, keepdims=True)
    cnt_ref[0] = cnt_s[...]


def _out_proj(x, yp, yl, ya, og, wo, g2, rw_t, rb, seq):
    t = x.shape[0]
    tm = PROJ_TILE
    steps = t // tm
    nt = seq // tm
    tile = lambda w_: pl.BlockSpec((tm, w_), lambda i: (i, 0))
    full = lambda shape: pl.BlockSpec(shape, lambda i: (0,) * len(shape))
    route = pl.BlockSpec((TOP_K, tm), lambda i: (0, i))
    return pl.pallas_call(
        functools.partial(_out_proj_kernel, tm=tm, steps_per_tile=MOE_TILE // tm),
        grid=(steps,),
        in_specs=[tile(D_MODEL), tile(POOL_WIDTH), tile(LRU_WIDTH),
                  pl.BlockSpec((1, MAX_DIL, tm // MAX_DIL, ATTN_WIDTH), lambda i: (i // nt, 0, i % nt, 0)),
                  full((1, D_MODEL)), full((D_MODEL, D_MODEL)), full((1, D_MODEL)),
                  full((2 * N_EXPERTS, D_MODEL)), full((N_EXPERTS, 1))],
        out_specs=[tile(D_MODEL), route, route, route,
                   pl.BlockSpec((1, N_EXPERTS, LANES), lambda i: (i, 0, 0))],
        out_shape=[jax.ShapeDtypeStruct((t, D_MODEL), f32),
                   jax.ShapeDtypeStruct((TOP_K, t), i32),
                   jax.ShapeDtypeStruct((TOP_K, t), f32),
                   jax.ShapeDtypeStruct((TOP_K, t), i32),
                   jax.ShapeDtypeStruct((steps, N_EXPERTS, LANES), f32)],
        scratch_shapes=[pltpu.VMEM((N_EXPERTS, LANES), f32), pltpu.VMEM((tm, tm), bf16),
                        pltpu.VMEM((ATTN_WIDTH // LANES, tm, LANES), f32)],
        compiler_params=pltpu.CompilerParams(dimension_semantics=("arbitrary",),
                                             vmem_limit_bytes=VMEM_LIMIT),
        name="out_proj_router",
    )(x, yp, yl, ya, og, wo, g2, rw_t, rb)


ROW_SLAB = 8
ROUTE_GROUP = 8
MOE_EXPERTS_PER_STEP = 2


def _slab_cols(ref, first_row, n_rows, col):
    return ref[pl.ds(first_row * ROW_SLAB + col, n_rows, stride=ROW_SLAB), :]


def _moe_kernel(off_ref, pos_ref, gate_ref, x_ref, g2_ref, wgu_ref, bgu_ref, wd_ref, bd_ref,
                o_ref, sorted_s, stage_s, *, tt, ch, io_rows):
    i = pl.program_id(0)
    s = pl.program_id(1)
    n_io = tt // io_rows
    n_exp_steps = N_EXPERTS // MOE_EXPERTS_PER_STEP
    base = i * (N_EXPERTS + 1)
    n_col = D_MODEL // LANES

    def slab(ref, first_sublane):
        return ref.at[pl.ds(pl.multiple_of(first_sublane, ROW_SLAB), ROW_SLAB), :]

    @pl.when(s < n_io)
    def _dispatch():
        @pl.when(s == 0)
        def _():
            sorted_s[pl.ds(TOP_K * tt * ROW_SLAB, ch * ROW_SLAB), :] = jnp.zeros((ch * ROW_SLAB, LANES), f32)

        h = _rms(x_ref[...], g2_ref[...])
        for j in range(n_col):
            stage_s[pl.ds(j, io_rows, stride=ROW_SLAB), :] = h[:, j * LANES:(j + 1) * LANES]

        def scatter(c, carry):
            for u_ in range(ROUTE_GROUP):
                row = slab(stage_s, (c * ROUTE_GROUP + u_) * ROW_SLAB)[...]
                for k in range(TOP_K):
                    slab(sorted_s, pos_ref[c, u_ * TOP_K + k])[...] = row
            return carry

        lax.fori_loop(0, io_rows // ROUTE_GROUP, scatter, 0)

    @pl.when((s >= n_io) & (s < n_io + n_exp_steps))
    def _experts():
        def run_chunk(sub, start, hi, rows):
            xc = jnp.concatenate([_slab_cols(sorted_s, start, rows, j) for j in range(n_col)], axis=1)
            gu = jnp.dot(xc.astype(bf16), wgu_ref[sub], preferred_element_type=f32) + bgu_ref[sub]
            gate = jnp.minimum(gu[:, :EXPERT_FF], SWIGLU_LIMIT)
            up = jnp.clip(gu[:, EXPERT_FF:], -SWIGLU_LIMIT, SWIGLU_LIMIT)
            act = (up + 1.0) * gate * jax.nn.sigmoid(SWIGLU_ALPHA * gate)
            y = jnp.dot(act.astype(bf16), wd_ref[sub], preferred_element_type=f32) + bd_ref[sub]
            mine = (lax.broadcasted_iota(i32, (rows, 1), 0) + start) < hi
            y = jnp.where(mine, y, xc)
            for j in range(n_col):
                sorted_s[pl.ds(start * ROW_SLAB + j, rows, stride=ROW_SLAB), :] = y[:, j * LANES:(j + 1) * LANES]

        def expert(sub, carry):
            e = (s - n_io) * MOE_EXPERTS_PER_STEP + sub
            lo = off_ref[base + e]
            hi = off_ref[base + e + 1]
            n_full = jnp.maximum(hi - lo - 1, 0) // ch

            def full(c, carry2):
                run_chunk(sub, lo + c * ch, hi, ch)
                return carry2

            lax.fori_loop(0, n_full, full, 0)
            start = lo + n_full * ch
            rest = hi - start
            for size_lo, size in zip((0,) + MOE_TAIL_SIZES[:-1], MOE_TAIL_SIZES):
                @pl.when((rest > size_lo) & (rest <= size))
                def _(size=size):
                    run_chunk(sub, start, hi, size)
            return carry

        lax.fori_loop(0, MOE_EXPERTS_PER_STEP, expert, 0)

    @pl.when(s >= n_io + n_exp_steps)
    def _combine():
        def gather(c, carry):
            for u_ in range(ROUTE_GROUP):
                acc = gate_ref[c, u_ * TOP_K] * slab(sorted_s, pos_ref[c, u_ * TOP_K])[...]
                for k in range(1, TOP_K):
                    acc = acc + gate_ref[c, u_ * TOP_K + k] * slab(sorted_s, pos_ref[c, u_ * TOP_K + k])[...]
                slab(stage_s, (c * ROUTE_GROUP + u_) * ROW_SLAB)[...] = acc
            return carry

        lax.fori_loop(0, io_rows // ROUTE_GROUP, gather, 0)
        moe = jnp.concatenate([_slab_cols(stage_s, 0, io_rows, j) for j in range(n_col)], axis=1)
        o_ref[...] = x_ref[...] + moe


def _moe(x, pos, gates, offsets, g2, wgu, bgu, wd, bd):
    t = x.shape[0]
    tt = MOE_TILE
    ch = MOE_CHUNK
    io_rows = PROJ_TILE
    n_io = tt // io_rows
    eps = MOE_EXPERTS_PER_STEP
    n_exp_steps = N_EXPERTS // eps
    n_steps = 2 * n_io + n_exp_steps

    def tok_idx(i, s, off):
        in_combine = s >= n_io + n_exp_steps
        return i * n_io + jnp.where(s < n_io, s, jnp.where(in_combine, s - n_io - n_exp_steps, n_io - 1))

    def out_idx(i, s, off):
        return i * n_io + jnp.maximum(s - n_io - n_exp_steps, 0)

    def expert_idx(i, s, off):
        return jnp.clip(s - n_io, 0, n_exp_steps - 1)

    route = pl.BlockSpec((io_rows // ROUTE_GROUP, ROUTE_GROUP * TOP_K), lambda i, s, off: (tok_idx(i, s, off), 0),
                         memory_space=pltpu.SMEM)
    return pl.pallas_call(
        functools.partial(_moe_kernel, tt=tt, ch=ch, io_rows=io_rows),
        grid_spec=pltpu.PrefetchScalarGridSpec(
            num_scalar_prefetch=1,
            grid=(t // tt, n_steps),
            in_specs=[route, route,
                      pl.BlockSpec((io_rows, D_MODEL), lambda i, s, off: (tok_idx(i, s, off), 0)),
                      pl.BlockSpec((1, D_MODEL), lambda i, s, off: (0, 0)),
                      pl.BlockSpec((eps, D_MODEL, 2 * EXPERT_FF), lambda i, s, off: (expert_idx(i, s, off), 0, 0)),
                      pl.BlockSpec((eps, 1, 2 * EXPERT_FF), lambda i, s, off: (expert_idx(i, s, off), 0, 0)),
                      pl.BlockSpec((eps, EXPERT_FF, D_MODEL), lambda i, s, off: (expert_idx(i, s, off), 0, 0)),
                      pl.BlockSpec((eps, 1, D_MODEL), lambda i, s, off: (expert_idx(i, s, off), 0, 0))],
            out_specs=pl.BlockSpec((io_rows, D_MODEL), lambda i, s, off: (out_idx(i, s, off), 0)),
            scratch_shapes=[pltpu.VMEM(((TOP_K * tt + ch) * ROW_SLAB, LANES), f32),
                            pltpu.VMEM((io_rows * ROW_SLAB, LANES), f32)]),
        out_shape=jax.ShapeDtypeStruct((t, D_MODEL), f32),
        compiler_params=pltpu.CompilerParams(dimension_semantics=("arbitrary", "arbitrary"),
                                             vmem_limit_bytes=VMEM_LIMIT),
        name="moe_experts",
    )(offsets, pos, gates, x, g2, wgu, bgu, wd, bd)


def _pair_layout(v):
    first, second = v[..., :QK_QUARTER], v[..., QK_QUARTER:]
    return jnp.concatenate([first, first, second, second], axis=-1)


def _qk_column_order():
    pair = jnp.arange(LANES).reshape(2, 2, QK_QUARTER).transpose(1, 0, 2).reshape(-1)
    cols = jnp.arange(_IN_SPLITS[-1])
    qk = cols[_IN_SPLITS[3]:_IN_SPLITS[5]].reshape(-1, LANES)[:, pair].reshape(-1)
    return jnp.concatenate([cols[:_IN_SPLITS[3]], qk, cols[_IN_SPLITS[5]:]])


def _rope_tables(seq):
    pos = jnp.arange(seq, dtype=f32)
    inv_freq = ROPE_THETA ** (-jnp.arange(0, HEAD_DIM, 2, dtype=f32) / HEAD_DIM)
    ang = pos[:, None] * inv_freq[None, :]
    cos, sin = jnp.cos(ang), jnp.sin(ang)
    return _pair_layout(jnp.concatenate([cos, cos], axis=1)), _pair_layout(jnp.concatenate([-sin, sin], axis=1))


def _layer(x, cos_t, sin_t, batch, seq, p):
    t = batch * seq
    row = lambda v: v.reshape(1, -1)
    pool_u, lru_u, gate, q, k, v = _in_proj(x, row(p["norm1_g"]), p["w_in"][:, _qk_column_order()].astype(bf16),
                                            cos_t, sin_t, row(_pair_layout(p["q_norm_g"])),
                                            row(_pair_layout(p["k_norm_g"])), batch, seq)
    wri = jnp.concatenate([block_diag(*p["lru_wa"]), block_diag(*p["lru_wx"])], axis=1).astype(bf16)
    bri = jnp.concatenate([p["lru_ba"], p["lru_bx"]]).reshape(1, -1)
    y_pool, y_lru = _mixer(pool_u, lru_u, gate, block_diag(*p["pool_w"]).astype(bf16), row(p["pool_scale"]),
                           p["conv_w"], row(p["conv_b"]), wri, bri, row(p["lru_lambda"]), batch, seq)
    y_attn = _attention(q, k, v, batch, seq)
    rw_hi = p["router_w"].T.astype(bf16)
    rw_split = jnp.concatenate([rw_hi, (p["router_w"].T - rw_hi.astype(f32)).astype(bf16)], axis=0)
    x, idx, gates, rank, cnt = _out_proj(x, y_pool, y_lru, y_attn, row(p["out_norm_g"]), p["w_out"].astype(bf16),
                                         row(p["norm2_g"]), rw_split, p["router_b"].reshape(-1, 1), seq)
    steps_per_tile = MOE_TILE // PROJ_TILE
    n_tiles = t // MOE_TILE
    counts = cnt[steps_per_tile - 1::steps_per_tile, :, 0].astype(i32)
    offsets = jnp.concatenate([jnp.zeros((n_tiles, 1), i32), jnp.cumsum(counts, axis=1)], axis=1)
    idx_t = idx.reshape(TOP_K, n_tiles, MOE_TILE)
    seg_start = jnp.zeros_like(idx_t)
    for e in range(N_EXPERTS):
        seg_start = jnp.where(idx_t == e, offsets[None, :, e, None], seg_start)
    pos = (rank + seg_start.reshape(TOP_K, t)) * ROW_SLAB
    to_smem = lambda a: a.T.reshape(t // ROUTE_GROUP, ROUTE_GROUP * TOP_K)
    return _moe(x, to_smem(pos), to_smem(gates), offsets.reshape(-1), row(p["norm2_g"]),
                p["w_gate_up"].astype(bf16), p["b_gate_up"][:, None, :],
                p["w_down"].astype(bf16), p["b_down"][:, None, :])


def kernel(x, norm1_g, w_in, pool_w, pool_scale, conv_w, conv_b, lru_wa, lru_ba, lru_wx, lru_bx, lru_lambda,
           q_norm_g, k_norm_g, out_norm_g, w_out, norm2_g, router_w, router_b, w_gate_up, b_gate_up,
           w_down, b_down):
    batch, seq, d = x.shape
    params = dict(norm1_g=norm1_g, w_in=w_in, pool_w=pool_w, pool_scale=pool_scale, conv_w=conv_w, conv_b=conv_b,
                  lru_wa=lru_wa, lru_ba=lru_ba, lru_wx=lru_wx, lru_bx=lru_bx, lru_lambda=lru_lambda,
                  q_norm_g=q_norm_g, k_norm_g=k_norm_g, out_norm_g=out_norm_g, w_out=w_out, norm2_g=norm2_g,
                  router_w=router_w, router_b=router_b, w_gate_up=w_gate_up, b_gate_up=b_gate_up,
                  w_down=w_down, b_down=b_down)
    cos_t, sin_t = _rope_tables(seq)
    xt = x.reshape(batch * seq, d)
    for layer in range(norm1_g.shape[0]):
        xt = _layer(xt, cos_t, sin_t, batch, seq, {name: val[layer] for name, val in params.items()})
    return xt.reshape(batch, seq, d)
```

```python
import functools

import jax
import jax.numpy as jnp
from jax import lax
from jax.experimental import pallas as pl
from jax.experimental.pallas import tpu as pltpu
from jax.scipy.linalg import block_diag

f32 = jnp.float32
bf16 = jnp.bfloat16
i32 = jnp.int32

D_MODEL = 1024
POOL_WIDTH = 256
POOL_WINDOWS = (2, 4, 8, 16)
LRU_WIDTH = 256
LRU_C = 8.0
ATTN_WIDTH = 512
HEAD_DIM = 64
DILATIONS = (1, 4, 16)
MAX_DIL = DILATIONS[-1]
ATTN_BLOCK = 128
ROPE_THETA = 10000.0
N_EXPERTS = 32
TOP_K = 4
EXPERT_FF = 256
SWIGLU_LIMIT = 7.0
SWIGLU_ALPHA = 1.702
NORM_EPS = 1e-6
NEG_INF = -1e30

LANES = 128
POOL_HALO = 16
CONV_HALO = 8
VMEM_LIMIT = 56 * 1024 * 1024

PROJ_TILE = 512
MIX_TILE = 512
MOE_TILE = 2048
MOE_CHUNK = 512
MOE_TAIL_SIZES = (64, 128, 192, 256, 320, 384, 448, MOE_CHUNK)
ATTN_INTERLEAVE = (8, 8, 8)


def _rms(x, g):
    return x * lax.rsqrt(jnp.mean(x * x, axis=-1, keepdims=True) + NORM_EPS) * g


_IN_SPLITS = (0, 256, 512, 768, 1280, 1792, 2304)


QK_QUARTER = HEAD_DIM // 2


def _pair_head0(shape):
    return (lax.broadcasted_iota(i32, shape, 1) % HEAD_DIM) < QK_QUARTER


def _qk_norm_rope(y, cs, sn, gain, scale):
    head0 = _pair_head0(y.shape)
    yy = y * y
    ms0 = jnp.sum(jnp.where(head0, yy, 0.0), axis=-1, keepdims=True)
    ms1 = jnp.sum(jnp.where(head0, 0.0, yy), axis=-1, keepdims=True)
    ms = jnp.where(head0, ms0, ms1) * (1.0 / HEAD_DIM)
    yn = y * lax.rsqrt(ms + NORM_EPS) * gain
    return (yn * cs + pltpu.roll(yn, HEAD_DIM, 1) * sn) * scale


def _in_proj_kernel(x_ref, g_ref, w_ref, cos_ref, sin_ref, qg_ref, kg_ref, *refs):
    out_refs, stage_s = refs[:-1], refs[-1]
    tm = x_ref.shape[0]
    h = _rms(x_ref[...], g_ref[...]).astype(bf16)

    def proj(n):
        return jnp.dot(h, w_ref[:, _IN_SPLITS[n]:_IN_SPLITS[n + 1]], preferred_element_type=f32)

    def emit_class_major(n, y, post=None):
        for p_, c in enumerate(range(0, y.shape[1], LANES)):
            yp = y[:, c:c + LANES]
            stage_s[n - 3, p_] = yp if post is None else _qk_norm_rope(yp, cos_ref[...], sin_ref[...], *post)
            for r in range(MAX_DIL):
                out_refs[n][0, r, :, c:c + LANES] = stage_s[n - 3, p_, pl.ds(r, tm // MAX_DIL, stride=MAX_DIL), :]

    q = proj(3)
    k = proj(4)
    emit_class_major(3, q, (qg_ref[...], HEAD_DIM ** -0.5))
    v = proj(5)
    out_refs[0][...] = proj(0)
    emit_class_major(4, k, (kg_ref[...], 1.0))
    out_refs[1][...] = proj(1)
    emit_class_major(5, v)
    out_refs[2][...] = proj(2)


def _in_proj(x, g, w, cos_t, sin_t, qg, kg, batch, seq):
    t = x.shape[0]
    tm = PROJ_TILE
    nt = seq // tm
    widths = [c1 - c0 for c0, c1 in zip(_IN_SPLITS[:-1], _IN_SPLITS[1:])]
    table = pl.BlockSpec((tm, LANES), lambda i: (i % nt, 0))
    gain = pl.BlockSpec((1, LANES), lambda i: (0, 0))
    flat = lambda w_: (pl.BlockSpec((tm, w_), lambda i: (i, 0)), jax.ShapeDtypeStruct((t, w_), f32))
    cls_major = lambda w_: (pl.BlockSpec((1, MAX_DIL, tm // MAX_DIL, w_), lambda i: (i // nt, 0, i % nt, 0)),
                            jax.ShapeDtypeStruct((batch, MAX_DIL, seq // MAX_DIL, w_), f32))
    outs = [flat(w_) for w_ in widths[:3]] + [cls_major(w_) for w_ in widths[3:]]
    return pl.pallas_call(
        _in_proj_kernel,
        grid=(t // tm,),
        in_specs=[pl.BlockSpec((tm, D_MODEL), lambda i: (i, 0)),
                  pl.BlockSpec((1, D_MODEL), lambda i: (0, 0)),
                  pl.BlockSpec((D_MODEL, _IN_SPLITS[-1]), lambda i: (0, 0)),
                  table, table, gain, gain],
        out_specs=[o[0] for o in outs],
        out_shape=[o[1] for o in outs],
        scratch_shapes=[pltpu.VMEM((3, ATTN_WIDTH // LANES, tm, LANES), f32)],
        compiler_params=pltpu.CompilerParams(dimension_semantics=("arbitrary",),
                                             vmem_limit_bytes=VMEM_LIMIT),
        name="in_proj",
    )(x, g, w, cos_t, sin_t, qg, kg)


def _mixer_kernel(pool_ref, lru_ref, gate_ref, pw_ref, ps_ref, cw_ref, cb_ref, wri_ref, bri_ref, lam_ref,
                  ypool_ref, ylru_ref,
                  b0, b1, b2, b3, cbuf, a_s, b_s, h_s, a2_s, b2_s, e2_s, e3_s, hc, *, ts):
    j = pl.program_id(1)
    n1 = ts // 8
    n2 = n1 // 8

    @pl.when(j == 0)
    def _():
        for buf in (b0, b1, b2, b3):
            buf[pl.ds(0, POOL_HALO), :] = jnp.zeros((POOL_HALO, POOL_WIDTH), f32)
        cbuf[pl.ds(0, CONV_HALO), :] = jnp.zeros((CONV_HALO, LRU_WIDTH), f32)
        hc[...] = jnp.zeros_like(hc)

    row = lax.broadcasted_iota(i32, (ts, POOL_WIDTH), 0) + j * ts
    grp = lax.broadcasted_iota(i32, (ts, POOL_WIDTH), 1) // (POOL_WIDTH // len(POOL_WINDOWS))

    u = pool_ref[...]
    b0[pl.ds(POOL_HALO, ts), :] = u
    s1 = u + b0[pl.ds(POOL_HALO - 1, ts), :]
    b1[pl.ds(POOL_HALO, ts), :] = s1
    s2 = s1 + b1[pl.ds(POOL_HALO - 2, ts), :]
    b2[pl.ds(POOL_HALO, ts), :] = s2
    s3 = s2 + b2[pl.ds(POOL_HALO - 4, ts), :]
    b3[pl.ds(POOL_HALO, ts), :] = s3
    s4 = s3 + b3[pl.ds(POOL_HALO - 8, ts), :]
    for buf in (b0, b1, b2, b3):
        buf[pl.ds(0, POOL_HALO), :] = buf[pl.ds(ts, POOL_HALO), :]
    sums = jnp.where(grp == 0, s1, jnp.where(grp == 1, s2, jnp.where(grp == 2, s3, s4)))
    win = jnp.where(grp == 0, float(POOL_WINDOWS[0]),
                    jnp.where(grp == 1, float(POOL_WINDOWS[1]),
                              jnp.where(grp == 2, float(POOL_WINDOWS[2]), float(POOL_WINDOWS[3]))))
    cnt = jnp.minimum((row + 1).astype(f32), win)
    pooled = sums / cnt - u
    ypool_ref[...] = jnp.dot(pooled.astype(bf16), pw_ref[...], preferred_element_type=f32) * ps_ref[...]

    lu = lru_ref[...]
    cbuf[pl.ds(CONV_HALO, ts), :] = lu
    xc = (cb_ref[...] + cw_ref[3:4, :] * lu
          + cw_ref[2:3, :] * cbuf[pl.ds(CONV_HALO - 1, ts), :]
          + cw_ref[1:2, :] * cbuf[pl.ds(CONV_HALO - 2, ts), :]
          + cw_ref[0:1, :] * cbuf[pl.ds(CONV_HALO - 3, ts), :])
    cbuf[pl.ds(0, CONV_HALO), :] = cbuf[pl.ds(ts, CONV_HALO), :]
    ri = jnp.dot(xc.astype(bf16), wri_ref[...], preferred_element_type=f32) + bri_ref[...]
    r = jax.nn.sigmoid(ri[:, :LRU_WIDTH])
    ig = jax.nn.sigmoid(ri[:, LRU_WIDTH:])
    log_a = -LRU_C * r * jax.nn.softplus(-lam_ref[...])
    a = jnp.exp(log_a)
    th = jnp.tanh(log_a)
    mult = jnp.sqrt(-2.0 * th / (1.0 - th))
    mult = jnp.where(row == 0, 1.0, mult)
    bb = mult * (ig * xc)

    def rows(ref, hf, s, n):
        return ref[hf, pl.ds(s, n, stride=8), :]

    for hf in range(LRU_WIDTH // LANES):
        a_s[hf] = a[:, hf * LANES:(hf + 1) * LANES]
        b_s[hf] = bb[:, hf * LANES:(hf + 1) * LANES]
        pa, pb = rows(a_s, hf, 0, n1), rows(b_s, hf, 0, n1)
        for s in range(1, 8):
            as_ = rows(a_s, hf, s, n1)
            pb = as_ * pb + rows(b_s, hf, s, n1)
            pa = as_ * pa
        a2_s[hf] = pa
        b2_s[hf] = pb
        pa, pb = rows(a2_s, hf, 0, n2), rows(b2_s, hf, 0, n2)
        for s in range(1, 8):
            as_ = rows(a2_s, hf, s, n2)
            pb = as_ * pb + rows(b2_s, hf, s, n2)
            pa = as_ * pa
        h = hc[hf, 0:1, :]
        for k in range(8):
            e3_s[hf, k:k + 1, :] = h
            h = pa[k:k + 1, :] * h + pb[k:k + 1, :]
        hc[hf, 0:1, :] = h
        prev = e3_s[hf]
        for s in range(8):
            e2_s[hf, pl.ds(s, n2, stride=8), :] = prev
            prev = rows(a2_s, hf, s, n2) * prev + rows(b2_s, hf, s, n2)
        prev = e2_s[hf]
        for s in range(8):
            prev = rows(a_s, hf, s, n1) * prev + rows(b_s, hf, s, n1)
            h_s[hf, pl.ds(s, n1, stride=8), :] = prev

    hfull = jnp.concatenate([h_s[hf] for hf in range(LRU_WIDTH // LANES)], axis=1)
    ylru_ref[...] = hfull * jax.nn.gelu(gate_ref[...])


def _mixer(pool_u, lru_u, gate, pw, ps, cw, cb, wri, bri, lam, batch, seq):
    ts = MIX_TILE
    nt = seq // ts
    t = batch * seq
    nh = LRU_WIDTH // LANES
    tile = lambda w_: pl.BlockSpec((ts, w_), lambda b, j: (b * nt + j, 0))
    full = lambda shape: pl.BlockSpec(shape, lambda b, j: (0,) * len(shape))
    return pl.pallas_call(
        functools.partial(_mixer_kernel, ts=ts),
        grid=(batch, nt),
        in_specs=[tile(POOL_WIDTH), tile(LRU_WIDTH), tile(LRU_WIDTH),
                  full((POOL_WIDTH, POOL_WIDTH)), full((1, POOL_WIDTH)),
                  full((4, LRU_WIDTH)), full((1, LRU_WIDTH)),
                  full((LRU_WIDTH, 2 * LRU_WIDTH)), full((1, 2 * LRU_WIDTH)), full((1, LRU_WIDTH))],
        out_specs=[tile(POOL_WIDTH), tile(LRU_WIDTH)],
        out_shape=[jax.ShapeDtypeStruct((t, POOL_WIDTH), f32), jax.ShapeDtypeStruct((t, LRU_WIDTH), f32)],
        scratch_shapes=[pltpu.VMEM((ts + POOL_HALO, POOL_WIDTH), f32)] * 4
        + [pltpu.VMEM((ts + CONV_HALO, LRU_WIDTH), f32)]
        + [pltpu.VMEM((nh, ts, LANES), f32)] * 3
        + [pltpu.VMEM((nh, ts // 8, LANES), f32)] * 3
        + [pltpu.VMEM((nh, 8, LANES), f32)] * 2,
        compiler_params=pltpu.CompilerParams(dimension_semantics=("arbitrary", "arbitrary"),
                                             vmem_limit_bytes=VMEM_LIMIT),
        name="pool_lru_mixer",
    )(pool_u, lru_u, gate, pw, ps, cw, cb, wri, bri, lam)


def _attn_kernel(q_ref, k_ref, v_ref, o_ref, *state_refs, seq):
    blk = ATTN_BLOCK
    cls = seq // MAX_DIL
    qs, ks, vs = q_ref.at[0], k_ref.at[0], v_ref.at[0]

    h0 = lax.broadcasted_iota(i32, (blk, LANES), 1) < HEAD_DIM
    q_h0 = _pair_head0((blk, LANES))
    states = [state_refs[3 * g:3 * g + 3] for g in range(len(DILATIONS))]
    qi = lax.broadcasted_iota(i32, (2 * blk, 2 * blk), 0) % blk
    kj = lax.broadcasted_iota(i32, (2 * blk, 2 * blk), 1)

    for bi, d in enumerate(DILATIONS):
        nb = seq // (blk * d)
        runs = MAX_DIL // d
        run = blk // runs

        def member(i, runs=runs, run=run):
            return runs * (i % run) + i // run

        jq = member(qi)
        jk = member(kj % blk)
        mask_cur = (kj >= blk) & (jk <= jq)
        mask_prev = (kj < blk) & (jk >= jq)

        def starts(res, n, d=d, runs=runs, run=run):
            return [(res + d * b, pl.multiple_of(run * n, 8)) for b in range(runs)]

        def ld(ref, row_starts, run=run):
            return jnp.concatenate([ref[c, pl.ds(r0, run), :] for c, r0 in row_starts], axis=0)

        def st(ref, row_starts, val, run=run):
            for b, (c, r0) in enumerate(row_starts):
                ref[c, pl.ds(r0, run), :] = val[b * run:(b + 1) * run, :]

        def body(it, carry, nb=nb, bi=bi, starts=starts, ld=ld, st=st, mask_cur=mask_cur, mask_prev=mask_prev):
            loaded = []
            for u_ in range(ATTN_INTERLEAVE[bi]):
                idx = it * ATTN_INTERLEAVE[bi] + u_
                res = idx // nb
                n = idx % nb
                cur = starts(res, n)
                prev = starts(res, jnp.maximum(n - 1, 0))
                qb = ld(qs, cur)
                kb = jnp.concatenate([ld(ks, prev), ld(ks, cur)], axis=0).astype(bf16)
                vb = jnp.concatenate([ld(vs, prev), ld(vs, cur)], axis=0).astype(bf16)
                loaded.append((n, cur, qb, kb, vb))
            scores = []
            for n, cur, qb, kb, vb in loaded:
                q2 = jnp.concatenate([jnp.where(q_h0, qb, 0.0), jnp.where(q_h0, 0.0, qb)], axis=0).astype(bf16)
                scores.append(lax.dot_general(q2, kb, (((1,), (1,)), ((), ())), preferred_element_type=f32))
            results = []
            for (n, cur, qb, kb, vb), s in zip(loaded, scores):
                valid = mask_cur | (mask_prev & (n > 0))
                s = jnp.where(valid, s, NEG_INF)
                mb = jnp.max(s, axis=-1, keepdims=True)
                p = jnp.exp(s - mb)
                lb = jnp.sum(p, axis=-1, keepdims=True)
                c = jnp.dot(p.astype(bf16), vb, preferred_element_type=f32)
                m_blk = jnp.where(h0, mb[:blk], mb[blk:])
                l_blk = jnp.where(h0, lb[:blk], lb[blk:])
                n_blk = jnp.where(h0, c[:blk], c[blk:])
                results.append((cur, m_blk, l_blk, n_blk))
            for cur, *vals in results:
                for ref, val in zip(states[bi], vals):
                    st(ref, cur, val)
            return carry

        lax.fori_loop(0, seq // (blk * ATTN_INTERLEAVE[bi]), body, 0)

    def fin(r, carry):
        ms = [st_[0][r] for st_ in states]
        m = functools.reduce(jnp.maximum, ms)
        num = den = None
        for (m_ref, l_ref, a_ref), m_g in zip(states, ms):
            w = jnp.exp(m_g - m)
            num = w * a_ref[r] if num is None else num + w * a_ref[r]
            den = w * l_ref[r] if den is None else den + w * l_ref[r]
        o_ref[0, r] = num / den
        return carry

    lax.fori_loop(0, MAX_DIL, fin, 0)


def _attention(q, k, v, batch, seq):
    npair = ATTN_WIDTH // LANES
    cls = seq // MAX_DIL
    tile = pl.BlockSpec((1, MAX_DIL, cls, LANES), lambda b, p: (b, 0, 0, p))
    return pl.pallas_call(
        functools.partial(_attn_kernel, seq=seq),
        grid=(batch, npair),
        in_specs=[tile, tile, tile],
        out_specs=tile,
        out_shape=jax.ShapeDtypeStruct((batch, MAX_DIL, cls, ATTN_WIDTH), f32),
        scratch_shapes=[pltpu.VMEM((MAX_DIL, cls, LANES), f32)] * (3 * len(DILATIONS)),
        compiler_params=pltpu.CompilerParams(dimension_semantics=("arbitrary", "arbitrary"),
                                             vmem_limit_bytes=VMEM_LIMIT),
        name="dilated_attention",
    )(q, k, v)


def _out_proj_kernel(x_ref, yp_ref, yl_ref, ya_ref, og_ref, wo_ref, g2_ref, rw_ref, rb_ref,
                     xo_ref, idx_ref, gate_ref, rank_ref, cnt_ref, cnt_s, before_s, ya_s, *, tm, steps_per_tile):
    i = pl.program_id(0)
    c0, c1 = POOL_WIDTH, POOL_WIDTH + LRU_WIDTH
    for r in range(MAX_DIL):
        for p_ in range(ATTN_WIDTH // LANES):
            ya_s[p_, pl.ds(r, tm // MAX_DIL, stride=MAX_DIL), :] = ya_ref[0, r, :, p_ * LANES:(p_ + 1) * LANES]
    ya = jnp.concatenate([ya_s[p_] for p_ in range(ATTN_WIDTH // LANES)], axis=1)
    acc = x_ref[...]
    for y, lo, hi in ((yp_ref[...], 0, c0), (yl_ref[...], c0, c1), (ya, c1, D_MODEL)):
        mix = _rms(y, og_ref[:, lo:hi]).astype(bf16)
        acc = acc + jnp.dot(mix, wo_ref[lo:hi, :], preferred_element_type=f32)
    xo_ref[...] = acc

    h2 = _rms(acc, g2_ref[...])
    h_hi = h2.astype(bf16)
    h_lo = (h2 - h_hi.astype(f32)).astype(bf16)
    nt = (((1,), (1,)), ((), ()))
    both = lax.dot_general(rw_ref[...], h_hi, nt, preferred_element_type=f32)
    logits = (both[:N_EXPERTS] + both[N_EXPERTS:]
              + lax.dot_general(rw_ref[:N_EXPERTS, :], h_lo, nt, preferred_element_type=f32)
              + rb_ref[...])
    eio = lax.broadcasted_iota(i32, (N_EXPERTS, tm), 0)
    cur = logits
    vals, idxs = [], []
    for _ in range(TOP_K):
        mx = jnp.max(cur, axis=0, keepdims=True)
        ix = jnp.min(jnp.where(cur == mx, eio, N_EXPERTS), axis=0, keepdims=True)
        vals.append(mx)
        idxs.append(ix)
        cur = jnp.where(eio == ix, -jnp.inf, cur)
    ex = [jnp.exp(v - vals[0]) for v in vals]
    den = ex[0] + ex[1] + ex[2] + ex[3]

    @pl.when(i % steps_per_tile == 0)
    def _():
        cnt_s[...] = jnp.zeros_like(cnt_s)

    onehot = jnp.zeros((N_EXPERTS, tm), f32)
    for ix in idxs:
        onehot = onehot + (eio == ix).astype(f32)
    @pl.when(i == 0)
    def _():
        before_s[...] = (lax.broadcasted_iota(i32, (tm, tm), 0) < lax.broadcasted_iota(i32, (tm, tm), 1)).astype(bf16)

    prefix = jnp.dot(onehot.astype(bf16), before_s[...], preferred_element_type=f32) + cnt_s[:, 0:1]
    for k in range(TOP_K):
        idx_ref[k:k + 1, :] = idxs[k]
        gate_ref[k:k + 1, :] = ex[k] / den
        rank_ref[k:k + 1, :] = jnp.sum(jnp.where(eio == idxs[k], prefix, 0.0), axis=0, keepdims=True).astype(i32)
    cnt_s[...] = cnt_s[...] + jnp.sum(onehot, axis=1, keepdims=True)
    cnt_ref[0] = cnt_s[...]


def _out_proj(x, yp, yl, ya, og, wo, g2, rw_t, rb, seq):
    t = x.shape[0]
    tm = PROJ_TILE
    steps = t // tm
    nt = seq // tm
    tile = lambda w_: pl.BlockSpec((tm, w_), lambda i: (i, 0))
    full = lambda shape: pl.BlockSpec(shape, lambda i: (0,) * len(shape))
    route = pl.BlockSpec((TOP_K, tm), lambda i: (0, i))
    return pl.pallas_call(
        functools.partial(_out_proj_kernel, tm=tm, steps_per_tile=MOE_TILE // tm),
        grid=(steps,),
        in_specs=[tile(D_MODEL), tile(POOL_WIDTH), tile(LRU_WIDTH),
                  pl.BlockSpec((1, MAX_DIL, tm // MAX_DIL, ATTN_WIDTH), lambda i: (i // nt, 0, i % nt, 0)),
                  full((1, D_MODEL)), full((D_MODEL, D_MODEL)), full((1, D_MODEL)),
                  full((2 * N_EXPERTS, D_MODEL)), full((N_EXPERTS, 1))],
        out_specs=[tile(D_MODEL), route, route, route,
                   pl.BlockSpec((1, N_EXPERTS, LANES), lambda i: (i, 0, 0))],
        out_shape=[jax.ShapeDtypeStruct((t, D_MODEL), f32),
                   jax.ShapeDtypeStruct((TOP_K, t), i32),
                   jax.ShapeDtypeStruct((TOP_K, t), f32),
                   jax.ShapeDtypeStruct((TOP_K, t), i32),
                   jax.ShapeDtypeStruct((steps, N_EXPERTS, LANES), f32)],
        scratch_shapes=[pltpu.VMEM((N_EXPERTS, LANES), f32), pltpu.VMEM((tm, tm), bf16),
                        pltpu.VMEM((ATTN_WIDTH // LANES, tm, LANES), f32)],
        compiler_params=pltpu.CompilerParams(dimension_semantics=("arbitrary",),
                                             vmem_limit_bytes=VMEM_LIMIT),
        name="out_proj_router",
    )(x, yp, yl, ya, og, wo, g2, rw_t, rb)


ROW_SLAB = 8
ROUTE_GROUP = 8
MOE_EXPERTS_PER_STEP = 2


def _slab_cols(ref, first_row, n_rows, col):
    return ref[pl.ds(first_row * ROW_SLAB + col, n_rows, stride=ROW_SLAB), :]


def _moe_kernel(off_ref, pos_ref, gate_ref, x_ref, g2_ref, wgu_ref, bgu_ref, wd_ref, bd_ref,
                o_ref, sorted_s, stage_s, *, tt, ch, io_rows):
    i = pl.program_id(0)
    s = pl.program_id(1)
    n_io = tt // io_rows
    n_exp_steps = N_EXPERTS // MOE_EXPERTS_PER_STEP
    base = i * (N_EXPERTS + 1)
    n_col = D_MODEL // LANES

    def slab(ref, first_sublane):
        return ref.at[pl.ds(pl.multiple_of(first_sublane, ROW_SLAB), ROW_SLAB), :]

    @pl.when(s < n_io)
    def _dispatch():
        @pl.when(s == 0)
        def _():
            sorted_s[pl.ds(TOP_K * tt * ROW_SLAB, ch * ROW_SLAB), :] = jnp.zeros((ch * ROW_SLAB, LANES), f32)

        h = _rms(x_ref[...], g2_ref[...])
        for j in range(n_col):
            stage_s[pl.ds(j, io_rows, stride=ROW_SLAB), :] = h[:, j * LANES:(j + 1) * LANES]

        def scatter(c, carry):
            for u_ in range(ROUTE_GROUP):
                row = slab(stage_s, (c * ROUTE_GROUP + u_) * ROW_SLAB)[...]
                for k in range(TOP_K):
                    slab(sorted_s, pos_ref[c, u_ * TOP_K + k])[...] = row
            return carry

        lax.fori_loop(0, io_rows // ROUTE_GROUP, scatter, 0)

    @pl.when((s >= n_io) & (s < n_io + n_exp_steps))
    def _experts():
        def run_chunk(sub, start, hi, rows):
            xc = jnp.concatenate([_slab_cols(sorted_s, start, rows, j) for j in range(n_col)], axis=1)
            gu = jnp.dot(xc.astype(bf16), wgu_ref[sub], preferred_element_type=f32) + bgu_ref[sub]
            gate = jnp.minimum(gu[:, :EXPERT_FF], SWIGLU_LIMIT)
            up = jnp.clip(gu[:, EXPERT_FF:], -SWIGLU_LIMIT, SWIGLU_LIMIT)
            act = (up + 1.0) * gate * jax.nn.sigmoid(SWIGLU_ALPHA * gate)
            y = jnp.dot(act.astype(bf16), wd_ref[sub], preferred_element_type=f32) + bd_ref[sub]
            mine = (lax.broadcasted_iota(i32, (rows, 1), 0) + start) < hi
            y = jnp.where(mine, y, xc)
            for j in range(n_col):
                sorted_s[pl.ds(start * ROW_SLAB + j, rows, stride=ROW_SLAB), :] = y[:, j * LANES:(j + 1) * LANES]

        def expert(sub, carry):
            e = (s - n_io) * MOE_EXPERTS_PER_STEP + sub
            lo = off_ref[base + e]
            hi = off_ref[base + e + 1]
            n_full = jnp.maximum(hi - lo - 1, 0) // ch

            def full(c, carry2):
                run_chunk(sub, lo + c * ch, hi, ch)
                return carry2

            lax.fori_loop(0, n_full, full, 0)
            start = lo + n_full * ch
            rest = hi - start
            for size_lo, size in zip((0,) + MOE_TAIL_SIZES[:-1], MOE_TAIL_SIZES):
                @pl.when((rest > size_lo) & (rest <= size))
                def _(size=size):
                    run_chunk(sub, start, hi, size)
            return carry

        lax.fori_loop(0, MOE_EXPERTS_PER_STEP, expert, 0)

    @pl.when(s >= n_io + n_exp_steps)
    def _combine():
        def gather(c, carry):
            for u_ in range(ROUTE_GROUP):
                acc = gate_ref[c, u_ * TOP_K] * slab(sorted_s, pos_ref[c, u_ * TOP_K])[...]
                for k in range(1, TOP_K):
                    acc = acc + gate_ref[c, u_ * TOP_K + k] * slab(sorted_s, pos_ref[c, u_ * TOP_K + k])[...]
                slab(stage_s, (c * ROUTE_GROUP + u_) * ROW_SLAB)[...] = acc
            return carry

        lax.fori_loop(0, io_rows // ROUTE_GROUP, gather, 0)
        moe = jnp.concatenate([_slab_cols(stage_s, 0, io_rows, j) for j in range(n_col)], axis=1)
        o_ref[...] = x_ref[...] + moe


def _moe(x, pos, gates, offsets, g2, wgu, bgu, wd, bd):
    t = x.shape[0]
    tt = MOE_TILE
    ch = MOE_CHUNK
    io_rows = PROJ_TILE
    n_io = tt // io_rows
    eps = MOE_EXPERTS_PER_STEP
    n_exp_steps = N_EXPERTS // eps
    n_steps = 2 * n_io + n_exp_steps

    def tok_idx(i, s, off):
        in_combine = s >= n_io + n_exp_steps
        return i * n_io + jnp.where(s < n_io, s, jnp.where(in_combine, s - n_io - n_exp_steps, n_io - 1))

    def out_idx(i, s, off):
        return i * n_io + jnp.maximum(s - n_io - n_exp_steps, 0)

    def expert_idx(i, s, off):
        return jnp.clip(s - n_io, 0, n_exp_steps - 1)

    route = pl.BlockSpec((io_rows // ROUTE_GROUP, ROUTE_GROUP * TOP_K), lambda i, s, off: (tok_idx(i, s, off), 0),
                         memory_space=pltpu.SMEM)
    return pl.pallas_call(
        functools.partial(_moe_kernel, tt=tt, ch=ch, io_rows=io_rows),
        grid_spec=pltpu.PrefetchScalarGridSpec(
            num_scalar_prefetch=1,
            grid=(t // tt, n_steps),
            in_specs=[route, route,
                      pl.BlockSpec((io_rows, D_MODEL), lambda i, s, off: (tok_idx(i, s, off), 0)),
                      pl.BlockSpec((1, D_MODEL), lambda i, s, off: (0, 0)),
                      pl.BlockSpec((eps, D_MODEL, 2 * EXPERT_FF), lambda i, s, off: (expert_idx(i, s, off), 0, 0)),
                      pl.BlockSpec((eps, 1, 2 * EXPERT_FF), lambda i, s, off: (expert_idx(i, s, off), 0, 0)),
                      pl.BlockSpec((eps, EXPERT_FF, D_MODEL), lambda i, s, off: (expert_idx(i, s, off), 0, 0)),
                      pl.BlockSpec((eps, 1, D_MODEL), lambda i, s, off: (expert_idx(i, s, off), 0, 0))],
            out_specs=pl.BlockSpec((io_rows, D_MODEL), lambda i, s, off: (out_idx(i, s, off), 0)),
            scratch_shapes=[pltpu.VMEM(((TOP_K * tt + ch) * ROW_SLAB, LANES), f32),
                            pltpu.VMEM((io_rows * ROW_SLAB, LANES), f32)]),
        out_shape=jax.ShapeDtypeStruct((t, D_MODEL), f32),
        compiler_params=pltpu.CompilerParams(dimension_semantics=("arbitrary", "arbitrary"),
                                             vmem_limit_bytes=VMEM_LIMIT),
        name="moe_experts",
    )(offsets, pos, gates, x, g2, wgu, bgu, wd, bd)


def _pair_layout(v):
    first, second = v[..., :QK_QUARTER], v[..., QK_QUARTER:]
    return jnp.concatenate([first, first, second, second], axis=-1)


def _qk_column_order():
    pair = jnp.arange(LANES).reshape(2, 2, QK_QUARTER).transpose(1, 0, 2).reshape(-1)
    cols = jnp.arange(_IN_SPLITS[-1])
    qk = cols[_IN_SPLITS[3]:_IN_SPLITS[5]].reshape(-1, LANES)[:, pair].reshape(-1)
    return jnp.concatenate([cols[:_IN_SPLITS[3]], qk, cols[_IN_SPLITS[5]:]])


def _rope_tables(seq):
    pos = jnp.arange(seq, dtype=f32)
    inv_freq = ROPE_THETA ** (-jnp.arange(0, HEAD_DIM, 2, dtype=f32) / HEAD_DIM)
    ang = pos[:, None] * inv_freq[None, :]
    cos, sin = jnp.cos(ang), jnp.sin(ang)
    return _pair_layout(jnp.concatenate([cos, cos], axis=1)), _pair_layout(jnp.concatenate([-sin, sin], axis=1))


def _layer(x, cos_t, sin_t, batch, seq, p):
    t = batch * seq
    row = lambda v: v.reshape(1, -1)
    pool_u, lru_u, gate, q, k, v = _in_proj(x, row(p["norm1_g"]), p["w_in"][:, _qk_column_order()].astype(bf16),
                                            cos_t, sin_t, row(_pair_layout(p["q_norm_g"])),
                                            row(_pair_layout(p["k_norm_g"])), batch, seq)
    wri = jnp.concatenate([block_diag(*p["lru_wa"]), block_diag(*p["lru_wx"])], axis=1).astype(bf16)
    bri = jnp.concatenate([p["lru_ba"], p["lru_bx"]]).reshape(1, -1)
    y_pool, y_lru = _mixer(pool_u, lru_u, gate, block_diag(*p["pool_w"]).astype(bf16), row(p["pool_scale"]),
                           p["conv_w"], row(p["conv_b"]), wri, bri, row(p["lru_lambda"]), batch, seq)
    y_attn = _attention(q, k, v, batch, seq)
    rw_hi = p["router_w"].T.astype(bf16)
    rw_split = jnp.concatenate([rw_hi, (p["router_w"].T - rw_hi.astype(f32)).astype(bf16)], axis=0)
    x, idx, gates, rank, cnt = _out_proj(x, y_pool, y_lru, y_attn, row(p["out_norm_g"]), p["w_out"].astype(bf16),
                                         row(p["norm2_g"]), rw_split, p["router_b"].reshape(-1, 1), seq)
    steps_per_tile = MOE_TILE // PROJ_TILE
    n_tiles = t // MOE_TILE
    counts = cnt[steps_per_tile - 1::steps_per_tile, :, 0].astype(i32)
    offsets = jnp.concatenate([jnp.zeros((n_tiles, 1), i32), jnp.cumsum(counts, axis=1)], axis=1)
    idx_t = idx.reshape(TOP_K, n_tiles, MOE_TILE)
    seg_start = jnp.zeros_like(idx_t)
    for e in range(N_EXPERTS):
        seg_start = jnp.where(idx_t == e, offsets[None, :, e, None], seg_start)
    pos = (rank + seg_start.reshape(TOP_K, t)) * ROW_SLAB
    to_smem = lambda a: a.T.reshape(t // ROUTE_GROUP, ROUTE_GROUP * TOP_K)
    return _moe(x, to_smem(pos), to_smem(gates), offsets.reshape(-1), row(p["norm2_g"]),
                p["w_gate_up"].astype(bf16), p["b_gate_up"][:, None, :],
                p["w_down"].astype(bf16), p["b_down"][:, None, :])


def kernel(x, norm1_g, w_in, pool_w, pool_scale, conv_w, conv_b, lru_wa, lru_ba, lru_wx, lru_bx, lru_lambda,
           q_norm_g, k_norm_g, out_norm_g, w_out, norm2_g, router_w, router_b, w_gate_up, b_gate_up,
           w_down, b_down):
    batch, seq, d = x.shape
    params = dict(norm1_g=norm1_g, w_in=w_in, pool_w=pool_w, pool_scale=pool_scale, conv_w=conv_w, conv_b=conv_b,
                  lru_wa=lru_wa, lru_ba=lru_ba, lru_wx=lru_wx, lru_bx=lru_bx, lru_lambda=lru_lambda,
                  q_norm_g=q_norm_g, k_norm_g=k_norm_g, out_norm_g=out_norm_g, w_out=w_out, norm2_g=norm2_g,
                  router_w=router_w, router_b=router_b, w_gate_up=w_gate_up, b_gate_up=b_gate_up,
                  w_down=w_down, b_down=b_down)
    cos_t, sin_t = _rope_tables(seq)
    xt = x.reshape(batch * seq, d)
    for layer in range(norm1_g.shape[0]):
        xt = _layer(xt, cos_t, sin_t, batch, seq, {name: val[layer] for name, val in params.items()})
    return xt.reshape(batch, seq, d)
```

```python
import functools

import jax
import jax.numpy as jnp
from jax import lax
from jax.experimental import pallas as pl
from jax.experimental.pallas import tpu as pltpu
from jax.scipy.linalg import block_diag

f32 = jnp.float32
bf16 = jnp.bfloat16
i32 = jnp.int32

D_MODEL = 1024
POOL_WIDTH = 256
POOL_WINDOWS = (2, 4, 8, 16)
LRU_WIDTH = 256
LRU_C = 8.0
ATTN_WIDTH = 512
HEAD_DIM = 64
DILATIONS = (1, 4, 16)
MAX_DIL = DILATIONS[-1]
ATTN_BLOCK = 128
ROPE_THETA = 10000.0
N_EXPERTS = 32
TOP_K = 4
EXPERT_FF = 256
SWIGLU_LIMIT = 7.0
SWIGLU_ALPHA = 1.702
NORM_EPS = 1e-6
NEG_INF = -1e30

LANES = 128
POOL_HALO = 16
CONV_HALO = 8
VMEM_LIMIT = 56 * 1024 * 1024

PROJ_TILE = 512
MIX_TILE = 512
MOE_TILE = 2048
MOE_CHUNK = 512
MOE_TAIL_SIZES = (64, 128, 192, 256, 320, 384, 448, MOE_CHUNK)
ATTN_SCORES_AHEAD = 2


def _rms(x, g):
    return x * lax.rsqrt(jnp.mean(x * x, axis=-1, keepdims=True) + NORM_EPS) * g


_IN_SPLITS = (0, 256, 512, 768, 1280, 1792, 2304)


QK_QUARTER = HEAD_DIM // 2


def _pair_head0(shape):
    return (lax.broadcasted_iota(i32, shape, 1) % HEAD_DIM) < QK_QUARTER


def _qk_norm_rope(y, cs, sn, gain, scale):
    head0 = _pair_head0(y.shape)
    yy = y * y
    ms0 = jnp.sum(jnp.where(head0, yy, 0.0), axis=-1, keepdims=True)
    ms1 = jnp.sum(jnp.where(head0, 0.0, yy), axis=-1, keepdims=True)
    ms = jnp.where(head0, ms0, ms1) * (1.0 / HEAD_DIM)
    yn = y * lax.rsqrt(ms + NORM_EPS) * gain
    return (yn * cs + pltpu.roll(yn, HEAD_DIM, 1) * sn) * scale


def _in_proj_kernel(x_ref, g_ref, w_ref, cos_ref, sin_ref, qg_ref, kg_ref, *refs):
    out_refs, stage_s = refs[:-1], refs[-1]
    tm = x_ref.shape[0]
    h = _rms(x_ref[...], g_ref[...]).astype(bf16)

    def proj(n):
        return jnp.dot(h, w_ref[:, _IN_SPLITS[n]:_IN_SPLITS[n + 1]], preferred_element_type=f32)

    def emit_class_major(n, y, post=None):
        for p_, c in enumerate(range(0, y.shape[1], LANES)):
            yp = y[:, c:c + LANES]
            stage_s[n - 3, p_] = yp if post is None else _qk_norm_rope(yp, cos_ref[...], sin_ref[...], *post)
            for r in range(MAX_DIL):
                out_refs[n][0, r, :, c:c + LANES] = stage_s[n - 3, p_, pl.ds(r, tm // MAX_DIL, stride=MAX_DIL), :]

    q = proj(3)
    k = proj(4)
    emit_class_major(3, q, (qg_ref[...], HEAD_DIM ** -0.5))
    v = proj(5)
    out_refs[0][...] = proj(0)
    emit_class_major(4, k, (kg_ref[...], 1.0))
    out_refs[1][...] = proj(1)
    emit_class_major(5, v)
    out_refs[2][...] = proj(2)


def _in_proj(x, g, w, cos_t, sin_t, qg, kg, batch, seq):
    t = x.shape[0]
    tm = PROJ_TILE
    nt = seq // tm
    widths = [c1 - c0 for c0, c1 in zip(_IN_SPLITS[:-1], _IN_SPLITS[1:])]
    table = pl.BlockSpec((tm, LANES), lambda i: (i % nt, 0))
    gain = pl.BlockSpec((1, LANES), lambda i: (0, 0))
    flat = lambda w_: (pl.BlockSpec((tm, w_), lambda i: (i, 0)), jax.ShapeDtypeStruct((t, w_), f32))
    cls_major = lambda w_: (pl.BlockSpec((1, MAX_DIL, tm // MAX_DIL, w_), lambda i: (i // nt, 0, i % nt, 0)),
                            jax.ShapeDtypeStruct((batch, MAX_DIL, seq // MAX_DIL, w_), f32))
    outs = [flat(w_) for w_ in widths[:3]] + [cls_major(w_) for w_ in widths[3:]]
    return pl.pallas_call(
        _in_proj_kernel,
        grid=(t // tm,),
        in_specs=[pl.BlockSpec((tm, D_MODEL), lambda i: (i, 0)),
                  pl.BlockSpec((1, D_MODEL), lambda i: (0, 0)),
                  pl.BlockSpec((D_MODEL, _IN_SPLITS[-1]), lambda i: (0, 0)),
                  table, table, gain, gain],
        out_specs=[o[0] for o in outs],
        out_shape=[o[1] for o in outs],
        scratch_shapes=[pltpu.VMEM((3, ATTN_WIDTH // LANES, tm, LANES), f32)],
        compiler_params=pltpu.CompilerParams(dimension_semantics=("arbitrary",),
                                             vmem_limit_bytes=VMEM_LIMIT),
        name="in_proj",
    )(x, g, w, cos_t, sin_t, qg, kg)


def _mixer_kernel(pool_ref, lru_ref, gate_ref, pw_ref, ps_ref, cw_ref, cb_ref, wri_ref, bri_ref, lam_ref,
                  ypool_ref, ylru_ref,
                  b0, b1, b2, b3, cbuf, a_s, b_s, h_s, a2_s, b2_s, e2_s, e3_s, hc, *, ts):
    j = pl.program_id(1)
    n1 = ts // 8
    n2 = n1 // 8

    @pl.when(j == 0)
    def _():
        for buf in (b0, b1, b2, b3):
            buf[pl.ds(0, POOL_HALO), :] = jnp.zeros((POOL_HALO, POOL_WIDTH), f32)
        cbuf[pl.ds(0, CONV_HALO), :] = jnp.zeros((CONV_HALO, LRU_WIDTH), f32)
        hc[...] = jnp.zeros_like(hc)

    row = lax.broadcasted_iota(i32, (ts, POOL_WIDTH), 0) + j * ts
    grp = lax.broadcasted_iota(i32, (ts, POOL_WIDTH), 1) // (POOL_WIDTH // len(POOL_WINDOWS))

    u = pool_ref[...]
    b0[pl.ds(POOL_HALO, ts), :] = u
    s1 = u + b0[pl.ds(POOL_HALO - 1, ts), :]
    b1[pl.ds(POOL_HALO, ts), :] = s1
    s2 = s1 + b1[pl.ds(POOL_HALO - 2, ts), :]
    b2[pl.ds(POOL_HALO, ts), :] = s2
    s3 = s2 + b2[pl.ds(POOL_HALO - 4, ts), :]
    b3[pl.ds(POOL_HALO, ts), :] = s3
    s4 = s3 + b3[pl.ds(POOL_HALO - 8, ts), :]
    for buf in (b0, b1, b2, b3):
        buf[pl.ds(0, POOL_HALO), :] = buf[pl.ds(ts, POOL_HALO), :]
    sums = jnp.where(grp == 0, s1, jnp.where(grp == 1, s2, jnp.where(grp == 2, s3, s4)))
    win = jnp.where(grp == 0, float(POOL_WINDOWS[0]),
                    jnp.where(grp == 1, float(POOL_WINDOWS[1]),
                              jnp.where(grp == 2, float(POOL_WINDOWS[2]), float(POOL_WINDOWS[3]))))
    cnt = jnp.minimum((row + 1).astype(f32), win)
    pooled = sums / cnt - u
    ypool_ref[...] = jnp.dot(pooled.astype(bf16), pw_ref[...], preferred_element_type=f32) * ps_ref[...]

    lu = lru_ref[...]
    cbuf[pl.ds(CONV_HALO, ts), :] = lu
    xc = (cb_ref[...] + cw_ref[3:4, :] * lu
          + cw_ref[2:3, :] * cbuf[pl.ds(CONV_HALO - 1, ts), :]
          + cw_ref[1:2, :] * cbuf[pl.ds(CONV_HALO - 2, ts), :]
          + cw_ref[0:1, :] * cbuf[pl.ds(CONV_HALO - 3, ts), :])
    cbuf[pl.ds(0, CONV_HALO), :] = cbuf[pl.ds(ts, CONV_HALO), :]
    ri = jnp.dot(xc.astype(bf16), wri_ref[...], preferred_element_type=f32) + bri_ref[...]
    r = jax.nn.sigmoid(ri[:, :LRU_WIDTH])
    ig = jax.nn.sigmoid(ri[:, LRU_WIDTH:])
    log_a = -LRU_C * r * jax.nn.softplus(-lam_ref[...])
    a = jnp.exp(log_a)
    th = jnp.tanh(log_a)
    mult = jnp.sqrt(-2.0 * th / (1.0 - th))
    mult = jnp.where(row == 0, 1.0, mult)
    bb = mult * (ig * xc)

    def rows(ref, hf, s, n):
        return ref[hf, pl.ds(s, n, stride=8), :]

    for hf in range(LRU_WIDTH // LANES):
        a_s[hf] = a[:, hf * LANES:(hf + 1) * LANES]
        b_s[hf] = bb[:, hf * LANES:(hf + 1) * LANES]
        pa, pb = rows(a_s, hf, 0, n1), rows(b_s, hf, 0, n1)
        for s in range(1, 8):
            as_ = rows(a_s, hf, s, n1)
            pb = as_ * pb + rows(b_s, hf, s, n1)
            pa = as_ * pa
        a2_s[hf] = pa
        b2_s[hf] = pb
        pa, pb = rows(a2_s, hf, 0, n2), rows(b2_s, hf, 0, n2)
        for s in range(1, 8):
            as_ = rows(a2_s, hf, s, n2)
            pb = as_ * pb + rows(b2_s, hf, s, n2)
            pa = as_ * pa
        h = hc[hf, 0:1, :]
        for k in range(8):
            e3_s[hf, k:k + 1, :] = h
            h = pa[k:k + 1, :] * h + pb[k:k + 1, :]
        hc[hf, 0:1, :] = h
        prev = e3_s[hf]
        for s in range(8):
            e2_s[hf, pl.ds(s, n2, stride=8), :] = prev
            prev = rows(a2_s, hf, s, n2) * prev + rows(b2_s, hf, s, n2)
        prev = e2_s[hf]
        for s in range(8):
            prev = rows(a_s, hf, s, n1) * prev + rows(b_s, hf, s, n1)
            h_s[hf, pl.ds(s, n1, stride=8), :] = prev

    hfull = jnp.concatenate([h_s[hf] for hf in range(LRU_WIDTH // LANES)], axis=1)
    ylru_ref[...] = hfull * jax.nn.gelu(gate_ref[...])


def _mixer(pool_u, lru_u, gate, pw, ps, cw, cb, wri, bri, lam, batch, seq):
    ts = MIX_TILE
    nt = seq // ts
    t = batch * seq
    nh = LRU_WIDTH // LANES
    tile = lambda w_: pl.BlockSpec((ts, w_), lambda b, j: (b * nt + j, 0))
    full = lambda shape: pl.BlockSpec(shape, lambda b, j: (0,) * len(shape))
    return pl.pallas_call(
        functools.partial(_mixer_kernel, ts=ts),
        grid=(batch, nt),
        in_specs=[tile(POOL_WIDTH), tile(LRU_WIDTH), tile(LRU_WIDTH),
                  full((POOL_WIDTH, POOL_WIDTH)), full((1, POOL_WIDTH)),
                  full((4, LRU_WIDTH)), full((1, LRU_WIDTH)),
                  full((LRU_WIDTH, 2 * LRU_WIDTH)), full((1, 2 * LRU_WIDTH)), full((1, LRU_WIDTH))],
        out_specs=[tile(POOL_WIDTH), tile(LRU_WIDTH)],
        out_shape=[jax.ShapeDtypeStruct((t, POOL_WIDTH), f32), jax.ShapeDtypeStruct((t, LRU_WIDTH), f32)],
        scratch_shapes=[pltpu.VMEM((ts + POOL_HALO, POOL_WIDTH), f32)] * 4
        + [pltpu.VMEM((ts + CONV_HALO, LRU_WIDTH), f32)]
        + [pltpu.VMEM((nh, ts, LANES), f32)] * 3
        + [pltpu.VMEM((nh, ts // 8, LANES), f32)] * 3
        + [pltpu.VMEM((nh, 8, LANES), f32)] * 2,
        compiler_params=pltpu.CompilerParams(dimension_semantics=("arbitrary", "arbitrary"),
                                             vmem_limit_bytes=VMEM_LIMIT),
        name="pool_lru_mixer",
    )(pool_u, lru_u, gate, pw, ps, cw, cb, wri, bri, lam)


def _attn_kernel(q_ref, k_ref, v_ref, o_ref, *state_refs, seq):
    blk = ATTN_BLOCK
    cls = seq // MAX_DIL
    qs, ks, vs = q_ref.at[0], k_ref.at[0], v_ref.at[0]

    h0 = lax.broadcasted_iota(i32, (blk, LANES), 1) < HEAD_DIM
    q_h0 = _pair_head0((blk, LANES))
    states = [state_refs[3 * g:3 * g + 3] for g in range(len(DILATIONS))]
    qi = lax.broadcasted_iota(i32, (2 * blk, 2 * blk), 0) % blk
    kj = lax.broadcasted_iota(i32, (2 * blk, 2 * blk), 1)

    for bi, d in enumerate(DILATIONS):
        nb = seq // (blk * d)
        runs = MAX_DIL // d
        run = blk // runs

        def member(i, runs=runs, run=run):
            return runs * (i % run) + i // run

        jq = member(qi)
        jk = member(kj % blk)
        valid_first = member(lax.broadcasted_iota(i32, (2 * blk, blk), 1)) <= member(lax.broadcasted_iota(i32, (2 * blk, blk), 0) % blk)
        valid_later = ((kj >= blk) & (jk <= jq)) | ((kj < blk) & (jk >= jq))

        def rows(res, n, d=d, runs=runs, run=run):
            return [(res + d * b, run * n) for b in range(runs)]

        def ld(ref, where, run=run):
            return jnp.concatenate([ref[c, r0:r0 + run, :] for c, r0 in where], axis=0)

        def score(idx, nb=nb, rows=rows, ld=ld):
            res, n = divmod(idx, nb)
            cur = rows(res, n)
            keys = cur if n == 0 else rows(res, n - 1) + cur
            qb = ld(qs, cur)
            q2 = jnp.concatenate([jnp.where(q_h0, qb, 0.0), jnp.where(q_h0, 0.0, qb)], axis=0).astype(bf16)
            return lax.dot_general(q2, ld(ks, keys).astype(bf16), (((1,), (1,)), ((), ())),
                                   preferred_element_type=f32)

        def finish(idx, s, nb=nb, rows=rows, ld=ld, run=run, bi=bi, valid_first=valid_first, valid_later=valid_later):
            res, n = divmod(idx, nb)
            cur = rows(res, n)
            keys = cur if n == 0 else rows(res, n - 1) + cur
            s = jnp.where(valid_first if n == 0 else valid_later, s, NEG_INF)
            mb = jnp.max(s, axis=-1, keepdims=True)
            p = jnp.exp(s - mb).astype(bf16)
            vb = ld(vs, keys).astype(bf16)
            c = jnp.dot(p, jnp.concatenate([vb, jnp.ones_like(vb)], axis=1), preferred_element_type=f32)
            vals = (jnp.where(h0, mb[:blk], mb[blk:]), jnp.where(h0, c[:blk, LANES:], c[blk:, LANES:]),
                    jnp.where(h0, c[:blk, :LANES], c[blk:, :LANES]))
            for ref, val in zip(states[bi], vals):
                for b_, (cls_i, r0) in enumerate(cur):
                    ref[cls_i, r0:r0 + run, :] = val[b_ * run:(b_ + 1) * run, :]

        n_blocks = seq // blk
        scores = [score(idx) for idx in range(ATTN_SCORES_AHEAD)]
        for idx in range(n_blocks):
            if idx + ATTN_SCORES_AHEAD < n_blocks:
                scores.append(score(idx + ATTN_SCORES_AHEAD))
            finish(idx, scores[idx])
            scores[idx] = None

    def fin(r, carry):
        ms = [st_[0][r] for st_ in states]
        m = functools.reduce(jnp.maximum, ms)
        num = den = None
        for (m_ref, l_ref, a_ref), m_g in zip(states, ms):
            w = jnp.exp(m_g - m)
            num = w * a_ref[r] if num is None else num + w * a_ref[r]
            den = w * l_ref[r] if den is None else den + w * l_ref[r]
        o_ref[0, r] = num / den
        return carry

    lax.fori_loop(0, MAX_DIL, fin, 0)


def _attention(q, k, v, batch, seq):
    npair = ATTN_WIDTH // LANES
    cls = seq // MAX_DIL
    tile = pl.BlockSpec((1, MAX_DIL, cls, LANES), lambda b, p: (b, 0, 0, p))
    return pl.pallas_call(
        functools.partial(_attn_kernel, seq=seq),
        grid=(batch, npair),
        in_specs=[tile, tile, tile],
        out_specs=tile,
        out_shape=jax.ShapeDtypeStruct((batch, MAX_DIL, cls, ATTN_WIDTH), f32),
        scratch_shapes=[pltpu.VMEM((MAX_DIL, cls, LANES), f32)] * (3 * len(DILATIONS)),
        compiler_params=pltpu.CompilerParams(dimension_semantics=("arbitrary", "arbitrary"),
                                             vmem_limit_bytes=VMEM_LIMIT),
        name="dilated_attention",
    )(q, k, v)


def _out_proj_kernel(x_ref, yp_ref, yl_ref, ya_ref, og_ref, wo_ref, g2_ref, rw_ref, rb_ref,
                     xo_ref, idx_ref, gate_ref, rank_ref, cnt_ref, cnt_s, before_s, ya_s, *, tm, steps_per_tile):
    i = pl.program_id(0)
    c0, c1 = POOL_WIDTH, POOL_WIDTH + LRU_WIDTH
    for r in range(MAX_DIL):
        for p_ in range(ATTN_WIDTH // LANES):
            ya_s[p_, pl.ds(r, tm // MAX_DIL, stride=MAX_DIL), :] = ya_ref[0, r, :, p_ * LANES:(p_ + 1) * LANES]
    ya = jnp.concatenate([ya_s[p_] for p_ in range(ATTN_WIDTH // LANES)], axis=1)
    acc = x_ref[...]
    for y, lo, hi in ((yp_ref[...], 0, c0), (yl_ref[...], c0, c1), (ya, c1, D_MODEL)):
        mix = _rms(y, og_ref[:, lo:hi]).astype(bf16)
        acc = acc + jnp.dot(mix, wo_ref[lo:hi, :], preferred_element_type=f32)
    xo_ref[...] = acc

    h2 = _rms(acc, g2_ref[...])
    h_hi = h2.astype(bf16)
    h_lo = (h2 - h_hi.astype(f32)).astype(bf16)
    nt = (((1,), (1,)), ((), ()))
    both = lax.dot_general(rw_ref[...], h_hi, nt, preferred_element_type=f32)
    logits = (both[:N_EXPERTS] + both[N_EXPERTS:]
              + lax.dot_general(rw_ref[:N_EXPERTS, :], h_lo, nt, preferred_element_type=f32)
              + rb_ref[...])
    eio = lax.broadcasted_iota(i32, (N_EXPERTS, tm), 0)
    cur = logits
    vals, idxs = [], []
    for _ in range(TOP_K):
        mx = jnp.max(cur, axis=0, keepdims=True)
        ix = jnp.min(jnp.where(cur == mx, eio, N_EXPERTS), axis=0, keepdims=True)
        vals.append(mx)
        idxs.append(ix)
        cur = jnp.where(eio == ix, -jnp.inf, cur)
    ex = [jnp.exp(v - vals[0]) for v in vals]
    den = ex[0] + ex[1] + ex[2] + ex[3]

    @pl.when(i % steps_per_tile == 0)
    def _():
        cnt_s[...] = jnp.zeros_like(cnt_s)

    onehot = jnp.zeros((N_EXPERTS, tm), f32)
    for ix in idxs:
        onehot = onehot + (eio == ix).astype(f32)
    @pl.when(i == 0)
    def _():
        before_s[...] = (lax.broadcasted_iota(i32, (tm, tm), 0) < lax.broadcasted_iota(i32, (tm, tm), 1)).astype(bf16)

    prefix = jnp.dot(onehot.astype(bf16), before_s[...], preferred_element_type=f32) + cnt_s[:, 0:1]
    for k in range(TOP_K):
        idx_ref[k:k + 1, :] = idxs[k]
        gate_ref[k:k + 1, :] = ex[k] / den
        rank_ref[k:k + 1, :] = jnp.sum(jnp.where(eio == idxs[k], prefix, 0.0), axis=0, keepdims=True).astype(i32)
    cnt_s[...] = cnt_s[...] + jnp.sum(onehot, axis=1, keepdims=True)
    cnt_ref[0] = cnt_s[...]


def _out_proj(x, yp, yl, ya, og, wo, g2, rw_t, rb, seq):
    t = x.shape[0]
    tm = PROJ_TILE
    steps = t // tm
    nt = seq // tm
    tile = lambda w_: pl.BlockSpec((tm, w_), lambda i: (i, 0))
    full = lambda shape: pl.BlockSpec(shape, lambda i: (0,) * len(shape))
    route = pl.BlockSpec((TOP_K, tm), lambda i: (0, i))
    return pl.pallas_call(
        functools.partial(_out_proj_kernel, tm=tm, steps_per_tile=MOE_TILE // tm),
        grid=(steps,),
        in_specs=[tile(D_MODEL), tile(POOL_WIDTH), tile(LRU_WIDTH),
                  pl.BlockSpec((1, MAX_DIL, tm // MAX_DIL, ATTN_WIDTH), lambda i: (i // nt, 0, i % nt, 0)),
                  full((1, D_MODEL)), full((D_MODEL, D_MODEL)), full((1, D_MODEL)),
                  full((2 * N_EXPERTS, D_MODEL)), full((N_EXPERTS, 1))],
        out_specs=[tile(D_MODEL), route, route, route,
                   pl.BlockSpec((1, N_EXPERTS, LANES), lambda i: (i, 0, 0))],
        out_shape=[jax.ShapeDtypeStruct((t, D_MODEL), f32),
                   jax.ShapeDtypeStruct((TOP_K, t), i32),
                   jax.ShapeDtypeStruct((TOP_K, t), f32),
                   jax.ShapeDtypeStruct((TOP_K, t), i32),
                   jax.ShapeDtypeStruct((steps, N_EXPERTS, LANES), f32)],
        scratch_shapes=[pltpu.VMEM((N_EXPERTS, LANES), f32), pltpu.VMEM((tm, tm), bf16),
                        pltpu.VMEM((ATTN_WIDTH // LANES, tm, LANES), f32)],
        compiler_params=pltpu.CompilerParams(dimension_semantics=("arbitrary",),
                                             vmem_limit_bytes=VMEM_LIMIT),
        name="out_proj_router",
    )(x, yp, yl, ya, og, wo, g2, rw_t, rb)


ROW_SLAB = 8
ROUTE_GROUP = 8
MOE_EXPERTS_PER_STEP = 2


def _slab_cols(ref, first_row, n_rows, col):
    return ref[pl.ds(first_row * ROW_SLAB + col, n_rows, stride=ROW_SLAB), :]


def _moe_kernel(off_ref, pos_ref, gate_ref, x_ref, g2_ref, wgu_ref, bgu_ref, wd_ref, bd_ref,
                o_ref, sorted_s, stage_s, *, tt, ch, io_rows):
    i = pl.program_id(0)
    s = pl.program_id(1)
    n_io = tt // io_rows
    n_exp_steps = N_EXPERTS // MOE_EXPERTS_PER_STEP
    base = i * (N_EXPERTS + 1)
    n_col = D_MODEL // LANES

    def slab(ref, first_sublane):
        return ref.at[pl.ds(pl.multiple_of(first_sublane, ROW_SLAB), ROW_SLAB), :]

    @pl.when(s < n_io)
    def _dispatch():
        @pl.when(s == 0)
        def _():
            sorted_s[pl.ds(TOP_K * tt * ROW_SLAB, ch * ROW_SLAB), :] = jnp.zeros((ch * ROW_SLAB, LANES), f32)

        h = _rms(x_ref[...], g2_ref[...])
        for j in range(n_col):
            stage_s[pl.ds(j, io_rows, stride=ROW_SLAB), :] = h[:, j * LANES:(j + 1) * LANES]

        def scatter(c, carry):
            for u_ in range(ROUTE_GROUP):
                row = slab(stage_s, (c * ROUTE_GROUP + u_) * ROW_SLAB)[...]
                for k in range(TOP_K):
                    slab(sorted_s, pos_ref[c, u_ * TOP_K + k])[...] = row
            return carry

        lax.fori_loop(0, io_rows // ROUTE_GROUP, scatter, 0)

    @pl.when((s >= n_io) & (s < n_io + n_exp_steps))
    def _experts():
        def run_chunk(sub, start, hi, rows):
            xc = jnp.concatenate([_slab_cols(sorted_s, start, rows, j) for j in range(n_col)], axis=1)
            gu = jnp.dot(xc.astype(bf16), wgu_ref[sub], preferred_element_type=f32) + bgu_ref[sub]
            gate = jnp.minimum(gu[:, :EXPERT_FF], SWIGLU_LIMIT)
            up = jnp.clip(gu[:, EXPERT_FF:], -SWIGLU_LIMIT, SWIGLU_LIMIT)
            act = (up + 1.0) * gate * jax.nn.sigmoid(SWIGLU_ALPHA * gate)
            y = jnp.dot(act.astype(bf16), wd_ref[sub], preferred_element_type=f32) + bd_ref[sub]
            mine = (lax.broadcasted_iota(i32, (rows, 1), 0) + start) < hi
            y = jnp.where(mine, y, xc)
            for j in range(n_col):
                sorted_s[pl.ds(start * ROW_SLAB + j, rows, stride=ROW_SLAB), :] = y[:, j * LANES:(j + 1) * LANES]

        def expert(sub, carry):
            e = (s - n_io) * MOE_EXPERTS_PER_STEP + sub
            lo = off_ref[base + e]
            hi = off_ref[base + e + 1]
            n_full = jnp.maximum(hi - lo - 1, 0) // ch

            def full(c, carry2):
                run_chunk(sub, lo + c * ch, hi, ch)
                return carry2

            lax.fori_loop(0, n_full, full, 0)
            start = lo + n_full * ch
            rest = hi - start
            for size_lo, size in zip((0,) + MOE_TAIL_SIZES[:-1], MOE_TAIL_SIZES):
                @pl.when((rest > size_lo) & (rest <= size))
                def _(size=size):
                    run_chunk(sub, start, hi, size)
            return carry

        lax.fori_loop(0, MOE_EXPERTS_PER_STEP, expert, 0)

    @pl.when(s >= n_io + n_exp_steps)
    def _combine():
        def gather(c, carry):
            for u_ in range(ROUTE_GROUP):
                acc = gate_ref[c, u_ * TOP_K] * slab(sorted_s, pos_ref[c, u_ * TOP_K])[...]
                for k in range(1, TOP_K):
                    acc = acc + gate_ref[c, u_ * TOP_K + k] * slab(sorted_s, pos_ref[c, u_ * TOP_K + k])[...]
                slab(stage_s, (c * ROUTE_GROUP + u_) * ROW_SLAB)[...] = acc
            return carry

        lax.fori_loop(0, io_rows // ROUTE_GROUP, gather, 0)
        moe = jnp.concatenate([_slab_cols(stage_s, 0, io_rows, j) for j in range(n_col)], axis=1)
        o_ref[...] = x_ref[...] + moe


def _moe(x, pos, gates, offsets, g2, wgu, bgu, wd, bd):
    t = x.shape[0]
    tt = MOE_TILE
    ch = MOE_CHUNK
    io_rows = PROJ_TILE
    n_io = tt // io_rows
    eps = MOE_EXPERTS_PER_STEP
    n_exp_steps = N_EXPERTS // eps
    n_steps = 2 * n_io + n_exp_steps

    def tok_idx(i, s, off):
        in_combine = s >= n_io + n_exp_steps
        return i * n_io + jnp.where(s < n_io, s, jnp.where(in_combine, s - n_io - n_exp_steps, n_io - 1))

    def out_idx(i, s, off):
        return i * n_io + jnp.maximum(s - n_io - n_exp_steps, 0)

    def expert_idx(i, s, off):
        return jnp.clip(s - n_io, 0, n_exp_steps - 1)

    route = pl.BlockSpec((io_rows // ROUTE_GROUP, ROUTE_GROUP * TOP_K), lambda i, s, off: (tok_idx(i, s, off), 0),
                         memory_space=pltpu.SMEM)
    return pl.pallas_call(
        functools.partial(_moe_kernel, tt=tt, ch=ch, io_rows=io_rows),
        grid_spec=pltpu.PrefetchScalarGridSpec(
            num_scalar_prefetch=1,
            grid=(t // tt, n_steps),
            in_specs=[route, route,
                      pl.BlockSpec((io_rows, D_MODEL), lambda i, s, off: (tok_idx(i, s, off), 0)),
                      pl.BlockSpec((1, D_MODEL), lambda i, s, off: (0, 0)),
                      pl.BlockSpec((eps, D_MODEL, 2 * EXPERT_FF), lambda i, s, off: (expert_idx(i, s, off), 0, 0)),
                      pl.BlockSpec((eps, 1, 2 * EXPERT_FF), lambda i, s, off: (expert_idx(i, s, off), 0, 0)),
                      pl.BlockSpec((eps, EXPERT_FF, D_MODEL), lambda i, s, off: (expert_idx(i, s, off), 0, 0)),
                      pl.BlockSpec((eps, 1, D_MODEL), lambda i, s, off: (expert_idx(i, s, off), 0, 0))],
            out_specs=pl.BlockSpec((io_rows, D_MODEL), lambda i, s, off: (out_idx(i, s, off), 0)),
            scratch_shapes=[pltpu.VMEM(((TOP_K * tt + ch) * ROW_SLAB, LANES), f32),
                            pltpu.VMEM((io_rows * ROW_SLAB, LANES), f32)]),
        out_shape=jax.ShapeDtypeStruct((t, D_MODEL), f32),
        compiler_params=pltpu.CompilerParams(dimension_semantics=("arbitrary", "arbitrary"),
                                             vmem_limit_bytes=VMEM_LIMIT),
        name="moe_experts",
    )(offsets, pos, gates, x, g2, wgu, bgu, wd, bd)


def _pair_layout(v):
    first, second = v[..., :QK_QUARTER], v[..., QK_QUARTER:]
    return jnp.concatenate([first, first, second, second], axis=-1)


def _qk_column_order():
    pair = jnp.arange(LANES).reshape(2, 2, QK_QUARTER).transpose(1, 0, 2).reshape(-1)
    cols = jnp.arange(_IN_SPLITS[-1])
    qk = cols[_IN_SPLITS[3]:_IN_SPLITS[5]].reshape(-1, LANES)[:, pair].reshape(-1)
    return jnp.concatenate([cols[:_IN_SPLITS[3]], qk, cols[_IN_SPLITS[5]:]])


def _rope_tables(seq):
    pos = jnp.arange(seq, dtype=f32)
    inv_freq = ROPE_THETA ** (-jnp.arange(0, HEAD_DIM, 2, dtype=f32) / HEAD_DIM)
    ang = pos[:, None] * inv_freq[None, :]
    cos, sin = jnp.cos(ang), jnp.sin(ang)
    return _pair_layout(jnp.concatenate([cos, cos], axis=1)), _pair_layout(jnp.concatenate([-sin, sin], axis=1))


def _layer(x, cos_t, sin_t, batch, seq, p):
    t = batch * seq
    row = lambda v: v.reshape(1, -1)
    pool_u, lru_u, gate, q, k, v = _in_proj(x, row(p["norm1_g"]), p["w_in"][:, _qk_column_order()].astype(bf16),
                                            cos_t, sin_t, row(_pair_layout(p["q_norm_g"])),
                                            row(_pair_layout(p["k_norm_g"])), batch, seq)
    wri = jnp.concatenate([block_diag(*p["lru_wa"]), block_diag(*p["lru_wx"])], axis=1).astype(bf16)
    bri = jnp.concatenate([p["lru_ba"], p["lru_bx"]]).reshape(1, -1)
    y_pool, y_lru = _mixer(pool_u, lru_u, gate, block_diag(*p["pool_w"]).astype(bf16), row(p["pool_scale"]),
                           p["conv_w"], row(p["conv_b"]), wri, bri, row(p["lru_lambda"]), batch, seq)
    y_attn = _attention(q, k, v, batch, seq)
    rw_hi = p["router_w"].T.astype(bf16)
    rw_split = jnp.concatenate([rw_hi, (p["router_w"].T - rw_hi.astype(f32)).astype(bf16)], axis=0)
    x, idx, gates, rank, cnt = _out_proj(x, y_pool, y_lru, y_attn, row(p["out_norm_g"]), p["w_out"].astype(bf16),
                                         row(p["norm2_g"]), rw_split, p["router_b"].reshape(-1, 1), seq)
    steps_per_tile = MOE_TILE // PROJ_TILE
    n_tiles = t // MOE_TILE
    counts = cnt[steps_per_tile - 1::steps_per_tile, :, 0].astype(i32)
    offsets = jnp.concatenate([jnp.zeros((n_tiles, 1), i32), jnp.cumsum(counts, axis=1)], axis=1)
    idx_t = idx.reshape(TOP_K, n_tiles, MOE_TILE)
    seg_start = jnp.zeros_like(idx_t)
    for e in range(N_EXPERTS):
        seg_start = jnp.where(idx_t == e, offsets[None, :, e, None], seg_start)
    pos = (rank + seg_start.reshape(TOP_K, t)) * ROW_SLAB
    to_smem = lambda a: a.T.reshape(t // ROUTE_GROUP, ROUTE_GROUP * TOP_K)
    return _moe(x, to_smem(pos), to_smem(gates), offsets.reshape(-1), row(p["norm2_g"]),
                p["w_gate_up"].astype(bf16), p["b_gate_up"][:, None, :],
                p["w_down"].astype(bf16), p["b_down"][:, None, :])


def kernel(x, norm1_g, w_in, pool_w, pool_scale, conv_w, conv_b, lru_wa, lru_ba, lru_wx, lru_bx, lru_lambda,
           q_norm_g, k_norm_g, out_norm_g, w_out, norm2_g, router_w, router_b, w_gate_up, b_gate_up,
           w_down, b_down):
    batch, seq, d = x.shape
    params = dict(norm1_g=norm1_g, w_in=w_in, pool_w=pool_w, pool_scale=pool_scale, conv_w=conv_w, conv_b=conv_b,
                  lru_wa=lru_wa, lru_ba=lru_ba, lru_wx=lru_wx, lru_bx=lru_bx, lru_lambda=lru_lambda,
                  q_norm_g=q_norm_g, k_norm_g=k_norm_g, out_norm_g=out_norm_g, w_out=w_out, norm2_g=norm2_g,
                  router_w=router_w, router_b=router_b, w_gate_up=w_gate_up, b_gate_up=b_gate_up,
                  w_down=w_down, b_down=b_down)
    cos_t, sin_t = _rope_tables(seq)
    xt = x.reshape(batch * seq, d)
    for layer in range(norm1_g.shape[0]):
        xt = _layer(xt, cos_t, sin_t, batch, seq, {name: val[layer] for name, val in params.items()})
    return xt.reshape(batch, seq, d)
```

```python
import functools

import jax
import jax.numpy as jnp
from jax import lax
from jax.experimental import pallas as pl
from jax.experimental.pallas import tpu as pltpu
from jax.scipy.linalg import block_diag

f32 = jnp.float32
bf16 = jnp.bfloat16
i32 = jnp.int32

D_MODEL = 1024
POOL_WIDTH = 256
POOL_WINDOWS = (2, 4, 8, 16)
LRU_WIDTH = 256
LRU_C = 8.0
ATTN_WIDTH = 512
HEAD_DIM = 64
DILATIONS = (1, 4, 16)
MAX_DIL = DILATIONS[-1]
ATTN_BLOCK = 128
ROPE_THETA = 10000.0
N_EXPERTS = 32
TOP_K = 4
EXPERT_FF = 256
SWIGLU_LIMIT = 7.0
SWIGLU_ALPHA = 1.702
NORM_EPS = 1e-6
NEG_INF = -1e30

LANES = 128
POOL_HALO = 16
CONV_HALO = 8
VMEM_LIMIT = 56 * 1024 * 1024

PROJ_TILE = 512
MIX_TILE = 512
MOE_TILE = 2048
MOE_CHUNK = 512
MOE_TAIL_SIZES = (64, 128, 192, 256, 320, 384, 448, MOE_CHUNK)
ATTN_SCORES_AHEAD = 2


def _rms(x, g):
    return x * lax.rsqrt(jnp.mean(x * x, axis=-1, keepdims=True) + NORM_EPS) * g


_IN_SPLITS = (0, 256, 512, 768, 1280, 1792, 2304)


QK_QUARTER = HEAD_DIM // 2


def _pair_head0(shape):
    return (lax.broadcasted_iota(i32, shape, 1) % HEAD_DIM) < QK_QUARTER


def _qk_norm_rope(y, cs, sn, gain, scale):
    head0 = _pair_head0(y.shape)
    yy = y * y
    ms0 = jnp.sum(jnp.where(head0, yy, 0.0), axis=-1, keepdims=True)
    ms1 = jnp.sum(jnp.where(head0, 0.0, yy), axis=-1, keepdims=True)
    ms = jnp.where(head0, ms0, ms1) * (1.0 / HEAD_DIM)
    yn = y * lax.rsqrt(ms + NORM_EPS) * gain
    return (yn * cs + pltpu.roll(yn, HEAD_DIM, 1) * sn) * scale


def _in_proj_kernel(x_ref, g_ref, w_ref, cos_ref, sin_ref, qg_ref, kg_ref, *refs):
    out_refs, stage_s = refs[:-1], refs[-1]
    tm = x_ref.shape[0]
    h = _rms(x_ref[...], g_ref[...]).astype(bf16)

    def proj(n):
        return jnp.dot(h, w_ref[:, _IN_SPLITS[n]:_IN_SPLITS[n + 1]], preferred_element_type=f32)

    def emit_class_major(n, y, post=None):
        for p_, c in enumerate(range(0, y.shape[1], LANES)):
            yp = y[:, c:c + LANES]
            stage_s[n - 3, p_] = yp if post is None else _qk_norm_rope(yp, cos_ref[...], sin_ref[...], *post)
            for r in range(MAX_DIL):
                out_refs[n][0, r, :, c:c + LANES] = stage_s[n - 3, p_, pl.ds(r, tm // MAX_DIL, stride=MAX_DIL), :]

    q = proj(3)
    k = proj(4)
    emit_class_major(3, q, (qg_ref[...], HEAD_DIM ** -0.5))
    v = proj(5)
    out_refs[0][...] = proj(0)
    emit_class_major(4, k, (kg_ref[...], 1.0))
    out_refs[1][...] = proj(1)
    emit_class_major(5, v)
    out_refs[2][...] = proj(2)


def _in_proj(x, g, w, cos_t, sin_t, qg, kg, batch, seq):
    t = x.shape[0]
    tm = PROJ_TILE
    nt = seq // tm
    widths = [c1 - c0 for c0, c1 in zip(_IN_SPLITS[:-1], _IN_SPLITS[1:])]
    table = pl.BlockSpec((tm, LANES), lambda i: (i % nt, 0))
    gain = pl.BlockSpec((1, LANES), lambda i: (0, 0))
    flat = lambda w_: (pl.BlockSpec((tm, w_), lambda i: (i, 0)), jax.ShapeDtypeStruct((t, w_), f32))
    cls_major = lambda w_: (pl.BlockSpec((1, MAX_DIL, tm // MAX_DIL, w_), lambda i: (i // nt, 0, i % nt, 0)),
                            jax.ShapeDtypeStruct((batch, MAX_DIL, seq // MAX_DIL, w_), f32))
    outs = [flat(w_) for w_ in widths[:3]] + [cls_major(w_) for w_ in widths[3:]]
    return pl.pallas_call(
        _in_proj_kernel,
        grid=(t // tm,),
        in_specs=[pl.BlockSpec((tm, D_MODEL), lambda i: (i, 0)),
                  pl.BlockSpec((1, D_MODEL), lambda i: (0, 0)),
                  pl.BlockSpec((D_MODEL, _IN_SPLITS[-1]), lambda i: (0, 0)),
                  table, table, gain, gain],
        out_specs=[o[0] for o in outs],
        out_shape=[o[1] for o in outs],
        scratch_shapes=[pltpu.VMEM((3, ATTN_WIDTH // LANES, tm, LANES), f32)],
        compiler_params=pltpu.CompilerParams(dimension_semantics=("arbitrary",),
                                             vmem_limit_bytes=VMEM_LIMIT),
        name="in_proj",
    )(x, g, w, cos_t, sin_t, qg, kg)


def _mixer_kernel(pool_ref, lru_ref, gate_ref, pw_ref, ps_ref, cw_ref, cb_ref, wri_ref, bri_ref, lam_ref,
                  ypool_ref, ylru_ref,
                  b0, b1, b2, b3, cbuf, a_s, b_s, h_s, a2_s, b2_s, e2_s, e3_s, hc, *, ts):
    j = pl.program_id(1)
    n1 = ts // 8
    n2 = n1 // 8

    @pl.when(j == 0)
    def _():
        for buf in (b0, b1, b2, b3):
            buf[pl.ds(0, POOL_HALO), :] = jnp.zeros((POOL_HALO, POOL_WIDTH), f32)
        cbuf[pl.ds(0, CONV_HALO), :] = jnp.zeros((CONV_HALO, LRU_WIDTH), f32)
        hc[...] = jnp.zeros_like(hc)

    row = lax.broadcasted_iota(i32, (ts, POOL_WIDTH), 0) + j * ts
    grp = lax.broadcasted_iota(i32, (ts, POOL_WIDTH), 1) // (POOL_WIDTH // len(POOL_WINDOWS))

    u = pool_ref[...]
    b0[pl.ds(POOL_HALO, ts), :] = u
    s1 = u + b0[pl.ds(POOL_HALO - 1, ts), :]
    b1[pl.ds(POOL_HALO, ts), :] = s1
    s2 = s1 + b1[pl.ds(POOL_HALO - 2, ts), :]
    b2[pl.ds(POOL_HALO, ts), :] = s2
    s3 = s2 + b2[pl.ds(POOL_HALO - 4, ts), :]
    b3[pl.ds(POOL_HALO, ts), :] = s3
    s4 = s3 + b3[pl.ds(POOL_HALO - 8, ts), :]
    for buf in (b0, b1, b2, b3):
        buf[pl.ds(0, POOL_HALO), :] = buf[pl.ds(ts, POOL_HALO), :]
    sums = jnp.where(grp == 0, s1, jnp.where(grp == 1, s2, jnp.where(grp == 2, s3, s4)))
    win = jnp.where(grp == 0, float(POOL_WINDOWS[0]),
                    jnp.where(grp == 1, float(POOL_WINDOWS[1]),
                              jnp.where(grp == 2, float(POOL_WINDOWS[2]), float(POOL_WINDOWS[3]))))
    cnt = jnp.minimum((row + 1).astype(f32), win)
    pooled = sums / cnt - u
    ypool_ref[...] = jnp.dot(pooled.astype(bf16), pw_ref[...], preferred_element_type=f32) * ps_ref[...]

    lu = lru_ref[...]
    cbuf[pl.ds(CONV_HALO, ts), :] = lu
    xc = (cb_ref[...] + cw_ref[3:4, :] * lu
          + cw_ref[2:3, :] * cbuf[pl.ds(CONV_HALO - 1, ts), :]
          + cw_ref[1:2, :] * cbuf[pl.ds(CONV_HALO - 2, ts), :]
          + cw_ref[0:1, :] * cbuf[pl.ds(CONV_HALO - 3, ts), :])
    cbuf[pl.ds(0, CONV_HALO), :] = cbuf[pl.ds(ts, CONV_HALO), :]
    ri = jnp.dot(xc.astype(bf16), wri_ref[...], preferred_element_type=f32) + bri_ref[...]
    r = jax.nn.sigmoid(ri[:, :LRU_WIDTH])
    ig = jax.nn.sigmoid(ri[:, LRU_WIDTH:])
    log_a = -LRU_C * r * jax.nn.softplus(-lam_ref[...])
    a = jnp.exp(log_a)
    th = jnp.tanh(log_a)
    mult = jnp.sqrt(-2.0 * th / (1.0 - th))
    mult = jnp.where(row == 0, 1.0, mult)
    bb = mult * (ig * xc)

    def rows(ref, hf, s, n):
        return ref[hf, pl.ds(s, n, stride=8), :]

    for hf in range(LRU_WIDTH // LANES):
        a_s[hf] = a[:, hf * LANES:(hf + 1) * LANES]
        b_s[hf] = bb[:, hf * LANES:(hf + 1) * LANES]
        pa, pb = rows(a_s, hf, 0, n1), rows(b_s, hf, 0, n1)
        for s in range(1, 8):
            as_ = rows(a_s, hf, s, n1)
            pb = as_ * pb + rows(b_s, hf, s, n1)
            pa = as_ * pa
        a2_s[hf] = pa
        b2_s[hf] = pb
        pa, pb = rows(a2_s, hf, 0, n2), rows(b2_s, hf, 0, n2)
        for s in range(1, 8):
            as_ = rows(a2_s, hf, s, n2)
            pb = as_ * pb + rows(b2_s, hf, s, n2)
            pa = as_ * pa
        h = hc[hf, 0:1, :]
        for k in range(8):
            e3_s[hf, k:k + 1, :] = h
            h = pa[k:k + 1, :] * h + pb[k:k + 1, :]
        hc[hf, 0:1, :] = h
        prev = e3_s[hf]
        for s in range(8):
            e2_s[hf, pl.ds(s, n2, stride=8), :] = prev
            prev = rows(a2_s, hf, s, n2) * prev + rows(b2_s, hf, s, n2)
        prev = e2_s[hf]
        for s in range(8):
            prev = rows(a_s, hf, s, n1) * prev + rows(b_s, hf, s, n1)
            h_s[hf, pl.ds(s, n1, stride=8), :] = prev

    hfull = jnp.concatenate([h_s[hf] for hf in range(LRU_WIDTH // LANES)], axis=1)
    ylru_ref[...] = hfull * jax.nn.gelu(gate_ref[...])


def _mixer(pool_u, lru_u, gate, pw, ps, cw, cb, wri, bri, lam, batch, seq):
    ts = MIX_TILE
    nt = seq // ts
    t = batch * seq
    nh = LRU_WIDTH // LANES
    tile = lambda w_: pl.BlockSpec((ts, w_), lambda b, j: (b * nt + j, 0))
    full = lambda shape: pl.BlockSpec(shape, lambda b, j: (0,) * len(shape))
    return pl.pallas_call(
        functools.partial(_mixer_kernel, ts=ts),
        grid=(batch, nt),
        in_specs=[tile(POOL_WIDTH), tile(LRU_WIDTH), tile(LRU_WIDTH),
                  full((POOL_WIDTH, POOL_WIDTH)), full((1, POOL_WIDTH)),
                  full((4, LRU_WIDTH)), full((1, LRU_WIDTH)),
                  full((LRU_WIDTH, 2 * LRU_WIDTH)), full((1, 2 * LRU_WIDTH)), full((1, LRU_WIDTH))],
        out_specs=[tile(POOL_WIDTH), tile(LRU_WIDTH)],
        out_shape=[jax.ShapeDtypeStruct((t, POOL_WIDTH), f32), jax.ShapeDtypeStruct((t, LRU_WIDTH), f32)],
        scratch_shapes=[pltpu.VMEM((ts + POOL_HALO, POOL_WIDTH), f32)] * 4
        + [pltpu.VMEM((ts + CONV_HALO, LRU_WIDTH), f32)]
        + [pltpu.VMEM((nh, ts, LANES), f32)] * 3
        + [pltpu.VMEM((nh, ts // 8, LANES), f32)] * 3
        + [pltpu.VMEM((nh, 8, LANES), f32)] * 2,
        compiler_params=pltpu.CompilerParams(dimension_semantics=("arbitrary", "arbitrary"),
                                             vmem_limit_bytes=VMEM_LIMIT),
        name="pool_lru_mixer",
    )(pool_u, lru_u, gate, pw, ps, cw, cb, wri, bri, lam)


def _attn_kernel(q_ref, k_ref, v_ref, o_ref, *state_refs, seq):
    blk = ATTN_BLOCK
    cls = seq // MAX_DIL
    qs, ks, vs = q_ref.at[0], k_ref.at[0], v_ref.at[0]

    h0 = lax.broadcasted_iota(i32, (blk, LANES), 1) < HEAD_DIM
    q_h0 = _pair_head0((blk, LANES))
    states = [state_refs[3 * g:3 * g + 3] for g in range(len(DILATIONS))]
    qi = lax.broadcasted_iota(i32, (2 * blk, 2 * blk), 0) % blk
    kj = lax.broadcasted_iota(i32, (2 * blk, 2 * blk), 1)

    for bi, d in enumerate(DILATIONS):
        nb = seq // (blk * d)
        runs = MAX_DIL // d
        run = blk // runs

        def member(i, runs=runs, run=run):
            return runs * (i % run) + i // run

        jq = member(qi)
        jk = member(kj % blk)
        valid_first = member(lax.broadcasted_iota(i32, (2 * blk, blk), 1)) <= member(lax.broadcasted_iota(i32, (2 * blk, blk), 0) % blk)
        valid_later = ((kj >= blk) & (jk <= jq)) | ((kj < blk) & (jk >= jq))

        def rows(res, n, d=d, runs=runs, run=run):
            return [(res + d * b, run * n) for b in range(runs)]

        def ld(ref, where, run=run):
            return jnp.concatenate([ref[c, r0:r0 + run, :] for c, r0 in where], axis=0)

        def score(idx, nb=nb, rows=rows, ld=ld):
            res, n = divmod(idx, nb)
            cur = rows(res, n)
            keys = cur if n == 0 else rows(res, n - 1) + cur
            qb = ld(qs, cur)
            q2 = jnp.concatenate([jnp.where(q_h0, qb, 0.0), jnp.where(q_h0, 0.0, qb)], axis=0).astype(bf16)
            return lax.dot_general(q2, ld(ks, keys).astype(bf16), (((1,), (1,)), ((), ())),
                                   preferred_element_type=f32)

        def finish(idx, s, nb=nb, rows=rows, ld=ld, run=run, bi=bi, valid_first=valid_first, valid_later=valid_later):
            res, n = divmod(idx, nb)
            cur = rows(res, n)
            keys = cur if n == 0 else rows(res, n - 1) + cur
            s = jnp.where(valid_first if n == 0 else valid_later, s, NEG_INF)
            mb = jnp.max(s, axis=-1, keepdims=True)
            p = jnp.exp(s - mb).astype(bf16)
            vb = ld(vs, keys).astype(bf16)
            c = jnp.dot(p, jnp.concatenate([vb, jnp.ones_like(vb)], axis=1), preferred_element_type=f32)
            vals = (jnp.where(h0, mb[:blk], mb[blk:]), jnp.where(h0, c[:blk, LANES:], c[blk:, LANES:]),
                    jnp.where(h0, c[:blk, :LANES], c[blk:, :LANES]))
            for ref, val in zip(states[bi], vals):
                for b_, (cls_i, r0) in enumerate(cur):
                    ref[cls_i, r0:r0 + run, :] = val[b_ * run:(b_ + 1) * run, :]

        n_blocks = seq // blk
        scores = [score(idx) for idx in range(ATTN_SCORES_AHEAD)]
        for idx in range(n_blocks):
            if idx + ATTN_SCORES_AHEAD < n_blocks:
                scores.append(score(idx + ATTN_SCORES_AHEAD))
            finish(idx, scores[idx])
            scores[idx] = None

    def fin(r, carry):
        ms = [st_[0][r] for st_ in states]
        m = functools.reduce(jnp.maximum, ms)
        num = den = None
        for (m_ref, l_ref, a_ref), m_g in zip(states, ms):
            w = jnp.exp(m_g - m)
            num = w * a_ref[r] if num is None else num + w * a_ref[r]
            den = w * l_ref[r] if den is None else den + w * l_ref[r]
        o_ref[0, r] = num / den
        return carry

    lax.fori_loop(0, MAX_DIL, fin, 0)


def _attention(q, k, v, batch, seq):
    npair = ATTN_WIDTH // LANES
    cls = seq // MAX_DIL
    tile = pl.BlockSpec((1, MAX_DIL, cls, LANES), lambda b, p: (b, 0, 0, p))
    return pl.pallas_call(
        functools.partial(_attn_kernel, seq=seq),
        grid=(batch, npair),
        in_specs=[tile, tile, tile],
        out_specs=tile,
        out_shape=jax.ShapeDtypeStruct((batch, MAX_DIL, cls, ATTN_WIDTH), f32),
        scratch_shapes=[pltpu.VMEM((MAX_DIL, cls, LANES), f32)] * (3 * len(DILATIONS)),
        compiler_params=pltpu.CompilerParams(dimension_semantics=("arbitrary", "arbitrary"),
                                             vmem_limit_bytes=VMEM_LIMIT),
        name="dilated_attention",
    )(q, k, v)


def _out_proj_kernel(x_ref, yp_ref, yl_ref, ya_ref, og_ref, wo_ref, g2_ref, rw_ref, rb_ref,
                     xo_ref, idx_ref, gate_ref, rank_ref, cnt_ref, cnt_s, before_s, ya_s, *, tm, steps_per_tile):
    i = pl.program_id(0)
    c0, c1 = POOL_WIDTH, POOL_WIDTH + LRU_WIDTH
    for r in range(MAX_DIL):
        for p_ in range(ATTN_WIDTH // LANES):
            ya_s[p_, pl.ds(r, tm // MAX_DIL, stride=MAX_DIL), :] = ya_ref[0, r, :, p_ * LANES:(p_ + 1) * LANES]
    ya = jnp.concatenate([ya_s[p_] for p_ in range(ATTN_WIDTH // LANES)], axis=1)
    acc = x_ref[...]
    for y, lo, hi in ((yp_ref[...], 0, c0), (yl_ref[...], c0, c1), (ya, c1, D_MODEL)):
        mix = _rms(y, og_ref[:, lo:hi]).astype(bf16)
        acc = acc + jnp.dot(mix, wo_ref[lo:hi, :], preferred_element_type=f32)
    xo_ref[...] = acc

    h2 = _rms(acc, g2_ref[...])
    h_hi = h2.astype(bf16)
    h_lo = (h2 - h_hi.astype(f32)).astype(bf16)
    nt = (((1,), (1,)), ((), ()))
    both = lax.dot_general(rw_ref[...], h_hi, nt, preferred_element_type=f32)
    logits = (both[:N_EXPERTS] + both[N_EXPERTS:]
              + lax.dot_general(rw_ref[:N_EXPERTS, :], h_lo, nt, preferred_element_type=f32)
              + rb_ref[...])
    eio = lax.broadcasted_iota(i32, (N_EXPERTS, tm), 0)
    cur = logits
    vals, idxs = [], []
    for _ in range(TOP_K):
        mx = jnp.max(cur, axis=0, keepdims=True)
        ix = jnp.min(jnp.where(cur == mx, eio, N_EXPERTS), axis=0, keepdims=True)
        vals.append(mx)
        idxs.append(ix)
        cur = jnp.where(eio == ix, -jnp.inf, cur)
    ex = [jnp.exp(v - vals[0]) for v in vals]
    den = ex[0] + ex[1] + ex[2] + ex[3]

    @pl.when(i % steps_per_tile == 0)
    def _():
        cnt_s[...] = jnp.zeros_like(cnt_s)

    onehot = jnp.zeros((N_EXPERTS, tm), f32)
    for ix in idxs:
        onehot = onehot + (eio == ix).astype(f32)
    @pl.when(i == 0)
    def _():
        before_s[...] = (lax.broadcasted_iota(i32, (tm, tm), 0) < lax.broadcasted_iota(i32, (tm, tm), 1)).astype(bf16)

    prefix = jnp.dot(onehot.astype(bf16), before_s[...], preferred_element_type=f32) + cnt_s[:, 0:1]
    for k in range(TOP_K):
        idx_ref[k:k + 1, :] = idxs[k]
        gate_ref[k:k + 1, :] = ex[k] / den
        rank_ref[k:k + 1, :] = jnp.sum(jnp.where(eio == idxs[k], prefix, 0.0), axis=0, keepdims=True).astype(i32)
    cnt_s[...] = cnt_s[...] + jnp.sum(onehot, axis=1, keepdims=True)
    cnt_ref[0] = cnt_s[...]


def _out_proj(x, yp, yl, ya, og, wo, g2, rw_t, rb, seq):
    t = x.shape[0]
    tm = PROJ_TILE
    steps = t // tm
    nt = seq // tm
    tile = lambda w_: pl.BlockSpec((tm, w_), lambda i: (i, 0))
    full = lambda shape: pl.BlockSpec(shape, lambda i: (0,) * len(shape))
    route = pl.BlockSpec((TOP_K, tm), lambda i: (0, i))
    return pl.pallas_call(
        functools.partial(_out_proj_kernel, tm=tm, steps_per_tile=MOE_TILE // tm),
        grid=(steps,),
        in_specs=[tile(D_MODEL), tile(POOL_WIDTH), tile(LRU_WIDTH),
                  pl.BlockSpec((1, MAX_DIL, tm // MAX_DIL, ATTN_WIDTH), lambda i: (i // nt, 0, i % nt, 0)),
                  full((1, D_MODEL)), full((D_MODEL, D_MODEL)), full((1, D_MODEL)),
                  full((2 * N_EXPERTS, D_MODEL)), full((N_EXPERTS, 1))],
        out_specs=[tile(D_MODEL), route, route, route,
                   pl.BlockSpec((1, N_EXPERTS, LANES), lambda i: (i, 0, 0))],
        out_shape=[jax.ShapeDtypeStruct((t, D_MODEL), f32),
                   jax.ShapeDtypeStruct((TOP_K, t), i32),
                   jax.ShapeDtypeStruct((TOP_K, t), f32),
                   jax.ShapeDtypeStruct((TOP_K, t), i32),
                   jax.ShapeDtypeStruct((steps, N_EXPERTS, LANES), f32)],
        scratch_shapes=[pltpu.VMEM((N_EXPERTS, LANES), f32), pltpu.VMEM((tm, tm), bf16),
                        pltpu.VMEM((ATTN_WIDTH // LANES, tm, LANES), f32)],
        compiler_params=pltpu.CompilerParams(dimension_semantics=("arbitrary",),
                                             vmem_limit_bytes=VMEM_LIMIT),
        name="out_proj_router",
    )(x, yp, yl, ya, og, wo, g2, rw_t, rb)


ROW_SLAB = 8
ROUTE_GROUP = 8
MOE_EXPERTS_PER_STEP = 2


def _slab_cols(ref, first_row, n_rows, col):
    return ref[pl.ds(first_row * ROW_SLAB + col, n_rows, stride=ROW_SLAB), :]


def _moe_kernel(off_ref, pos_ref, gate_ref, x_ref, g2_ref, wgu_ref, bgu_ref, wd_ref, bd_ref,
                o_ref, sorted_s, stage_s, *, tt, ch, io_rows):
    i = pl.program_id(0)
    s = pl.program_id(1)
    n_io = tt // io_rows
    n_exp_steps = N_EXPERTS // MOE_EXPERTS_PER_STEP
    base = i * (N_EXPERTS + 1)
    n_col = D_MODEL // LANES

    def slab(ref, first_sublane):
        return ref.at[pl.ds(pl.multiple_of(first_sublane, ROW_SLAB), ROW_SLAB), :]

    @pl.when(s < n_io)
    def _dispatch():
        @pl.when(s == 0)
        def _():
            sorted_s[pl.ds(TOP_K * tt * ROW_SLAB, ch * ROW_SLAB), :] = jnp.zeros((ch * ROW_SLAB, LANES), f32)

        h = _rms(x_ref[...], g2_ref[...])
        for j in range(n_col):
            stage_s[pl.ds(j, io_rows, stride=ROW_SLAB), :] = h[:, j * LANES:(j + 1) * LANES]

        for c in range(io_rows // ROUTE_GROUP):
            for u_ in range(ROUTE_GROUP):
                row = stage_s[(c * ROUTE_GROUP + u_) * ROW_SLAB:(c * ROUTE_GROUP + u_ + 1) * ROW_SLAB, :]
                for k in range(TOP_K):
                    slab(sorted_s, pos_ref[c, u_ * TOP_K + k])[...] = row

    @pl.when((s >= n_io) & (s < n_io + n_exp_steps))
    def _experts():
        def run_chunk(sub, start, hi, rows):
            xc = jnp.concatenate([_slab_cols(sorted_s, start, rows, j) for j in range(n_col)], axis=1)
            gu = jnp.dot(xc.astype(bf16), wgu_ref[sub], preferred_element_type=f32) + bgu_ref[sub]
            gate = jnp.minimum(gu[:, :EXPERT_FF], SWIGLU_LIMIT)
            up = jnp.clip(gu[:, EXPERT_FF:], -SWIGLU_LIMIT, SWIGLU_LIMIT)
            act = (up + 1.0) * gate * jax.nn.sigmoid(SWIGLU_ALPHA * gate)
            y = jnp.dot(act.astype(bf16), wd_ref[sub], preferred_element_type=f32) + bd_ref[sub]
            mine = (lax.broadcasted_iota(i32, (rows, 1), 0) + start) < hi
            y = jnp.where(mine, y, xc)
            for j in range(n_col):
                sorted_s[pl.ds(start * ROW_SLAB + j, rows, stride=ROW_SLAB), :] = y[:, j * LANES:(j + 1) * LANES]

        def expert(sub, carry):
            e = (s - n_io) * MOE_EXPERTS_PER_STEP + sub
            lo = off_ref[base + e]
            hi = off_ref[base + e + 1]
            n_full = jnp.maximum(hi - lo - 1, 0) // ch

            def full(c, carry2):
                run_chunk(sub, lo + c * ch, hi, ch)
                return carry2

            lax.fori_loop(0, n_full, full, 0)
            start = lo + n_full * ch
            rest = hi - start
            for size_lo, size in zip((0,) + MOE_TAIL_SIZES[:-1], MOE_TAIL_SIZES):
                @pl.when((rest > size_lo) & (rest <= size))
                def _(size=size):
                    run_chunk(sub, start, hi, size)
            return carry

        lax.fori_loop(0, MOE_EXPERTS_PER_STEP, expert, 0)

    @pl.when(s >= n_io + n_exp_steps)
    def _combine():
        for c in range(io_rows // ROUTE_GROUP):
            for u_ in range(ROUTE_GROUP):
                acc = gate_ref[c, u_ * TOP_K] * slab(sorted_s, pos_ref[c, u_ * TOP_K])[...]
                for k in range(1, TOP_K):
                    acc = acc + gate_ref[c, u_ * TOP_K + k] * slab(sorted_s, pos_ref[c, u_ * TOP_K + k])[...]
                stage_s[(c * ROUTE_GROUP + u_) * ROW_SLAB:(c * ROUTE_GROUP + u_ + 1) * ROW_SLAB, :] = acc
        moe = jnp.concatenate([_slab_cols(stage_s, 0, io_rows, j) for j in range(n_col)], axis=1)
        o_ref[...] = x_ref[...] + moe


def _moe(x, pos, gates, offsets, g2, wgu, bgu, wd, bd):
    t = x.shape[0]
    tt = MOE_TILE
    ch = MOE_CHUNK
    io_rows = PROJ_TILE
    n_io = tt // io_rows
    eps = MOE_EXPERTS_PER_STEP
    n_exp_steps = N_EXPERTS // eps
    n_steps = 2 * n_io + n_exp_steps

    def tok_idx(i, s, off):
        in_combine = s >= n_io + n_exp_steps
        return i * n_io + jnp.where(s < n_io, s, jnp.where(in_combine, s - n_io - n_exp_steps, n_io - 1))

    def out_idx(i, s, off):
        return i * n_io + jnp.maximum(s - n_io - n_exp_steps, 0)

    def expert_idx(i, s, off):
        return jnp.clip(s - n_io, 0, n_exp_steps - 1)

    route = pl.BlockSpec((io_rows // ROUTE_GROUP, ROUTE_GROUP * TOP_K), lambda i, s, off: (tok_idx(i, s, off), 0),
                         memory_space=pltpu.SMEM)
    return pl.pallas_call(
        functools.partial(_moe_kernel, tt=tt, ch=ch, io_rows=io_rows),
        grid_spec=pltpu.PrefetchScalarGridSpec(
            num_scalar_prefetch=1,
            grid=(t // tt, n_steps),
            in_specs=[route, route,
                      pl.BlockSpec((io_rows, D_MODEL), lambda i, s, off: (tok_idx(i, s, off), 0)),
                      pl.BlockSpec((1, D_MODEL), lambda i, s, off: (0, 0)),
                      pl.BlockSpec((eps, D_MODEL, 2 * EXPERT_FF), lambda i, s, off: (expert_idx(i, s, off), 0, 0)),
                      pl.BlockSpec((eps, 1, 2 * EXPERT_FF), lambda i, s, off: (expert_idx(i, s, off), 0, 0)),
                      pl.BlockSpec((eps, EXPERT_FF, D_MODEL), lambda i, s, off: (expert_idx(i, s, off), 0, 0)),
                      pl.BlockSpec((eps, 1, D_MODEL), lambda i, s, off: (expert_idx(i, s, off), 0, 0))],
            out_specs=pl.BlockSpec((io_rows, D_MODEL), lambda i, s, off: (out_idx(i, s, off), 0)),
            scratch_shapes=[pltpu.VMEM(((TOP_K * tt + ch) * ROW_SLAB, LANES), f32),
                            pltpu.VMEM((io_rows * ROW_SLAB, LANES), f32)]),
        out_shape=jax.ShapeDtypeStruct((t, D_MODEL), f32),
        compiler_params=pltpu.CompilerParams(dimension_semantics=("arbitrary", "arbitrary"),
                                             vmem_limit_bytes=VMEM_LIMIT),
        name="moe_experts",
    )(offsets, pos, gates, x, g2, wgu, bgu, wd, bd)


def _pair_layout(v):
    first, second = v[..., :QK_QUARTER], v[..., QK_QUARTER:]
    return jnp.concatenate([first, first, second, second], axis=-1)


def _qk_column_order():
    pair = jnp.arange(LANES).reshape(2, 2, QK_QUARTER).transpose(1, 0, 2).reshape(-1)
    cols = jnp.arange(_IN_SPLITS[-1])
    qk = cols[_IN_SPLITS[3]:_IN_SPLITS[5]].reshape(-1, LANES)[:, pair].reshape(-1)
    return jnp.concatenate([cols[:_IN_SPLITS[3]], qk, cols[_IN_SPLITS[5]:]])


def _rope_tables(seq):
    pos = jnp.arange(seq, dtype=f32)
    inv_freq = ROPE_THETA ** (-jnp.arange(0, HEAD_DIM, 2, dtype=f32) / HEAD_DIM)
    ang = pos[:, None] * inv_freq[None, :]
    cos, sin = jnp.cos(ang), jnp.sin(ang)
    return _pair_layout(jnp.concatenate([cos, cos], axis=1)), _pair_layout(jnp.concatenate([-sin, sin], axis=1))


def _layer(x, cos_t, sin_t, batch, seq, p):
    t = batch * seq
    row = lambda v: v.reshape(1, -1)
    pool_u, lru_u, gate, q, k, v = _in_proj(x, row(p["norm1_g"]), p["w_in"][:, _qk_column_order()].astype(bf16),
                                            cos_t, sin_t, row(_pair_layout(p["q_norm_g"])),
                                            row(_pair_layout(p["k_norm_g"])), batch, seq)
    wri = jnp.concatenate([block_diag(*p["lru_wa"]), block_diag(*p["lru_wx"])], axis=1).astype(bf16)
    bri = jnp.concatenate([p["lru_ba"], p["lru_bx"]]).reshape(1, -1)
    y_pool, y_lru = _mixer(pool_u, lru_u, gate, block_diag(*p["pool_w"]).astype(bf16), row(p["pool_scale"]),
                           p["conv_w"], row(p["conv_b"]), wri, bri, row(p["lru_lambda"]), batch, seq)
    y_attn = _attention(q, k, v, batch, seq)
    rw_hi = p["router_w"].T.astype(bf16)
    rw_split = jnp.concatenate([rw_hi, (p["router_w"].T - rw_hi.astype(f32)).astype(bf16)], axis=0)
    x, idx, gates, rank, cnt = _out_proj(x, y_pool, y_lru, y_attn, row(p["out_norm_g"]), p["w_out"].astype(bf16),
                                         row(p["norm2_g"]), rw_split, p["router_b"].reshape(-1, 1), seq)
    steps_per_tile = MOE_TILE // PROJ_TILE
    n_tiles = t // MOE_TILE
    counts = cnt[steps_per_tile - 1::steps_per_tile, :, 0].astype(i32)
    offsets = jnp.concatenate([jnp.zeros((n_tiles, 1), i32), jnp.cumsum(counts, axis=1)], axis=1)
    idx_t = idx.reshape(TOP_K, n_tiles, MOE_TILE)
    seg_start = jnp.zeros_like(idx_t)
    for e in range(N_EXPERTS):
        seg_start = jnp.where(idx_t == e, offsets[None, :, e, None], seg_start)
    pos = (rank + seg_start.reshape(TOP_K, t)) * ROW_SLAB
    to_smem = lambda a: a.T.reshape(t // ROUTE_GROUP, ROUTE_GROUP * TOP_K)
    return _moe(x, to_smem(pos), to_smem(gates), offsets.reshape(-1), row(p["norm2_g"]),
                p["w_gate_up"].astype(bf16), p["b_gate_up"][:, None, :],
                p["w_down"].astype(bf16), p["b_down"][:, None, :])


def kernel(x, norm1_g, w_in, pool_w, pool_scale, conv_w, conv_b, lru_wa, lru_ba, lru_wx, lru_bx, lru_lambda,
           q_norm_g, k_norm_g, out_norm_g, w_out, norm2_g, router_w, router_b, w_gate_up, b_gate_up,
           w_down, b_down):
    batch, seq, d = x.shape
    params = dict(norm1_g=norm1_g, w_in=w_in, pool_w=pool_w, pool_scale=pool_scale, conv_w=conv_w, conv_b=conv_b,
                  lru_wa=lru_wa, lru_ba=lru_ba, lru_wx=lru_wx, lru_bx=lru_bx, lru_lambda=lru_lambda,
                  q_norm_g=q_norm_g, k_norm_g=k_norm_g, out_norm_g=out_norm_g, w_out=w_out, norm2_g=norm2_g,
                  router_w=router_w, router_b=router_b, w_gate_up=w_gate_up, b_gate_up=b_gate_up,
                  w_down=w_down, b_down=b_down)
    cos_t, sin_t = _rope_tables(seq)
    xt = x.reshape(batch * seq, d)
    for layer in range(norm1_g.shape[0]):
        xt = _layer(xt, cos_t, sin_t, batch, seq, {name: val[layer] for name, val in params.items()})
    return xt.reshape(batch, seq, d)
```

```python
import functools

import jax
import jax.numpy as jnp
from jax import lax
from jax.experimental import pallas as pl
from jax.experimental.pallas import tpu as pltpu
from jax.scipy.linalg import block_diag

f32 = jnp.float32
bf16 = jnp.bfloat16
i32 = jnp.int32

D_MODEL = 1024
POOL_WIDTH = 256
POOL_WINDOWS = (2, 4, 8, 16)
LRU_WIDTH = 256
LRU_C = 8.0
ATTN_WIDTH = 512
HEAD_DIM = 64
DILATIONS = (1, 4, 16)
MAX_DIL = DILATIONS[-1]
ATTN_BLOCK = 128
ROPE_THETA = 10000.0
N_EXPERTS = 32
TOP_K = 4
EXPERT_FF = 256
SWIGLU_LIMIT = 7.0
SWIGLU_ALPHA = 1.702
NORM_EPS = 1e-6
NEG_INF = -1e30

LANES = 128
POOL_HALO = 16
CONV_HALO = 8
VMEM_LIMIT = 56 * 1024 * 1024

PROJ_TILE = 512
MIX_TILE = 512
MOE_TILE = 2048
MOE_CHUNK = 512
MOE_TAIL_SIZES = (64, 128, 192, 256, 320, 384, 448, MOE_CHUNK)
ATTN_SCORES_AHEAD = 2


def _rms(x, g):
    return x * lax.rsqrt(jnp.mean(x * x, axis=-1, keepdims=True) + NORM_EPS) * g


_IN_SPLITS = (0, 256, 512, 768, 1280, 1792, 2304)


QK_QUARTER = HEAD_DIM // 2


def _pair_head0(shape):
    return (lax.broadcasted_iota(i32, shape, 1) % HEAD_DIM) < QK_QUARTER


def _qk_norm_rope(y, cs, sn, gain, scale):
    head0 = _pair_head0(y.shape)
    yy = y * y
    ms0 = jnp.sum(jnp.where(head0, yy, 0.0), axis=-1, keepdims=True)
    ms1 = jnp.sum(jnp.where(head0, 0.0, yy), axis=-1, keepdims=True)
    ms = jnp.where(head0, ms0, ms1) * (1.0 / HEAD_DIM)
    yn = y * lax.rsqrt(ms + NORM_EPS) * gain
    return (yn * cs + pltpu.roll(yn, HEAD_DIM, 1) * sn) * scale


def _in_proj_kernel(x_ref, g_ref, w_ref, cos_ref, sin_ref, qg_ref, kg_ref, *refs):
    out_refs, stage_s = refs[:-1], refs[-1]
    tm = x_ref.shape[0]
    h = _rms(x_ref[...], g_ref[...]).astype(bf16)

    def proj(n):
        return jnp.dot(h, w_ref[:, _IN_SPLITS[n]:_IN_SPLITS[n + 1]], preferred_element_type=f32)

    def emit_class_major(n, y, post=None):
        for p_, c in enumerate(range(0, y.shape[1], LANES)):
            yp = y[:, c:c + LANES]
            stage_s[n - 3, p_] = yp if post is None else _qk_norm_rope(yp, cos_ref[...], sin_ref[...], *post)
            for r in range(MAX_DIL):
                out_refs[n][0, r, :, c:c + LANES] = stage_s[n - 3, p_, pl.ds(r, tm // MAX_DIL, stride=MAX_DIL), :]

    q = proj(3)
    k = proj(4)
    emit_class_major(3, q, (qg_ref[...], HEAD_DIM ** -0.5))
    v = proj(5)
    out_refs[0][...] = proj(0)
    emit_class_major(4, k, (kg_ref[...], 1.0))
    out_refs[1][...] = proj(1)
    emit_class_major(5, v)
    out_refs[2][...] = proj(2)


def _in_proj(x, g, w, cos_t, sin_t, qg, kg, batch, seq):
    t = x.shape[0]
    tm = PROJ_TILE
    nt = seq // tm
    widths = [c1 - c0 for c0, c1 in zip(_IN_SPLITS[:-1], _IN_SPLITS[1:])]
    table = pl.BlockSpec((tm, LANES), lambda i: (i % nt, 0))
    gain = pl.BlockSpec((1, LANES), lambda i: (0, 0))
    flat = lambda w_: (pl.BlockSpec((tm, w_), lambda i: (i, 0)), jax.ShapeDtypeStruct((t, w_), f32))
    cls_major = lambda w_: (pl.BlockSpec((1, MAX_DIL, tm // MAX_DIL, w_), lambda i: (i // nt, 0, i % nt, 0)),
                            jax.ShapeDtypeStruct((batch, MAX_DIL, seq // MAX_DIL, w_), f32))
    outs = [flat(w_) for w_ in widths[:3]] + [cls_major(w_) for w_ in widths[3:]]
    return pl.pallas_call(
        _in_proj_kernel,
        grid=(t // tm,),
        in_specs=[pl.BlockSpec((tm, D_MODEL), lambda i: (i, 0)),
                  pl.BlockSpec((1, D_MODEL), lambda i: (0, 0)),
                  pl.BlockSpec((D_MODEL, _IN_SPLITS[-1]), lambda i: (0, 0)),
                  table, table, gain, gain],
        out_specs=[o[0] for o in outs],
        out_shape=[o[1] for o in outs],
        scratch_shapes=[pltpu.VMEM((3, ATTN_WIDTH // LANES, tm, LANES), f32)],
        compiler_params=pltpu.CompilerParams(dimension_semantics=("arbitrary",),
                                             vmem_limit_bytes=VMEM_LIMIT),
        name="in_proj",
    )(x, g, w, cos_t, sin_t, qg, kg)


def _mixer_kernel(pool_ref, lru_ref, gate_ref, pw_ref, ps_ref, cw_ref, cb_ref, wri_ref, bri_ref, lam_ref,
                  ypool_ref, ylru_ref,
                  b0, b1, b2, b3, cbuf, a_s, b_s, h_s, a2_s, b2_s, e2_s, e3_s, hc, *, ts):
    j = pl.program_id(1)
    n1 = ts // 8
    n2 = n1 // 8

    @pl.when(j == 0)
    def _():
        for buf in (b0, b1, b2, b3):
            buf[pl.ds(0, POOL_HALO), :] = jnp.zeros((POOL_HALO, POOL_WIDTH), f32)
        cbuf[pl.ds(0, CONV_HALO), :] = jnp.zeros((CONV_HALO, LRU_WIDTH), f32)
        hc[...] = jnp.zeros_like(hc)

    row = lax.broadcasted_iota(i32, (ts, POOL_WIDTH), 0) + j * ts
    grp = lax.broadcasted_iota(i32, (ts, POOL_WIDTH), 1) // (POOL_WIDTH // len(POOL_WINDOWS))

    u = pool_ref[...]
    b0[pl.ds(POOL_HALO, ts), :] = u
    s1 = u + b0[pl.ds(POOL_HALO - 1, ts), :]
    b1[pl.ds(POOL_HALO, ts), :] = s1
    s2 = s1 + b1[pl.ds(POOL_HALO - 2, ts), :]
    b2[pl.ds(POOL_HALO, ts), :] = s2
    s3 = s2 + b2[pl.ds(POOL_HALO - 4, ts), :]
    b3[pl.ds(POOL_HALO, ts), :] = s3
    s4 = s3 + b3[pl.ds(POOL_HALO - 8, ts), :]
    for buf in (b0, b1, b2, b3):
        buf[pl.ds(0, POOL_HALO), :] = buf[pl.ds(ts, POOL_HALO), :]
    sums = jnp.where(grp == 0, s1, jnp.where(grp == 1, s2, jnp.where(grp == 2, s3, s4)))
    win = jnp.where(grp == 0, float(POOL_WINDOWS[0]),
                    jnp.where(grp == 1, float(POOL_WINDOWS[1]),
                              jnp.where(grp == 2, float(POOL_WINDOWS[2]), float(POOL_WINDOWS[3]))))
    cnt = jnp.minimum((row + 1).astype(f32), win)
    pooled = sums / cnt - u
    ypool_ref[...] = jnp.dot(pooled.astype(bf16), pw_ref[...], preferred_element_type=f32) * ps_ref[...]

    lu = lru_ref[...]
    cbuf[pl.ds(CONV_HALO, ts), :] = lu
    xc = (cb_ref[...] + cw_ref[3:4, :] * lu
          + cw_ref[2:3, :] * cbuf[pl.ds(CONV_HALO - 1, ts), :]
          + cw_ref[1:2, :] * cbuf[pl.ds(CONV_HALO - 2, ts), :]
          + cw_ref[0:1, :] * cbuf[pl.ds(CONV_HALO - 3, ts), :])
    cbuf[pl.ds(0, CONV_HALO), :] = cbuf[pl.ds(ts, CONV_HALO), :]
    ri = jnp.dot(xc.astype(bf16), wri_ref[...], preferred_element_type=f32) + bri_ref[...]
    r = jax.nn.sigmoid(ri[:, :LRU_WIDTH])
    ig = jax.nn.sigmoid(ri[:, LRU_WIDTH:])
    log_a = -LRU_C * r * jax.nn.softplus(-lam_ref[...])
    a = jnp.exp(log_a)
    th = jnp.tanh(log_a)
    mult = jnp.sqrt(-2.0 * th / (1.0 - th))
    mult = jnp.where(row == 0, 1.0, mult)
    bb = mult * (ig * xc)

    def rows(ref, hf, s, n):
        return ref[hf, pl.ds(s, n, stride=8), :]

    for hf in range(LRU_WIDTH // LANES):
        a_s[hf] = a[:, hf * LANES:(hf + 1) * LANES]
        b_s[hf] = bb[:, hf * LANES:(hf + 1) * LANES]
        pa, pb = rows(a_s, hf, 0, n1), rows(b_s, hf, 0, n1)
        for s in range(1, 8):
            as_ = rows(a_s, hf, s, n1)
            pb = as_ * pb + rows(b_s, hf, s, n1)
            pa = as_ * pa
        a2_s[hf] = pa
        b2_s[hf] = pb
        pa, pb = rows(a2_s, hf, 0, n2), rows(b2_s, hf, 0, n2)
        for s in range(1, 8):
            as_ = rows(a2_s, hf, s, n2)
            pb = as_ * pb + rows(b2_s, hf, s, n2)
            pa = as_ * pa
        h = hc[hf, 0:1, :]
        for k in range(8):
            e3_s[hf, k:k + 1, :] = h
            h = pa[k:k + 1, :] * h + pb[k:k + 1, :]
        hc[hf, 0:1, :] = h
        prev = e3_s[hf]
        for s in range(8):
            e2_s[hf, pl.ds(s, n2, stride=8), :] = prev
            prev = rows(a2_s, hf, s, n2) * prev + rows(b2_s, hf, s, n2)
        prev = e2_s[hf]
        for s in range(8):
            prev = rows(a_s, hf, s, n1) * prev + rows(b_s, hf, s, n1)
            h_s[hf, pl.ds(s, n1, stride=8), :] = prev

    hfull = jnp.concatenate([h_s[hf] for hf in range(LRU_WIDTH // LANES)], axis=1)
    ylru_ref[...] = hfull * jax.nn.gelu(gate_ref[...])


def _mixer(pool_u, lru_u, gate, pw, ps, cw, cb, wri, bri, lam, batch, seq):
    ts = MIX_TILE
    nt = seq // ts
    t = batch * seq
    nh = LRU_WIDTH // LANES
    tile = lambda w_: pl.BlockSpec((ts, w_), lambda b, j: (b * nt + j, 0))
    full = lambda shape: pl.BlockSpec(shape, lambda b, j: (0,) * len(shape))
    return pl.pallas_call(
        functools.partial(_mixer_kernel, ts=ts),
        grid=(batch, nt),
        in_specs=[tile(POOL_WIDTH), tile(LRU_WIDTH), tile(LRU_WIDTH),
                  full((POOL_WIDTH, POOL_WIDTH)), full((1, POOL_WIDTH)),
                  full((4, LRU_WIDTH)), full((1, LRU_WIDTH)),
                  full((LRU_WIDTH, 2 * LRU_WIDTH)), full((1, 2 * LRU_WIDTH)), full((1, LRU_WIDTH))],
        out_specs=[tile(POOL_WIDTH), tile(LRU_WIDTH)],
        out_shape=[jax.ShapeDtypeStruct((t, POOL_WIDTH), f32), jax.ShapeDtypeStruct((t, LRU_WIDTH), f32)],
        scratch_shapes=[pltpu.VMEM((ts + POOL_HALO, POOL_WIDTH), f32)] * 4
        + [pltpu.VMEM((ts + CONV_HALO, LRU_WIDTH), f32)]
        + [pltpu.VMEM((nh, ts, LANES), f32)] * 3
        + [pltpu.VMEM((nh, ts // 8, LANES), f32)] * 3
        + [pltpu.VMEM((nh, 8, LANES), f32)] * 2,
        compiler_params=pltpu.CompilerParams(dimension_semantics=("arbitrary", "arbitrary"),
                                             vmem_limit_bytes=VMEM_LIMIT),
        name="pool_lru_mixer",
    )(pool_u, lru_u, gate, pw, ps, cw, cb, wri, bri, lam)


def _attn_kernel(q_ref, k_ref, v_ref, o_ref, *state_refs, seq):
    blk = ATTN_BLOCK
    cls = seq // MAX_DIL
    qs, ks, vs = q_ref.at[0], k_ref.at[0], v_ref.at[0]

    h0 = lax.broadcasted_iota(i32, (blk, LANES), 1) < HEAD_DIM
    q_h0 = _pair_head0((blk, LANES))
    states = [state_refs[3 * g:3 * g + 3] for g in range(len(DILATIONS))]
    qi = lax.broadcasted_iota(i32, (2 * blk, 2 * blk), 0) % blk
    kj = lax.broadcasted_iota(i32, (2 * blk, 2 * blk), 1)

    for bi, d in enumerate(DILATIONS):
        nb = seq // (blk * d)
        runs = MAX_DIL // d
        run = blk // runs

        def member(i, runs=runs, run=run):
            return runs * (i % run) + i // run

        jq = member(qi)
        jk = member(kj % blk)
        valid_first = member(lax.broadcasted_iota(i32, (2 * blk, blk), 1)) <= member(lax.broadcasted_iota(i32, (2 * blk, blk), 0) % blk)
        valid_later = ((kj >= blk) & (jk <= jq)) | ((kj < blk) & (jk >= jq))

        def rows(res, n, d=d, runs=runs, run=run):
            return [(res + d * b, run * n) for b in range(runs)]

        def ld(ref, where, run=run):
            return jnp.concatenate([ref[c, r0:r0 + run, :] for c, r0 in where], axis=0)

        def score(idx, nb=nb, rows=rows, ld=ld):
            res, n = divmod(idx, nb)
            cur = rows(res, n)
            keys = cur if n == 0 else rows(res, n - 1) + cur
            qb = ld(qs, cur)
            q2 = jnp.concatenate([jnp.where(q_h0, qb, 0.0), jnp.where(q_h0, 0.0, qb)], axis=0).astype(bf16)
            return lax.dot_general(q2, ld(ks, keys).astype(bf16), (((1,), (1,)), ((), ())),
                                   preferred_element_type=f32)

        def finish(idx, s, nb=nb, rows=rows, ld=ld, run=run, bi=bi, valid_first=valid_first, valid_later=valid_later):
            res, n = divmod(idx, nb)
            cur = rows(res, n)
            keys = cur if n == 0 else rows(res, n - 1) + cur
            s = jnp.where(valid_first if n == 0 else valid_later, s, NEG_INF)
            mb = jnp.max(s, axis=-1, keepdims=True)
            p = jnp.exp(s - mb).astype(bf16)
            vb = ld(vs, keys).astype(bf16)
            c = jnp.dot(p, jnp.concatenate([vb, jnp.ones_like(vb)], axis=1), preferred_element_type=f32)
            vals = (jnp.where(h0, mb[:blk], mb[blk:]), jnp.where(h0, c[:blk, LANES:], c[blk:, LANES:]),
                    jnp.where(h0, c[:blk, :LANES], c[blk:, :LANES]))
            for ref, val in zip(states[bi], vals):
                for b_, (cls_i, r0) in enumerate(cur):
                    ref[cls_i, r0:r0 + run, :] = val[b_ * run:(b_ + 1) * run, :]

        n_blocks = seq // blk
        scores = [score(idx) for idx in range(ATTN_SCORES_AHEAD)]
        for idx in range(n_blocks):
            if idx + ATTN_SCORES_AHEAD < n_blocks:
                scores.append(score(idx + ATTN_SCORES_AHEAD))
            finish(idx, scores[idx])
            scores[idx] = None

    def fin(r, carry):
        ms = [st_[0][r] for st_ in states]
        m = functools.reduce(jnp.maximum, ms)
        num = den = None
        for (m_ref, l_ref, a_ref), m_g in zip(states, ms):
            w = jnp.exp(m_g - m)
            num = w * a_ref[r] if num is None else num + w * a_ref[r]
            den = w * l_ref[r] if den is None else den + w * l_ref[r]
        o_ref[0, r] = num / den
        return carry

    lax.fori_loop(0, MAX_DIL, fin, 0)


def _attention(q, k, v, batch, seq):
    npair = ATTN_WIDTH // LANES
    cls = seq // MAX_DIL
    tile = pl.BlockSpec((1, MAX_DIL, cls, LANES), lambda b, p: (b, 0, 0, p))
    return pl.pallas_call(
        functools.partial(_attn_kernel, seq=seq),
        grid=(batch, npair),
        in_specs=[tile, tile, tile],
        out_specs=tile,
        out_shape=jax.ShapeDtypeStruct((batch, MAX_DIL, cls, ATTN_WIDTH), f32),
        scratch_shapes=[pltpu.VMEM((MAX_DIL, cls, LANES), f32)] * (3 * len(DILATIONS)),
        compiler_params=pltpu.CompilerParams(dimension_semantics=("arbitrary", "arbitrary"),
                                             vmem_limit_bytes=VMEM_LIMIT),
        name="dilated_attention",
    )(q, k, v)


def _out_proj_kernel(x_ref, yp_ref, yl_ref, ya_ref, og_ref, wo_ref, g2_ref, rw_ref, rb_ref,
                     xo_ref, idx_ref, gate_ref, rank_ref, cnt_ref, cnt_s, before_s, ya_s, *, tm, steps_per_tile):
    i = pl.program_id(0)
    c0, c1 = POOL_WIDTH, POOL_WIDTH + LRU_WIDTH
    for r in range(MAX_DIL):
        for p_ in range(ATTN_WIDTH // LANES):
            ya_s[p_, pl.ds(r, tm // MAX_DIL, stride=MAX_DIL), :] = ya_ref[0, r, :, p_ * LANES:(p_ + 1) * LANES]
    ya = jnp.concatenate([ya_s[p_] for p_ in range(ATTN_WIDTH // LANES)], axis=1)
    acc = x_ref[...]
    for y, lo, hi in ((yp_ref[...], 0, c0), (yl_ref[...], c0, c1), (ya, c1, D_MODEL)):
        mix = _rms(y, og_ref[:, lo:hi]).astype(bf16)
        acc = acc + jnp.dot(mix, wo_ref[lo:hi, :], preferred_element_type=f32)
    xo_ref[...] = acc

    h2 = _rms(acc, g2_ref[...])
    h_hi = h2.astype(bf16)
    h_lo = (h2 - h_hi.astype(f32)).astype(bf16)
    nt = (((1,), (1,)), ((), ()))
    both = lax.dot_general(rw_ref[...], h_hi, nt, preferred_element_type=f32)
    logits = (both[:N_EXPERTS] + both[N_EXPERTS:]
              + lax.dot_general(rw_ref[:N_EXPERTS, :], h_lo, nt, preferred_element_type=f32)
              + rb_ref[...])
    eio = lax.broadcasted_iota(i32, (N_EXPERTS, tm), 0)
    cur = logits
    vals, idxs = [], []
    for _ in range(TOP_K):
        mx = jnp.max(cur, axis=0, keepdims=True)
        ix = jnp.min(jnp.where(cur == mx, eio, N_EXPERTS), axis=0, keepdims=True)
        vals.append(mx)
        idxs.append(ix)
        cur = jnp.where(eio == ix, -jnp.inf, cur)
    ex = [jnp.exp(v - vals[0]) for v in vals]
    den = ex[0] + ex[1] + ex[2] + ex[3]

    @pl.when(i % steps_per_tile == 0)
    def _():
        cnt_s[...] = jnp.zeros_like(cnt_s)

    onehot = jnp.zeros((N_EXPERTS, tm), f32)
    for ix in idxs:
        onehot = onehot + (eio == ix).astype(f32)
    @pl.when(i == 0)
    def _():
        before_s[...] = (lax.broadcasted_iota(i32, (tm, tm), 0) < lax.broadcasted_iota(i32, (tm, tm), 1)).astype(bf16)

    prefix = jnp.dot(onehot.astype(bf16), before_s[...], preferred_element_type=f32) + cnt_s[:, 0:1]
    for k in range(TOP_K):
        idx_ref[k:k + 1, :] = idxs[k]
        gate_ref[k:k + 1, :] = ex[k] / den
        rank_ref[k:k + 1, :] = jnp.sum(jnp.where(eio == idxs[k], prefix, 0.0), axis=0, keepdims=True).astype(i32)
    cnt_s[...] = cnt_s[...] + jnp.sum(onehot, axis=1, keepdims=True)
    cnt_ref[0] = cnt_s[...]


def _out_proj(x, yp, yl, ya, og, wo, g2, rw_t, rb, seq):
    t = x.shape[0]
    tm = PROJ_TILE
    steps = t // tm
    nt = seq // tm
    tile = lambda w_: pl.BlockSpec((tm, w_), lambda i: (i, 0))
    full = lambda shape: pl.BlockSpec(shape, lambda i: (0,) * len(shape))
    route = pl.BlockSpec((TOP_K, tm), lambda i: (0, i))
    return pl.pallas_call(
        functools.partial(_out_proj_kernel, tm=tm, steps_per_tile=MOE_TILE // tm),
        grid=(steps,),
        in_specs=[tile(D_MODEL), tile(POOL_WIDTH), tile(LRU_WIDTH),
                  pl.BlockSpec((1, MAX_DIL, tm // MAX_DIL, ATTN_WIDTH), lambda i: (i // nt, 0, i % nt, 0)),
                  full((1, D_MODEL)), full((D_MODEL, D_MODEL)), full((1, D_MODEL)),
                  full((2 * N_EXPERTS, D_MODEL)), full((N_EXPERTS, 1))],
        out_specs=[tile(D_MODEL), route, route, route,
                   pl.BlockSpec((1, N_EXPERTS, LANES), lambda i: (i, 0, 0))],
        out_shape=[jax.ShapeDtypeStruct((t, D_MODEL), f32),
                   jax.ShapeDtypeStruct((TOP_K, t), i32),
                   jax.ShapeDtypeStruct((TOP_K, t), f32),
                   jax.ShapeDtypeStruct((TOP_K, t), i32),
                   jax.ShapeDtypeStruct((steps, N_EXPERTS, LANES), f32)],
        scratch_shapes=[pltpu.VMEM((N_EXPERTS, LANES), f32), pltpu.VMEM((tm, tm), bf16),
                        pltpu.VMEM((ATTN_WIDTH // LANES, tm, LANES), f32)],
        compiler_params=pltpu.CompilerParams(dimension_semantics=("arbitrary",),
                                             vmem_limit_bytes=VMEM_LIMIT),
        name="out_proj_router",
    )(x, yp, yl, ya, og, wo, g2, rw_t, rb)


ROW_SLAB = 8
ROUTE_GROUP = 8
MOE_EXPERTS_PER_STEP = 2


def _slab_cols(ref, first_row, n_rows, col):
    return ref[pl.ds(first_row * ROW_SLAB + col, n_rows, stride=ROW_SLAB), :]


def _moe_kernel(off_ref, pos_ref, gate_ref, x_ref, g2_ref, wgu_ref, bgu_ref, wd_ref, bd_ref,
                o_ref, sorted_s, stage_s, *, tt, ch, io_rows):
    i = pl.program_id(0)
    s = pl.program_id(1)
    n_io = tt // io_rows
    n_exp_steps = N_EXPERTS // MOE_EXPERTS_PER_STEP
    base = i * (N_EXPERTS + 1)
    n_col = D_MODEL // LANES

    def slab(ref, first_sublane):
        return ref.at[pl.ds(pl.multiple_of(first_sublane, ROW_SLAB), ROW_SLAB), :]

    @pl.when(s < n_io)
    def _dispatch():
        @pl.when(s == 0)
        def _():
            sorted_s[pl.ds(TOP_K * tt * ROW_SLAB, ch * ROW_SLAB), :] = jnp.zeros((ch * ROW_SLAB, LANES), f32)

        h = _rms(x_ref[...], g2_ref[...])
        for j in range(n_col):
            stage_s[pl.ds(j, io_rows, stride=ROW_SLAB), :] = h[:, j * LANES:(j + 1) * LANES]

        for c in range(io_rows // ROUTE_GROUP):
            for u_ in range(ROUTE_GROUP):
                row = stage_s[(c * ROUTE_GROUP + u_) * ROW_SLAB:(c * ROUTE_GROUP + u_ + 1) * ROW_SLAB, :]
                for k in range(TOP_K):
                    slab(sorted_s, pos_ref[k, c * ROUTE_GROUP + u_])[...] = row

    @pl.when((s >= n_io) & (s < n_io + n_exp_steps))
    def _experts():
        def run_chunk(sub, start, hi, rows):
            xc = jnp.concatenate([_slab_cols(sorted_s, start, rows, j) for j in range(n_col)], axis=1)
            gu = jnp.dot(xc.astype(bf16), wgu_ref[sub], preferred_element_type=f32) + bgu_ref[sub]
            gate = jnp.minimum(gu[:, :EXPERT_FF], SWIGLU_LIMIT)
            up = jnp.clip(gu[:, EXPERT_FF:], -SWIGLU_LIMIT, SWIGLU_LIMIT)
            act = (up + 1.0) * gate * jax.nn.sigmoid(SWIGLU_ALPHA * gate)
            y = jnp.dot(act.astype(bf16), wd_ref[sub], preferred_element_type=f32) + bd_ref[sub]
            mine = (lax.broadcasted_iota(i32, (rows, 1), 0) + start) < hi
            y = jnp.where(mine, y, xc)
            for j in range(n_col):
                sorted_s[pl.ds(start * ROW_SLAB + j, rows, stride=ROW_SLAB), :] = y[:, j * LANES:(j + 1) * LANES]

        def expert(sub, carry):
            e = (s - n_io) * MOE_EXPERTS_PER_STEP + sub
            lo = off_ref[base + e]
            hi = off_ref[base + e + 1]
            n_full = jnp.maximum(hi - lo - 1, 0) // ch

            def full(c, carry2):
                run_chunk(sub, lo + c * ch, hi, ch)
                return carry2

            lax.fori_loop(0, n_full, full, 0)
            start = lo + n_full * ch
            rest = hi - start
            for size_lo, size in zip((0,) + MOE_TAIL_SIZES[:-1], MOE_TAIL_SIZES):
                @pl.when((rest > size_lo) & (rest <= size))
                def _(size=size):
                    run_chunk(sub, start, hi, size)
            return carry

        lax.fori_loop(0, MOE_EXPERTS_PER_STEP, expert, 0)

    @pl.when(s >= n_io + n_exp_steps)
    def _combine():
        for c in range(io_rows // ROUTE_GROUP):
            for u_ in range(ROUTE_GROUP):
                acc = gate_ref[0, c * ROUTE_GROUP + u_] * slab(sorted_s, pos_ref[0, c * ROUTE_GROUP + u_])[...]
                for k in range(1, TOP_K):
                    acc = acc + gate_ref[k, c * ROUTE_GROUP + u_] * slab(sorted_s, pos_ref[k, c * ROUTE_GROUP + u_])[...]
                stage_s[(c * ROUTE_GROUP + u_) * ROW_SLAB:(c * ROUTE_GROUP + u_ + 1) * ROW_SLAB, :] = acc
        moe = jnp.concatenate([_slab_cols(stage_s, 0, io_rows, j) for j in range(n_col)], axis=1)
        o_ref[...] = x_ref[...] + moe


def _moe(x, pos, gates, offsets, g2, wgu, bgu, wd, bd):
    t = x.shape[0]
    tt = MOE_TILE
    ch = MOE_CHUNK
    io_rows = PROJ_TILE
    n_io = tt // io_rows
    eps = MOE_EXPERTS_PER_STEP
    n_exp_steps = N_EXPERTS // eps
    n_steps = 2 * n_io + n_exp_steps

    def tok_idx(i, s, off):
        in_combine = s >= n_io + n_exp_steps
        return i * n_io + jnp.where(s < n_io, s, jnp.where(in_combine, s - n_io - n_exp_steps, n_io - 1))

    def out_idx(i, s, off):
        return i * n_io + jnp.maximum(s - n_io - n_exp_steps, 0)

    def expert_idx(i, s, off):
        return jnp.clip(s - n_io, 0, n_exp_steps - 1)

    route = pl.BlockSpec((TOP_K, io_rows), lambda i, s, off: (0, tok_idx(i, s, off)), memory_space=pltpu.SMEM)
    return pl.pallas_call(
        functools.partial(_moe_kernel, tt=tt, ch=ch, io_rows=io_rows),
        grid_spec=pltpu.PrefetchScalarGridSpec(
            num_scalar_prefetch=1,
            grid=(t // tt, n_steps),
            in_specs=[route, route,
                      pl.BlockSpec((io_rows, D_MODEL), lambda i, s, off: (tok_idx(i, s, off), 0)),
                      pl.BlockSpec((1, D_MODEL), lambda i, s, off: (0, 0)),
                      pl.BlockSpec((eps, D_MODEL, 2 * EXPERT_FF), lambda i, s, off: (expert_idx(i, s, off), 0, 0)),
                      pl.BlockSpec((eps, 1, 2 * EXPERT_FF), lambda i, s, off: (expert_idx(i, s, off), 0, 0)),
                      pl.BlockSpec((eps, EXPERT_FF, D_MODEL), lambda i, s, off: (expert_idx(i, s, off), 0, 0)),
                      pl.BlockSpec((eps, 1, D_MODEL), lambda i, s, off: (expert_idx(i, s, off), 0, 0))],
            out_specs=pl.BlockSpec((io_rows, D_MODEL), lambda i, s, off: (out_idx(i, s, off), 0)),
            scratch_shapes=[pltpu.VMEM(((TOP_K * tt + ch) * ROW_SLAB, LANES), f32),
                            pltpu.VMEM((io_rows * ROW_SLAB, LANES), f32)]),
        out_shape=jax.ShapeDtypeStruct((t, D_MODEL), f32),
        compiler_params=pltpu.CompilerParams(dimension_semantics=("arbitrary", "arbitrary"),
                                             vmem_limit_bytes=VMEM_LIMIT),
        name="moe_experts",
    )(offsets, pos, gates, x, g2, wgu, bgu, wd, bd)


def _pair_layout(v):
    first, second = v[..., :QK_QUARTER], v[..., QK_QUARTER:]
    return jnp.concatenate([first, first, second, second], axis=-1)


def _qk_column_order():
    pair = jnp.arange(LANES).reshape(2, 2, QK_QUARTER).transpose(1, 0, 2).reshape(-1)
    cols = jnp.arange(_IN_SPLITS[-1])
    qk = cols[_IN_SPLITS[3]:_IN_SPLITS[5]].reshape(-1, LANES)[:, pair].reshape(-1)
    return jnp.concatenate([cols[:_IN_SPLITS[3]], qk, cols[_IN_SPLITS[5]:]])


def _rope_tables(seq):
    pos = jnp.arange(seq, dtype=f32)
    inv_freq = ROPE_THETA ** (-jnp.arange(0, HEAD_DIM, 2, dtype=f32) / HEAD_DIM)
    ang = pos[:, None] * inv_freq[None, :]
    cos, sin = jnp.cos(ang), jnp.sin(ang)
    return _pair_layout(jnp.concatenate([cos, cos], axis=1)), _pair_layout(jnp.concatenate([-sin, sin], axis=1))


def _layer(x, cos_t, sin_t, batch, seq, p):
    t = batch * seq
    row = lambda v: v.reshape(1, -1)
    pool_u, lru_u, gate, q, k, v = _in_proj(x, row(p["norm1_g"]), p["w_in"][:, _qk_column_order()].astype(bf16),
                                            cos_t, sin_t, row(_pair_layout(p["q_norm_g"])),
                                            row(_pair_layout(p["k_norm_g"])), batch, seq)
    wri = jnp.concatenate([block_diag(*p["lru_wa"]), block_diag(*p["lru_wx"])], axis=1).astype(bf16)
    bri = jnp.concatenate([p["lru_ba"], p["lru_bx"]]).reshape(1, -1)
    y_pool, y_lru = _mixer(pool_u, lru_u, gate, block_diag(*p["pool_w"]).astype(bf16), row(p["pool_scale"]),
                           p["conv_w"], row(p["conv_b"]), wri, bri, row(p["lru_lambda"]), batch, seq)
    y_attn = _attention(q, k, v, batch, seq)
    rw_hi = p["router_w"].T.astype(bf16)
    rw_split = jnp.concatenate([rw_hi, (p["router_w"].T - rw_hi.astype(f32)).astype(bf16)], axis=0)
    x, idx, gates, rank, cnt = _out_proj(x, y_pool, y_lru, y_attn, row(p["out_norm_g"]), p["w_out"].astype(bf16),
                                         row(p["norm2_g"]), rw_split, p["router_b"].reshape(-1, 1), seq)
    steps_per_tile = MOE_TILE // PROJ_TILE
    n_tiles = t // MOE_TILE
    counts = cnt[steps_per_tile - 1::steps_per_tile, :, 0].astype(i32)
    offsets = jnp.concatenate([jnp.zeros((n_tiles, 1), i32), jnp.cumsum(counts, axis=1)], axis=1)
    idx_t = idx.reshape(TOP_K, n_tiles, MOE_TILE)
    seg_start = jnp.zeros_like(idx_t)
    for e in range(N_EXPERTS):
        seg_start = jnp.where(idx_t == e, offsets[None, :, e, None], seg_start)
    pos = (rank + seg_start.reshape(TOP_K, t)) * ROW_SLAB
    return _moe(x, pos, gates, offsets.reshape(-1), row(p["norm2_g"]),
                p["w_gate_up"].astype(bf16), p["b_gate_up"][:, None, :],
                p["w_down"].astype(bf16), p["b_down"][:, None, :])


def kernel(x, norm1_g, w_in, pool_w, pool_scale, conv_w, conv_b, lru_wa, lru_ba, lru_wx, lru_bx, lru_lambda,
           q_norm_g, k_norm_g, out_norm_g, w_out, norm2_g, router_w, router_b, w_gate_up, b_gate_up,
           w_down, b_down):
    batch, seq, d = x.shape
    params = dict(norm1_g=norm1_g, w_in=w_in, pool_w=pool_w, pool_scale=pool_scale, conv_w=conv_w, conv_b=conv_b,
                  lru_wa=lru_wa, lru_ba=lru_ba, lru_wx=lru_wx, lru_bx=lru_bx, lru_lambda=lru_lambda,
                  q_norm_g=q_norm_g, k_norm_g=k_norm_g, out_norm_g=out_norm_g, w_out=w_out, norm2_g=norm2_g,
                  router_w=router_w, router_b=router_b, w_gate_up=w_gate_up, b_gate_up=b_gate_up,
                  w_down=w_down, b_down=b_down)
    cos_t, sin_t = _rope_tables(seq)
    xt = x.reshape(batch * seq, d)
    for layer in range(norm1_g.shape[0]):
        xt = _layer(xt, cos_t, sin_t, batch, seq, {name: val[layer] for name, val in params.items()})
    return xt.reshape(batch, seq, d)
```

```python
import functools

import jax
import jax.numpy as jnp
from jax import lax
from jax.experimental import pallas as pl
from jax.experimental.pallas import tpu as pltpu
from jax.scipy.linalg import block_diag

f32 = jnp.float32
bf16 = jnp.bfloat16
i32 = jnp.int32

D_MODEL = 1024
POOL_WIDTH = 256
POOL_WINDOWS = (2, 4, 8, 16)
LRU_WIDTH = 256
LRU_C = 8.0
ATTN_WIDTH = 512
HEAD_DIM = 64
DILATIONS = (1, 4, 16)
MAX_DIL = DILATIONS[-1]
ATTN_BLOCK = 128
ROPE_THETA = 10000.0
N_EXPERTS = 32
TOP_K = 4
EXPERT_FF = 256
SWIGLU_LIMIT = 7.0
SWIGLU_ALPHA = 1.702
NORM_EPS = 1e-6
NEG_INF = -1e30

LANES = 128
POOL_HALO = 16
CONV_HALO = 8
VMEM_LIMIT = 56 * 1024 * 1024

PROJ_TILE = 512
MIX_TILE = 512
MOE_TILE = 2048
MOE_CHUNK = 512
MOE_TAIL_SIZES = (64, 128, 192, 256, 320, 384, 448, MOE_CHUNK)
ATTN_SCORES_AHEAD = 3


def _rms(x, g):
    return x * lax.rsqrt(jnp.mean(x * x, axis=-1, keepdims=True) + NORM_EPS) * g


_IN_SPLITS = (0, 256, 512, 768, 1280, 1792, 2304)


QK_QUARTER = HEAD_DIM // 2


def _pair_head0(shape):
    return (lax.broadcasted_iota(i32, shape, 1) % HEAD_DIM) < QK_QUARTER


def _qk_norm_rope(y, cs, sn, gain, scale):
    head0 = _pair_head0(y.shape)
    yy = y * y
    ms0 = jnp.sum(jnp.where(head0, yy, 0.0), axis=-1, keepdims=True)
    ms1 = jnp.sum(jnp.where(head0, 0.0, yy), axis=-1, keepdims=True)
    ms = jnp.where(head0, ms0, ms1) * (1.0 / HEAD_DIM)
    yn = y * lax.rsqrt(ms + NORM_EPS) * gain
    return (yn * cs + pltpu.roll(yn, HEAD_DIM, 1) * sn) * scale


def _in_proj_kernel(x_ref, g_ref, w_ref, cos_ref, sin_ref, qg_ref, kg_ref, *refs):
    out_refs, stage_s = refs[:-1], refs[-1]
    tm = x_ref.shape[0]
    h = _rms(x_ref[...], g_ref[...]).astype(bf16)

    def proj(n):
        return jnp.dot(h, w_ref[:, _IN_SPLITS[n]:_IN_SPLITS[n + 1]], preferred_element_type=f32)

    def emit_class_major(n, y, post=None):
        for p_, c in enumerate(range(0, y.shape[1], LANES)):
            yp = y[:, c:c + LANES]
            stage_s[n - 3, p_] = yp if post is None else _qk_norm_rope(yp, cos_ref[...], sin_ref[...], *post)
            for r in range(MAX_DIL):
                out_refs[n][0, r, :, c:c + LANES] = stage_s[n - 3, p_, pl.ds(r, tm // MAX_DIL, stride=MAX_DIL), :]

    q = proj(3)
    k = proj(4)
    emit_class_major(3, q, (qg_ref[...], HEAD_DIM ** -0.5))
    v = proj(5)
    out_refs[0][...] = proj(0)
    emit_class_major(4, k, (kg_ref[...], 1.0))
    out_refs[1][...] = proj(1)
    emit_class_major(5, v)
    out_refs[2][...] = proj(2)


def _in_proj(x, g, w, cos_t, sin_t, qg, kg, batch, seq):
    t = x.shape[0]
    tm = PROJ_TILE
    nt = seq // tm
    widths = [c1 - c0 for c0, c1 in zip(_IN_SPLITS[:-1], _IN_SPLITS[1:])]
    table = pl.BlockSpec((tm, LANES), lambda i: (i % nt, 0))
    gain = pl.BlockSpec((1, LANES), lambda i: (0, 0))
    flat = lambda w_: (pl.BlockSpec((tm, w_), lambda i: (i, 0)), jax.ShapeDtypeStruct((t, w_), f32))
    cls_major = lambda w_: (pl.BlockSpec((1, MAX_DIL, tm // MAX_DIL, w_), lambda i: (i // nt, 0, i % nt, 0)),
                            jax.ShapeDtypeStruct((batch, MAX_DIL, seq // MAX_DIL, w_), f32))
    outs = [flat(w_) for w_ in widths[:3]] + [cls_major(w_) for w_ in widths[3:]]
    return pl.pallas_call(
        _in_proj_kernel,
        grid=(t // tm,),
        in_specs=[pl.BlockSpec((tm, D_MODEL), lambda i: (i, 0)),
                  pl.BlockSpec((1, D_MODEL), lambda i: (0, 0)),
                  pl.BlockSpec((D_MODEL, _IN_SPLITS[-1]), lambda i: (0, 0)),
                  table, table, gain, gain],
        out_specs=[o[0] for o in outs],
        out_shape=[o[1] for o in outs],
        scratch_shapes=[pltpu.VMEM((3, ATTN_WIDTH // LANES, tm, LANES), f32)],
        compiler_params=pltpu.CompilerParams(dimension_semantics=("arbitrary",),
                                             vmem_limit_bytes=VMEM_LIMIT),
        name="in_proj",
    )(x, g, w, cos_t, sin_t, qg, kg)


def _mixer_kernel(pool_ref, lru_ref, gate_ref, pw_ref, ps_ref, cw_ref, cb_ref, wri_ref, bri_ref, lam_ref,
                  ypool_ref, ylru_ref,
                  b0, b1, b2, b3, cbuf, a_s, b_s, h_s, a2_s, b2_s, e2_s, e3_s, hc, *, ts):
    j = pl.program_id(1)
    n1 = ts // 8
    n2 = n1 // 8

    @pl.when(j == 0)
    def _():
        for buf in (b0, b1, b2, b3):
            buf[pl.ds(0, POOL_HALO), :] = jnp.zeros((POOL_HALO, POOL_WIDTH), f32)
        cbuf[pl.ds(0, CONV_HALO), :] = jnp.zeros((CONV_HALO, LRU_WIDTH), f32)
        hc[...] = jnp.zeros_like(hc)

    row = lax.broadcasted_iota(i32, (ts, POOL_WIDTH), 0) + j * ts
    grp = lax.broadcasted_iota(i32, (ts, POOL_WIDTH), 1) // (POOL_WIDTH // len(POOL_WINDOWS))

    u = pool_ref[...]
    b0[pl.ds(POOL_HALO, ts), :] = u
    s1 = u + b0[pl.ds(POOL_HALO - 1, ts), :]
    b1[pl.ds(POOL_HALO, ts), :] = s1
    s2 = s1 + b1[pl.ds(POOL_HALO - 2, ts), :]
    b2[pl.ds(POOL_HALO, ts), :] = s2
    s3 = s2 + b2[pl.ds(POOL_HALO - 4, ts), :]
    b3[pl.ds(POOL_HALO, ts), :] = s3
    s4 = s3 + b3[pl.ds(POOL_HALO - 8, ts), :]
    for buf in (b0, b1, b2, b3):
        buf[pl.ds(0, POOL_HALO), :] = buf[pl.ds(ts, POOL_HALO), :]
    sums = jnp.where(grp == 0, s1, jnp.where(grp == 1, s2, jnp.where(grp == 2, s3, s4)))
    win = jnp.where(grp == 0, float(POOL_WINDOWS[0]),
                    jnp.where(grp == 1, float(POOL_WINDOWS[1]),
                              jnp.where(grp == 2, float(POOL_WINDOWS[2]), float(POOL_WINDOWS[3]))))
    cnt = jnp.minimum((row + 1).astype(f32), win)
    pooled = sums / cnt - u
    ypool_ref[...] = jnp.dot(pooled.astype(bf16), pw_ref[...], preferred_element_type=f32) * ps_ref[...]

    lu = lru_ref[...]
    cbuf[pl.ds(CONV_HALO, ts), :] = lu
    xc = (cb_ref[...] + cw_ref[3:4, :] * lu
          + cw_ref[2:3, :] * cbuf[pl.ds(CONV_HALO - 1, ts), :]
          + cw_ref[1:2, :] * cbuf[pl.ds(CONV_HALO - 2, ts), :]
          + cw_ref[0:1, :] * cbuf[pl.ds(CONV_HALO - 3, ts), :])
    cbuf[pl.ds(0, CONV_HALO), :] = cbuf[pl.ds(ts, CONV_HALO), :]
    ri = jnp.dot(xc.astype(bf16), wri_ref[...], preferred_element_type=f32) + bri_ref[...]
    r = jax.nn.sigmoid(ri[:, :LRU_WIDTH])
    ig = jax.nn.sigmoid(ri[:, LRU_WIDTH:])
    log_a = -LRU_C * r * jax.nn.softplus(-lam_ref[...])
    a = jnp.exp(log_a)
    th = jnp.tanh(log_a)
    mult = jnp.sqrt(-2.0 * th / (1.0 - th))
    mult = jnp.where(row == 0, 1.0, mult)
    bb = mult * (ig * xc)

    def rows(ref, hf, s, n):
        return ref[hf, pl.ds(s, n, stride=8), :]

    for hf in range(LRU_WIDTH // LANES):
        a_s[hf] = a[:, hf * LANES:(hf + 1) * LANES]
        b_s[hf] = bb[:, hf * LANES:(hf + 1) * LANES]
        pa, pb = rows(a_s, hf, 0, n1), rows(b_s, hf, 0, n1)
        for s in range(1, 8):
            as_ = rows(a_s, hf, s, n1)
            pb = as_ * pb + rows(b_s, hf, s, n1)
            pa = as_ * pa
        a2_s[hf] = pa
        b2_s[hf] = pb
        pa, pb = rows(a2_s, hf, 0, n2), rows(b2_s, hf, 0, n2)
        for s in range(1, 8):
            as_ = rows(a2_s, hf, s, n2)
            pb = as_ * pb + rows(b2_s, hf, s, n2)
            pa = as_ * pa
        h = hc[hf, 0:1, :]
        for k in range(8):
            e3_s[hf, k:k + 1, :] = h
            h = pa[k:k + 1, :] * h + pb[k:k + 1, :]
        hc[hf, 0:1, :] = h
        prev = e3_s[hf]
        for s in range(8):
            e2_s[hf, pl.ds(s, n2, stride=8), :] = prev
            prev = rows(a2_s, hf, s, n2) * prev + rows(b2_s, hf, s, n2)
        prev = e2_s[hf]
        for s in range(8):
            prev = rows(a_s, hf, s, n1) * prev + rows(b_s, hf, s, n1)
            h_s[hf, pl.ds(s, n1, stride=8), :] = prev

    hfull = jnp.concatenate([h_s[hf] for hf in range(LRU_WIDTH // LANES)], axis=1)
    ylru_ref[...] = hfull * jax.nn.gelu(gate_ref[...])


def _mixer(pool_u, lru_u, gate, pw, ps, cw, cb, wri, bri, lam, batch, seq):
    ts = MIX_TILE
    nt = seq // ts
    t = batch * seq
    nh = LRU_WIDTH // LANES
    tile = lambda w_: pl.BlockSpec((ts, w_), lambda b, j: (b * nt + j, 0))
    full = lambda shape: pl.BlockSpec(shape, lambda b, j: (0,) * len(shape))
    return pl.pallas_call(
        functools.partial(_mixer_kernel, ts=ts),
        grid=(batch, nt),
        in_specs=[tile(POOL_WIDTH), tile(LRU_WIDTH), tile(LRU_WIDTH),
                  full((POOL_WIDTH, POOL_WIDTH)), full((1, POOL_WIDTH)),
                  full((4, LRU_WIDTH)), full((1, LRU_WIDTH)),
                  full((LRU_WIDTH, 2 * LRU_WIDTH)), full((1, 2 * LRU_WIDTH)), full((1, LRU_WIDTH))],
        out_specs=[tile(POOL_WIDTH), tile(LRU_WIDTH)],
        out_shape=[jax.ShapeDtypeStruct((t, POOL_WIDTH), f32), jax.ShapeDtypeStruct((t, LRU_WIDTH), f32)],
        scratch_shapes=[pltpu.VMEM((ts + POOL_HALO, POOL_WIDTH), f32)] * 4
        + [pltpu.VMEM((ts + CONV_HALO, LRU_WIDTH), f32)]
        + [pltpu.VMEM((nh, ts, LANES), f32)] * 3
        + [pltpu.VMEM((nh, ts // 8, LANES), f32)] * 3
        + [pltpu.VMEM((nh, 8, LANES), f32)] * 2,
        compiler_params=pltpu.CompilerParams(dimension_semantics=("arbitrary", "arbitrary"),
                                             vmem_limit_bytes=VMEM_LIMIT),
        name="pool_lru_mixer",
    )(pool_u, lru_u, gate, pw, ps, cw, cb, wri, bri, lam)


def _attn_kernel(q_ref, k_ref, v_ref, o_ref, *state_refs, seq):
    blk = ATTN_BLOCK
    cls = seq // MAX_DIL
    qs, ks, vs = q_ref.at[0], k_ref.at[0], v_ref.at[0]

    h0 = lax.broadcasted_iota(i32, (blk, LANES), 1) < HEAD_DIM
    q_h0 = _pair_head0((blk, LANES))
    states = [state_refs[3 * g:3 * g + 3] for g in range(len(DILATIONS))]
    qi = lax.broadcasted_iota(i32, (2 * blk, 2 * blk), 0) % blk
    kj = lax.broadcasted_iota(i32, (2 * blk, 2 * blk), 1)

    for bi, d in enumerate(DILATIONS):
        nb = seq // (blk * d)
        runs = MAX_DIL // d
        run = blk // runs

        def member(i, runs=runs, run=run):
            return runs * (i % run) + i // run

        jq = member(qi)
        jk = member(kj % blk)
        valid_first = member(lax.broadcasted_iota(i32, (2 * blk, blk), 1)) <= member(lax.broadcasted_iota(i32, (2 * blk, blk), 0) % blk)
        valid_later = ((kj >= blk) & (jk <= jq)) | ((kj < blk) & (jk >= jq))

        def rows(res, n, d=d, runs=runs, run=run):
            return [(res + d * b, run * n) for b in range(runs)]

        def ld(ref, where, run=run):
            return jnp.concatenate([ref[c, r0:r0 + run, :] for c, r0 in where], axis=0)

        def score(idx, nb=nb, rows=rows, ld=ld):
            res, n = divmod(idx, nb)
            cur = rows(res, n)
            keys = cur if n == 0 else rows(res, n - 1) + cur
            qb = ld(qs, cur)
            q2 = jnp.concatenate([jnp.where(q_h0, qb, 0.0), jnp.where(q_h0, 0.0, qb)], axis=0).astype(bf16)
            return lax.dot_general(q2, ld(ks, keys).astype(bf16), (((1,), (1,)), ((), ())),
                                   preferred_element_type=f32)

        def finish(idx, s, nb=nb, rows=rows, ld=ld, run=run, bi=bi, valid_first=valid_first, valid_later=valid_later):
            res, n = divmod(idx, nb)
            cur = rows(res, n)
            keys = cur if n == 0 else rows(res, n - 1) + cur
            s = jnp.where(valid_first if n == 0 else valid_later, s, NEG_INF)
            mb = jnp.max(s, axis=-1, keepdims=True)
            p = jnp.exp(s - mb).astype(bf16)
            vb = ld(vs, keys).astype(bf16)
            c = jnp.dot(p, jnp.concatenate([vb, jnp.ones_like(vb)], axis=1), preferred_element_type=f32)
            vals = (jnp.where(h0, mb[:blk], mb[blk:]), jnp.where(h0, c[:blk, LANES:], c[blk:, LANES:]),
                    jnp.where(h0, c[:blk, :LANES], c[blk:, :LANES]))
            for ref, val in zip(states[bi], vals):
                for b_, (cls_i, r0) in enumerate(cur):
                    ref[cls_i, r0:r0 + run, :] = val[b_ * run:(b_ + 1) * run, :]

        n_blocks = seq // blk
        scores = [score(idx) for idx in range(ATTN_SCORES_AHEAD)]
        for idx in range(n_blocks):
            if idx + ATTN_SCORES_AHEAD < n_blocks:
                scores.append(score(idx + ATTN_SCORES_AHEAD))
            finish(idx, scores[idx])
            scores[idx] = None

    for r in range(MAX_DIL):
        ms = [st_[0][r] for st_ in states]
        m = functools.reduce(jnp.maximum, ms)
        num = den = None
        for (m_ref, l_ref, a_ref), m_g in zip(states, ms):
            w = jnp.exp(m_g - m)
            num = w * a_ref[r] if num is None else num + w * a_ref[r]
            den = w * l_ref[r] if den is None else den + w * l_ref[r]
        o_ref[0, r] = num / den


def _attention(q, k, v, batch, seq):
    npair = ATTN_WIDTH // LANES
    cls = seq // MAX_DIL
    tile = pl.BlockSpec((1, MAX_DIL, cls, LANES), lambda b, p: (b, 0, 0, p))
    return pl.pallas_call(
        functools.partial(_attn_kernel, seq=seq),
        grid=(batch, npair),
        in_specs=[tile, tile, tile],
        out_specs=tile,
        out_shape=jax.ShapeDtypeStruct((batch, MAX_DIL, cls, ATTN_WIDTH), f32),
        scratch_shapes=[pltpu.VMEM((MAX_DIL, cls, LANES), f32)] * (3 * len(DILATIONS)),
        compiler_params=pltpu.CompilerParams(dimension_semantics=("arbitrary", "arbitrary"),
                                             vmem_limit_bytes=VMEM_LIMIT),
        name="dilated_attention",
    )(q, k, v)


def _out_proj_kernel(x_ref, yp_ref, yl_ref, ya_ref, og_ref, wo_ref, g2_ref, rw_ref, rb_ref,
                     xo_ref, idx_ref, gate_ref, rank_ref, cnt_ref, cnt_s, before_s, ya_s, *, tm, steps_per_tile):
    i = pl.program_id(0)
    c0, c1 = POOL_WIDTH, POOL_WIDTH + LRU_WIDTH
    for r in range(MAX_DIL):
        for p_ in range(ATTN_WIDTH // LANES):
            ya_s[p_, pl.ds(r, tm // MAX_DIL, stride=MAX_DIL), :] = ya_ref[0, r, :, p_ * LANES:(p_ + 1) * LANES]
    ya = jnp.concatenate([ya_s[p_] for p_ in range(ATTN_WIDTH // LANES)], axis=1)
    acc = x_ref[...]
    for y, lo, hi in ((yp_ref[...], 0, c0), (yl_ref[...], c0, c1), (ya, c1, D_MODEL)):
        mix = _rms(y, og_ref[:, lo:hi]).astype(bf16)
        acc = acc + jnp.dot(mix, wo_ref[lo:hi, :], preferred_element_type=f32)
    xo_ref[...] = acc

    h2 = _rms(acc, g2_ref[...])
    h_hi = h2.astype(bf16)
    h_lo = (h2 - h_hi.astype(f32)).astype(bf16)
    nt = (((1,), (1,)), ((), ()))
    both = lax.dot_general(rw_ref[...], h_hi, nt, preferred_element_type=f32)
    logits = (both[:N_EXPERTS] + both[N_EXPERTS:]
              + lax.dot_general(rw_ref[:N_EXPERTS, :], h_lo, nt, preferred_element_type=f32)
              + rb_ref[...])
    eio = lax.broadcasted_iota(i32, (N_EXPERTS, tm), 0)
    cur = logits
    vals, idxs = [], []
    for _ in range(TOP_K):
        mx = jnp.max(cur, axis=0, keepdims=True)
        ix = jnp.min(jnp.where(cur == mx, eio, N_EXPERTS), axis=0, keepdims=True)
        vals.append(mx)
        idxs.append(ix)
        cur = jnp.where(eio == ix, -jnp.inf, cur)
    ex = [jnp.exp(v - vals[0]) for v in vals]
    den = ex[0] + ex[1] + ex[2] + ex[3]

    @pl.when(i % steps_per_tile == 0)
    def _():
        cnt_s[...] = jnp.zeros_like(cnt_s)

    onehot = jnp.zeros((N_EXPERTS, tm), f32)
    for ix in idxs:
        onehot = onehot + (eio == ix).astype(f32)
    @pl.when(i == 0)
    def _():
        before_s[...] = (lax.broadcasted_iota(i32, (tm, tm), 0) < lax.broadcasted_iota(i32, (tm, tm), 1)).astype(bf16)

    prefix = jnp.dot(onehot.astype(bf16), before_s[...], preferred_element_type=f32) + cnt_s[:, 0:1]
    for k in range(TOP_K):
        idx_ref[k:k + 1, :] = idxs[k]
        gate_ref[k:k + 1, :] = ex[k] / den
        rank_ref[k:k + 1, :] = jnp.sum(jnp.where(eio == idxs[k], prefix, 0.0), axis=0, keepdims=True).astype(i32)
    cnt_s[...] = cnt_s[...] + jnp.sum(onehot, axis=1, keepdims=True)
    cnt_ref[0] = cnt_s[...]


def _out_proj(x, yp, yl, ya, og, wo, g2, rw_t, rb, seq):
    t = x.shape[0]
    tm = PROJ_TILE
    steps = t // tm
    nt = seq // tm
    tile = lambda w_: pl.BlockSpec((tm, w_), lambda i: (i, 0))
    full = lambda shape: pl.BlockSpec(shape, lambda i: (0,) * len(shape))
    route = pl.BlockSpec((TOP_K, tm), lambda i: (0, i))
    return pl.pallas_call(
        functools.partial(_out_proj_kernel, tm=tm, steps_per_tile=MOE_TILE // tm),
        grid=(steps,),
        in_specs=[tile(D_MODEL), tile(POOL_WIDTH), tile(LRU_WIDTH),
                  pl.BlockSpec((1, MAX_DIL, tm // MAX_DIL, ATTN_WIDTH), lambda i: (i // nt, 0, i % nt, 0)),
                  full((1, D_MODEL)), full((D_MODEL, D_MODEL)), full((1, D_MODEL)),
                  full((2 * N_EXPERTS, D_MODEL)), full((N_EXPERTS, 1))],
        out_specs=[tile(D_MODEL), route, route, route,
                   pl.BlockSpec((1, N_EXPERTS, LANES), lambda i: (i, 0, 0))],
        out_shape=[jax.ShapeDtypeStruct((t, D_MODEL), f32),
                   jax.ShapeDtypeStruct((TOP_K, t), i32),
                   jax.ShapeDtypeStruct((TOP_K, t), f32),
                   jax.ShapeDtypeStruct((TOP_K, t), i32),
                   jax.ShapeDtypeStruct((steps, N_EXPERTS, LANES), f32)],
        scratch_shapes=[pltpu.VMEM((N_EXPERTS, LANES), f32), pltpu.VMEM((tm, tm), bf16),
                        pltpu.VMEM((ATTN_WIDTH // LANES, tm, LANES), f32)],
        compiler_params=pltpu.CompilerParams(dimension_semantics=("arbitrary",),
                                             vmem_limit_bytes=VMEM_LIMIT),
        name="out_proj_router",
    )(x, yp, yl, ya, og, wo, g2, rw_t, rb)


ROW_SLAB = 8
ROUTE_GROUP = 8
MOE_EXPERTS_PER_STEP = 2


def _slab_cols(ref, first_row, n_rows, col):
    return ref[pl.ds(first_row * ROW_SLAB + col, n_rows, stride=ROW_SLAB), :]


def _moe_kernel(off_ref, pos_ref, gate_ref, x_ref, g2_ref, wgu_ref, bgu_ref, wd_ref, bd_ref,
                o_ref, sorted_s, stage_s, *, tt, ch, io_rows):
    i = pl.program_id(0)
    s = pl.program_id(1)
    n_io = tt // io_rows
    n_exp_steps = N_EXPERTS // MOE_EXPERTS_PER_STEP
    base = i * (N_EXPERTS + 1)
    n_col = D_MODEL // LANES

    def slab(ref, first_sublane):
        return ref.at[pl.ds(pl.multiple_of(first_sublane, ROW_SLAB), ROW_SLAB), :]

    @pl.when(s < n_io)
    def _dispatch():
        @pl.when(s == 0)
        def _():
            sorted_s[pl.ds(TOP_K * tt * ROW_SLAB, ch * ROW_SLAB), :] = jnp.zeros((ch * ROW_SLAB, LANES), f32)

        h = _rms(x_ref[...], g2_ref[...])
        for j in range(n_col):
            stage_s[pl.ds(j, io_rows, stride=ROW_SLAB), :] = h[:, j * LANES:(j + 1) * LANES]

        for c in range(io_rows // ROUTE_GROUP):
            for u_ in range(ROUTE_GROUP):
                row = stage_s[(c * ROUTE_GROUP + u_) * ROW_SLAB:(c * ROUTE_GROUP + u_ + 1) * ROW_SLAB, :]
                for k in range(TOP_K):
                    slab(sorted_s, pos_ref[k, c * ROUTE_GROUP + u_])[...] = row

    @pl.when((s >= n_io) & (s < n_io + n_exp_steps))
    def _experts():
        def run_chunk(sub, start, hi, rows):
            xc = jnp.concatenate([_slab_cols(sorted_s, start, rows, j) for j in range(n_col)], axis=1)
            gu = jnp.dot(xc.astype(bf16), wgu_ref[sub], preferred_element_type=f32) + bgu_ref[sub]
            gate = jnp.minimum(gu[:, :EXPERT_FF], SWIGLU_LIMIT)
            up = jnp.clip(gu[:, EXPERT_FF:], -SWIGLU_LIMIT, SWIGLU_LIMIT)
            act = (up + 1.0) * gate * jax.nn.sigmoid(SWIGLU_ALPHA * gate)
            y = jnp.dot(act.astype(bf16), wd_ref[sub], preferred_element_type=f32) + bd_ref[sub]
            mine = (lax.broadcasted_iota(i32, (rows, 1), 0) + start) < hi
            y = jnp.where(mine, y, xc)
            for j in range(n_col):
                sorted_s[pl.ds(start * ROW_SLAB + j, rows, stride=ROW_SLAB), :] = y[:, j * LANES:(j + 1) * LANES]

        def expert(sub, carry):
            e = (s - n_io) * MOE_EXPERTS_PER_STEP + sub
            lo = off_ref[base + e]
            hi = off_ref[base + e + 1]
            n_full = jnp.maximum(hi - lo - 1, 0) // ch

            def full(c, carry2):
                run_chunk(sub, lo + c * ch, hi, ch)
                return carry2

            lax.fori_loop(0, n_full, full, 0)
            start = lo + n_full * ch
            rest = hi - start
            for size_lo, size in zip((0,) + MOE_TAIL_SIZES[:-1], MOE_TAIL_SIZES):
                @pl.when((rest > size_lo) & (rest <= size))
                def _(size=size):
                    run_chunk(sub, start, hi, size)
            return carry

        lax.fori_loop(0, MOE_EXPERTS_PER_STEP, expert, 0)

    @pl.when(s >= n_io + n_exp_steps)
    def _combine():
        for c in range(io_rows // ROUTE_GROUP):
            for u_ in range(ROUTE_GROUP):
                acc = gate_ref[0, c * ROUTE_GROUP + u_] * slab(sorted_s, pos_ref[0, c * ROUTE_GROUP + u_])[...]
                for k in range(1, TOP_K):
                    acc = acc + gate_ref[k, c * ROUTE_GROUP + u_] * slab(sorted_s, pos_ref[k, c * ROUTE_GROUP + u_])[...]
                stage_s[(c * ROUTE_GROUP + u_) * ROW_SLAB:(c * ROUTE_GROUP + u_ + 1) * ROW_SLAB, :] = acc
        moe = jnp.concatenate([_slab_cols(stage_s, 0, io_rows, j) for j in range(n_col)], axis=1)
        o_ref[...] = x_ref[...] + moe


def _moe(x, pos, gates, offsets, g2, wgu, bgu, wd, bd):
    t = x.shape[0]
    tt = MOE_TILE
    ch = MOE_CHUNK
    io_rows = PROJ_TILE
    n_io = tt // io_rows
    eps = MOE_EXPERTS_PER_STEP
    n_exp_steps = N_EXPERTS // eps
    n_steps = 2 * n_io + n_exp_steps

    def tok_idx(i, s, off):
        in_combine = s >= n_io + n_exp_steps
        return i * n_io + jnp.where(s < n_io, s, jnp.where(in_combine, s - n_io - n_exp_steps, n_io - 1))

    def out_idx(i, s, off):
        return i * n_io + jnp.maximum(s - n_io - n_exp_steps, 0)

    def expert_idx(i, s, off):
        return jnp.clip(s - n_io, 0, n_exp_steps - 1)

    route = pl.BlockSpec((TOP_K, io_rows), lambda i, s, off: (0, tok_idx(i, s, off)), memory_space=pltpu.SMEM)
    return pl.pallas_call(
        functools.partial(_moe_kernel, tt=tt, ch=ch, io_rows=io_rows),
        grid_spec=pltpu.PrefetchScalarGridSpec(
            num_scalar_prefetch=1,
            grid=(t // tt, n_steps),
            in_specs=[route, route,
                      pl.BlockSpec((io_rows, D_MODEL), lambda i, s, off: (tok_idx(i, s, off), 0)),
                      pl.BlockSpec((1, D_MODEL), lambda i, s, off: (0, 0)),
                      pl.BlockSpec((eps, D_MODEL, 2 * EXPERT_FF), lambda i, s, off: (expert_idx(i, s, off), 0, 0)),
                      pl.BlockSpec((eps, 1, 2 * EXPERT_FF), lambda i, s, off: (expert_idx(i, s, off), 0, 0)),
                      pl.BlockSpec((eps, EXPERT_FF, D_MODEL), lambda i, s, off: (expert_idx(i, s, off), 0, 0)),
                      pl.BlockSpec((eps, 1, D_MODEL), lambda i, s, off: (expert_idx(i, s, off), 0, 0))],
            out_specs=pl.BlockSpec((io_rows, D_MODEL), lambda i, s, off: (out_idx(i, s, off), 0)),
            scratch_shapes=[pltpu.VMEM(((TOP_K * tt + ch) * ROW_SLAB, LANES), f32),
                            pltpu.VMEM((io_rows * ROW_SLAB, LANES), f32)]),
        out_shape=jax.ShapeDtypeStruct((t, D_MODEL), f32),
        compiler_params=pltpu.CompilerParams(dimension_semantics=("arbitrary", "arbitrary"),
                                             vmem_limit_bytes=VMEM_LIMIT),
        name="moe_experts",
    )(offsets, pos, gates, x, g2, wgu, bgu, wd, bd)


def _pair_layout(v):
    first, second = v[..., :QK_QUARTER], v[..., QK_QUARTER:]
    return jnp.concatenate([first, first, second, second], axis=-1)


def _qk_column_order():
    pair = jnp.arange(LANES).reshape(2, 2, QK_QUARTER).transpose(1, 0, 2).reshape(-1)
    cols = jnp.arange(_IN_SPLITS[-1])
    qk = cols[_IN_SPLITS[3]:_IN_SPLITS[5]].reshape(-1, LANES)[:, pair].reshape(-1)
    return jnp.concatenate([cols[:_IN_SPLITS[3]], qk, cols[_IN_SPLITS[5]:]])


def _rope_tables(seq):
    pos = jnp.arange(seq, dtype=f32)
    inv_freq = ROPE_THETA ** (-jnp.arange(0, HEAD_DIM, 2, dtype=f32) / HEAD_DIM)
    ang = pos[:, None] * inv_freq[None, :]
    cos, sin = jnp.cos(ang), jnp.sin(ang)
    return _pair_layout(jnp.concatenate([cos, cos], axis=1)), _pair_layout(jnp.concatenate([-sin, sin], axis=1))


def _layer(x, cos_t, sin_t, batch, seq, p):
    t = batch * seq
    row = lambda v: v.reshape(1, -1)
    pool_u, lru_u, gate, q, k, v = _in_proj(x, row(p["norm1_g"]), p["w_in"][:, _qk_column_order()].astype(bf16),
                                            cos_t, sin_t, row(_pair_layout(p["q_norm_g"])),
                                            row(_pair_layout(p["k_norm_g"])), batch, seq)
    wri = jnp.concatenate([block_diag(*p["lru_wa"]), block_diag(*p["lru_wx"])], axis=1).astype(bf16)
    bri = jnp.concatenate([p["lru_ba"], p["lru_bx"]]).reshape(1, -1)
    y_pool, y_lru = _mixer(pool_u, lru_u, gate, block_diag(*p["pool_w"]).astype(bf16), row(p["pool_scale"]),
                           p["conv_w"], row(p["conv_b"]), wri, bri, row(p["lru_lambda"]), batch, seq)
    y_attn = _attention(q, k, v, batch, seq)
    rw_hi = p["router_w"].T.astype(bf16)
    rw_split = jnp.concatenate([rw_hi, (p["router_w"].T - rw_hi.astype(f32)).astype(bf16)], axis=0)
    x, idx, gates, rank, cnt = _out_proj(x, y_pool, y_lru, y_attn, row(p["out_norm_g"]), p["w_out"].astype(bf16),
                                         row(p["norm2_g"]), rw_split, p["router_b"].reshape(-1, 1), seq)
    steps_per_tile = MOE_TILE // PROJ_TILE
    n_tiles = t // MOE_TILE
    counts = cnt[steps_per_tile - 1::steps_per_tile, :, 0].astype(i32)
    offsets = jnp.concatenate([jnp.zeros((n_tiles, 1), i32), jnp.cumsum(counts, axis=1)], axis=1)
    idx_t = idx.reshape(TOP_K, n_tiles, MOE_TILE)
    seg_start = jnp.zeros_like(idx_t)
    for e in range(N_EXPERTS):
        seg_start = jnp.where(idx_t == e, offsets[None, :, e, None], seg_start)
    pos = (rank + seg_start.reshape(TOP_K, t)) * ROW_SLAB
    return _moe(x, pos, gates, offsets.reshape(-1), row(p["norm2_g"]),
                p["w_gate_up"].astype(bf16), p["b_gate_up"][:, None, :],
                p["w_down"].astype(bf16), p["b_down"][:, None, :])


def kernel(x, norm1_g, w_in, pool_w, pool_scale, conv_w, conv_b, lru_wa, lru_ba, lru_wx, lru_bx, lru_lambda,
           q_norm_g, k_norm_g, out_norm_g, w_out, norm2_g, router_w, router_b, w_gate_up, b_gate_up,
           w_down, b_down):
    batch, seq, d = x.shape
    params = dict(norm1_g=norm1_g, w_in=w_in, pool_w=pool_w, pool_scale=pool_scale, conv_w=conv_w, conv_b=conv_b,
                  lru_wa=lru_wa, lru_ba=lru_ba, lru_wx=lru_wx, lru_bx=lru_bx, lru_lambda=lru_lambda,
                  q_norm_g=q_norm_g, k_norm_g=k_norm_g, out_norm_g=out_norm_g, w_out=w_out, norm2_g=norm2_g,
                  router_w=router_w, router_b=router_b, w_gate_up=w_gate_up, b_gate_up=b_gate_up,
                  w_down=w_down, b_down=b_down)
    cos_t, sin_t = _rope_tables(seq)
    xt = x.reshape(batch * seq, d)
    for layer in range(norm1_g.shape[0]):
        xt = _layer(xt, cos_t, sin_t, batch, seq, {name: val[layer] for name, val in params.items()})
    return xt.reshape(batch, seq, d)
```

```python
import functools

import jax
import jax.numpy as jnp
from jax import lax
from jax.experimental import pallas as pl
from jax.experimental.pallas import tpu as pltpu
from jax.scipy.linalg import block_diag

f32 = jnp.float32
bf16 = jnp.bfloat16
i32 = jnp.int32

D_MODEL = 1024
POOL_WIDTH = 256
POOL_WINDOWS = (2, 4, 8, 16)
LRU_WIDTH = 256
LRU_C = 8.0
ATTN_WIDTH = 512
HEAD_DIM = 64
DILATIONS = (1, 4, 16)
MAX_DIL = DILATIONS[-1]
ATTN_BLOCK = 128
ROPE_THETA = 10000.0
N_EXPERTS = 32
TOP_K = 4
EXPERT_FF = 256
SWIGLU_LIMIT = 7.0
SWIGLU_ALPHA = 1.702
NORM_EPS = 1e-6
NEG_INF = -1e30

LANES = 128
POOL_HALO = 16
CONV_HALO = 8
VMEM_LIMIT = 56 * 1024 * 1024

PROJ_TILE = 512
MIX_TILE = 512
MOE_TILE = 2048
MOE_CHUNK = 512
MOE_TAIL_SIZES = (64, 128, 192, 256, 320, 384, 448, MOE_CHUNK)
ATTN_SCORES_AHEAD = 3


def _rms(x, g):
    return x * lax.rsqrt(jnp.mean(x * x, axis=-1, keepdims=True) + NORM_EPS) * g


_IN_SPLITS = (0, 256, 512, 768, 1280, 1792, 2304)


QK_QUARTER = HEAD_DIM // 2


def _pair_head0(shape):
    return (lax.broadcasted_iota(i32, shape, 1) % HEAD_DIM) < QK_QUARTER


def _qk_norm_rope(y, cs, sn, gain, scale):
    head0 = _pair_head0(y.shape)
    yy = y * y
    ms0 = jnp.sum(jnp.where(head0, yy, 0.0), axis=-1, keepdims=True)
    ms1 = jnp.sum(jnp.where(head0, 0.0, yy), axis=-1, keepdims=True)
    ms = jnp.where(head0, ms0, ms1) * (1.0 / HEAD_DIM)
    yn = y * lax.rsqrt(ms + NORM_EPS) * gain
    return (yn * cs + pltpu.roll(yn, HEAD_DIM, 1) * sn) * scale


def _in_proj_kernel(x_ref, g_ref, w_ref, cos_ref, sin_ref, qg_ref, kg_ref, *refs):
    out_refs, stage_s = refs[:-1], refs[-1]
    tm = x_ref.shape[0]
    h = _rms(x_ref[...], g_ref[...]).astype(bf16)

    def proj(n):
        return jnp.dot(h, w_ref[:, _IN_SPLITS[n]:_IN_SPLITS[n + 1]], preferred_element_type=f32)

    def emit_class_major(n, y, post=None):
        for p_, c in enumerate(range(0, y.shape[1], LANES)):
            yp = y[:, c:c + LANES]
            stage_s[n - 3, p_] = yp if post is None else _qk_norm_rope(yp, cos_ref[...], sin_ref[...], *post)
            for r in range(MAX_DIL):
                out_refs[n][0, r, :, c:c + LANES] = stage_s[n - 3, p_, pl.ds(r, tm // MAX_DIL, stride=MAX_DIL), :]

    q = proj(3)
    k = proj(4)
    emit_class_major(3, q, (qg_ref[...], HEAD_DIM ** -0.5))
    v = proj(5)
    out_refs[0][...] = proj(0)
    emit_class_major(4, k, (kg_ref[...], 1.0))
    out_refs[1][...] = proj(1)
    emit_class_major(5, v)
    out_refs[2][...] = proj(2)


def _in_proj(x, g, w, cos_t, sin_t, qg, kg, batch, seq):
    t = x.shape[0]
    tm = PROJ_TILE
    nt = seq // tm
    widths = [c1 - c0 for c0, c1 in zip(_IN_SPLITS[:-1], _IN_SPLITS[1:])]
    table = pl.BlockSpec((tm, LANES), lambda i: (i % nt, 0))
    gain = pl.BlockSpec((1, LANES), lambda i: (0, 0))
    flat = lambda w_: (pl.BlockSpec((tm, w_), lambda i: (i, 0)), jax.ShapeDtypeStruct((t, w_), f32))
    cls_major = lambda w_: (pl.BlockSpec((1, MAX_DIL, tm // MAX_DIL, w_), lambda i: (i // nt, 0, i % nt, 0)),
                            jax.ShapeDtypeStruct((batch, MAX_DIL, seq // MAX_DIL, w_), f32))
    outs = [flat(w_) for w_ in widths[:3]] + [cls_major(w_) for w_ in widths[3:]]
    return pl.pallas_call(
        _in_proj_kernel,
        grid=(t // tm,),
        in_specs=[pl.BlockSpec((tm, D_MODEL), lambda i: (i, 0)),
                  pl.BlockSpec((1, D_MODEL), lambda i: (0, 0)),
                  pl.BlockSpec((D_MODEL, _IN_SPLITS[-1]), lambda i: (0, 0)),
                  table, table, gain, gain],
        out_specs=[o[0] for o in outs],
        out_shape=[o[1] for o in outs],
        scratch_shapes=[pltpu.VMEM((3, ATTN_WIDTH // LANES, tm, LANES), f32)],
        compiler_params=pltpu.CompilerParams(dimension_semantics=("arbitrary",),
                                             vmem_limit_bytes=VMEM_LIMIT),
        name="in_proj",
    )(x, g, w, cos_t, sin_t, qg, kg)


def _mixer_kernel(pool_ref, lru_ref, gate_ref, pw_ref, ps_ref, cw_ref, cb_ref, wri_ref, bri_ref, lam_ref,
                  ypool_ref, ylru_ref,
                  b0, b1, b2, b3, cbuf, a_s, b_s, h_s, a2_s, b2_s, e2_s, e3_s, hc, *, ts):
    j = pl.program_id(1)
    n1 = ts // 8
    n2 = n1 // 8

    @pl.when(j == 0)
    def _():
        for buf in (b0, b1, b2, b3):
            buf[pl.ds(0, POOL_HALO), :] = jnp.zeros((POOL_HALO, buf.shape[1]), f32)
        cbuf[pl.ds(0, CONV_HALO), :] = jnp.zeros((CONV_HALO, LRU_WIDTH), f32)
        hc[...] = jnp.zeros_like(hc)

    row = lax.broadcasted_iota(i32, (ts, POOL_WIDTH), 0) + j * ts
    grp = lax.broadcasted_iota(i32, (ts, POOL_WIDTH), 1) // (POOL_WIDTH // len(POOL_WINDOWS))

    u = pool_ref[...]
    b0[pl.ds(POOL_HALO, ts), :] = u
    s1 = u + b0[pl.ds(POOL_HALO - 1, ts), :]
    b1[pl.ds(POOL_HALO, ts), :] = s1
    s2 = s1 + b1[pl.ds(POOL_HALO - 2, ts), :]
    s2_up = s2[:, LANES:]
    b2[pl.ds(POOL_HALO, ts), :] = s2_up
    s3 = s2_up + b2[pl.ds(POOL_HALO - 4, ts), :]
    b3[pl.ds(POOL_HALO, ts), :] = s3
    s4 = s3 + b3[pl.ds(POOL_HALO - 8, ts), :]
    for buf in (b0, b1, b2, b3):
        buf[pl.ds(0, POOL_HALO), :] = buf[pl.ds(ts, POOL_HALO), :]
    first_grp = lax.broadcasted_iota(i32, (ts, LANES), 1) < POOL_WIDTH // len(POOL_WINDOWS)
    sums = jnp.concatenate([jnp.where(first_grp, s1[:, :LANES], s2[:, :LANES]),
                            jnp.where(first_grp, s3, s4)], axis=1)
    win = jnp.where(grp == 0, float(POOL_WINDOWS[0]),
                    jnp.where(grp == 1, float(POOL_WINDOWS[1]),
                              jnp.where(grp == 2, float(POOL_WINDOWS[2]), float(POOL_WINDOWS[3]))))
    cnt = jnp.minimum((row + 1).astype(f32), win)
    pooled = sums / cnt - u
    ypool_ref[...] = jnp.dot(pooled.astype(bf16), pw_ref[...], preferred_element_type=f32) * ps_ref[...]

    lu = lru_ref[...]
    cbuf[pl.ds(CONV_HALO, ts), :] = lu
    xc = (cb_ref[...] + cw_ref[3:4, :] * lu
          + cw_ref[2:3, :] * cbuf[pl.ds(CONV_HALO - 1, ts), :]
          + cw_ref[1:2, :] * cbuf[pl.ds(CONV_HALO - 2, ts), :]
          + cw_ref[0:1, :] * cbuf[pl.ds(CONV_HALO - 3, ts), :])
    cbuf[pl.ds(0, CONV_HALO), :] = cbuf[pl.ds(ts, CONV_HALO), :]
    ri = jnp.dot(xc.astype(bf16), wri_ref[...], preferred_element_type=f32) + bri_ref[...]
    r = jax.nn.sigmoid(ri[:, :LRU_WIDTH])
    ig = jax.nn.sigmoid(ri[:, LRU_WIDTH:])
    log_a = -LRU_C * r * jax.nn.softplus(-lam_ref[...])
    a = jnp.exp(log_a)
    th = jnp.tanh(log_a)
    mult = jnp.sqrt(-2.0 * th / (1.0 - th))
    mult = jnp.where(row == 0, 1.0, mult)
    bb = mult * (ig * xc)

    def rows(ref, hf, s, n):
        return ref[hf, pl.ds(s, n, stride=8), :]

    for hf in range(LRU_WIDTH // LANES):
        a_s[hf] = a[:, hf * LANES:(hf + 1) * LANES]
        b_s[hf] = bb[:, hf * LANES:(hf + 1) * LANES]
        pa, pb = rows(a_s, hf, 0, n1), rows(b_s, hf, 0, n1)
        for s in range(1, 8):
            as_ = rows(a_s, hf, s, n1)
            pb = as_ * pb + rows(b_s, hf, s, n1)
            pa = as_ * pa
        a2_s[hf] = pa
        b2_s[hf] = pb
        pa, pb = rows(a2_s, hf, 0, n2), rows(b2_s, hf, 0, n2)
        for s in range(1, 8):
            as_ = rows(a2_s, hf, s, n2)
            pb = as_ * pb + rows(b2_s, hf, s, n2)
            pa = as_ * pa
        h = hc[hf, 0:1, :]
        for k in range(8):
            e3_s[hf, k:k + 1, :] = h
            h = pa[k:k + 1, :] * h + pb[k:k + 1, :]
        hc[hf, 0:1, :] = h
        prev = e3_s[hf]
        for s in range(8):
            e2_s[hf, pl.ds(s, n2, stride=8), :] = prev
            prev = rows(a2_s, hf, s, n2) * prev + rows(b2_s, hf, s, n2)
        prev = e2_s[hf]
        for s in range(8):
            prev = rows(a_s, hf, s, n1) * prev + rows(b_s, hf, s, n1)
            h_s[hf, pl.ds(s, n1, stride=8), :] = prev

    hfull = jnp.concatenate([h_s[hf] for hf in range(LRU_WIDTH // LANES)], axis=1)
    ylru_ref[...] = hfull * jax.nn.gelu(gate_ref[...])


def _mixer(pool_u, lru_u, gate, pw, ps, cw, cb, wri, bri, lam, batch, seq):
    ts = MIX_TILE
    nt = seq // ts
    t = batch * seq
    nh = LRU_WIDTH // LANES
    tile = lambda w_: pl.BlockSpec((ts, w_), lambda b, j: (b * nt + j, 0))
    full = lambda shape: pl.BlockSpec(shape, lambda b, j: (0,) * len(shape))
    return pl.pallas_call(
        functools.partial(_mixer_kernel, ts=ts),
        grid=(batch, nt),
        in_specs=[tile(POOL_WIDTH), tile(LRU_WIDTH), tile(LRU_WIDTH),
                  full((POOL_WIDTH, POOL_WIDTH)), full((1, POOL_WIDTH)),
                  full((4, LRU_WIDTH)), full((1, LRU_WIDTH)),
                  full((LRU_WIDTH, 2 * LRU_WIDTH)), full((1, 2 * LRU_WIDTH)), full((1, LRU_WIDTH))],
        out_specs=[tile(POOL_WIDTH), tile(LRU_WIDTH)],
        out_shape=[jax.ShapeDtypeStruct((t, POOL_WIDTH), f32), jax.ShapeDtypeStruct((t, LRU_WIDTH), f32)],
        scratch_shapes=[pltpu.VMEM((ts + POOL_HALO, POOL_WIDTH), f32)] * 2
        + [pltpu.VMEM((ts + POOL_HALO, LANES), f32)] * 2
        + [pltpu.VMEM((ts + CONV_HALO, LRU_WIDTH), f32)]
        + [pltpu.VMEM((nh, ts, LANES), f32)] * 3
        + [pltpu.VMEM((nh, ts // 8, LANES), f32)] * 3
        + [pltpu.VMEM((nh, 8, LANES), f32)] * 2,
        compiler_params=pltpu.CompilerParams(dimension_semantics=("arbitrary", "arbitrary"),
                                             vmem_limit_bytes=VMEM_LIMIT),
        name="pool_lru_mixer",
    )(pool_u, lru_u, gate, pw, ps, cw, cb, wri, bri, lam)


def _attn_kernel(q_ref, k_ref, v_ref, o_ref, *state_refs, seq):
    blk = ATTN_BLOCK
    cls = seq // MAX_DIL
    qs, ks, vs = q_ref.at[0], k_ref.at[0], v_ref.at[0]

    h0 = lax.broadcasted_iota(i32, (blk, LANES), 1) < HEAD_DIM
    q_h0 = _pair_head0((blk, LANES))
    states = [state_refs[3 * g:3 * g + 3] for g in range(len(DILATIONS))]
    qi = lax.broadcasted_iota(i32, (2 * blk, 2 * blk), 0) % blk
    kj = lax.broadcasted_iota(i32, (2 * blk, 2 * blk), 1)

    for bi, d in enumerate(DILATIONS):
        nb = seq // (blk * d)
        runs = MAX_DIL // d
        run = blk // runs

        def member(i, runs=runs, run=run):
            return runs * (i % run) + i // run

        jq = member(qi)
        jk = member(kj % blk)
        valid_first = member(lax.broadcasted_iota(i32, (2 * blk, blk), 1)) <= member(lax.broadcasted_iota(i32, (2 * blk, blk), 0) % blk)
        valid_later = ((kj >= blk) & (jk <= jq)) | ((kj < blk) & (jk >= jq))

        def rows(res, n, d=d, runs=runs, run=run):
            return [(res + d * b, run * n) for b in range(runs)]

        def ld(ref, where, run=run):
            return jnp.concatenate([ref[c, r0:r0 + run, :] for c, r0 in where], axis=0)

        def score(idx, nb=nb, rows=rows, ld=ld):
            res, n = divmod(idx, nb)
            cur = rows(res, n)
            keys = cur if n == 0 else rows(res, n - 1) + cur
            qb = ld(qs, cur)
            q2 = jnp.concatenate([jnp.where(q_h0, qb, 0.0), jnp.where(q_h0, 0.0, qb)], axis=0).astype(bf16)
            return lax.dot_general(q2, ld(ks, keys).astype(bf16), (((1,), (1,)), ((), ())),
                                   preferred_element_type=f32)

        def finish(idx, s, nb=nb, rows=rows, ld=ld, run=run, bi=bi, valid_first=valid_first, valid_later=valid_later):
            res, n = divmod(idx, nb)
            cur = rows(res, n)
            keys = cur if n == 0 else rows(res, n - 1) + cur
            s = jnp.where(valid_first if n == 0 else valid_later, s, NEG_INF)
            mb = jnp.max(s, axis=-1, keepdims=True)
            p = jnp.exp(s - mb).astype(bf16)
            vb = ld(vs, keys).astype(bf16)
            c = jnp.dot(p, jnp.concatenate([vb, jnp.ones_like(vb)], axis=1), preferred_element_type=f32)
            vals = (jnp.where(h0, mb[:blk], mb[blk:]), jnp.where(h0, c[:blk, LANES:], c[blk:, LANES:]),
                    jnp.where(h0, c[:blk, :LANES], c[blk:, :LANES]))
            for ref, val in zip(states[bi], vals):
                for b_, (cls_i, r0) in enumerate(cur):
                    ref[cls_i, r0:r0 + run, :] = val[b_ * run:(b_ + 1) * run, :]

        n_blocks = seq // blk
        scores = [score(idx) for idx in range(ATTN_SCORES_AHEAD)]
        for idx in range(n_blocks):
            if idx + ATTN_SCORES_AHEAD < n_blocks:
                scores.append(score(idx + ATTN_SCORES_AHEAD))
            finish(idx, scores[idx])
            scores[idx] = None

    for r in range(MAX_DIL):
        ms = [st_[0][r] for st_ in states]
        m = functools.reduce(jnp.maximum, ms)
        num = den = None
        for (m_ref, l_ref, a_ref), m_g in zip(states, ms):
            w = jnp.exp(m_g - m)
            num = w * a_ref[r] if num is None else num + w * a_ref[r]
            den = w * l_ref[r] if den is None else den + w * l_ref[r]
        o_ref[0, r] = num / den


def _attention(q, k, v, batch, seq):
    npair = ATTN_WIDTH // LANES
    cls = seq // MAX_DIL
    tile = pl.BlockSpec((1, MAX_DIL, cls, LANES), lambda b, p: (b, 0, 0, p))
    return pl.pallas_call(
        functools.partial(_attn_kernel, seq=seq),
        grid=(batch, npair),
        in_specs=[tile, tile, tile],
        out_specs=tile,
        out_shape=jax.ShapeDtypeStruct((batch, MAX_DIL, cls, ATTN_WIDTH), f32),
        scratch_shapes=[pltpu.VMEM((MAX_DIL, cls, LANES), f32)] * (3 * len(DILATIONS)),
        compiler_params=pltpu.CompilerParams(dimension_semantics=("arbitrary", "arbitrary"),
                                             vmem_limit_bytes=VMEM_LIMIT),
        name="dilated_attention",
    )(q, k, v)


def _out_proj_kernel(x_ref, yp_ref, yl_ref, ya_ref, og_ref, wo_ref, g2_ref, rw_ref, rb_ref,
                     xo_ref, idx_ref, gate_ref, rank_ref, cnt_ref, cnt_s, before_s, ya_s, *, tm, steps_per_tile):
    i = pl.program_id(0)
    c0, c1 = POOL_WIDTH, POOL_WIDTH + LRU_WIDTH
    for r in range(MAX_DIL):
        for p_ in range(ATTN_WIDTH // LANES):
            ya_s[p_, pl.ds(r, tm // MAX_DIL, stride=MAX_DIL), :] = ya_ref[0, r, :, p_ * LANES:(p_ + 1) * LANES]
    ya = jnp.concatenate([ya_s[p_] for p_ in range(ATTN_WIDTH // LANES)], axis=1)
    acc = x_ref[...]
    for y, lo, hi in ((yp_ref[...], 0, c0), (yl_ref[...], c0, c1), (ya, c1, D_MODEL)):
        mix = _rms(y, og_ref[:, lo:hi]).astype(bf16)
        acc = acc + jnp.dot(mix, wo_ref[lo:hi, :], preferred_element_type=f32)
    xo_ref[...] = acc

    h2 = _rms(acc, g2_ref[...])
    h_hi = h2.astype(bf16)
    h_lo = (h2 - h_hi.astype(f32)).astype(bf16)
    nt = (((1,), (1,)), ((), ()))
    both = lax.dot_general(rw_ref[...], h_hi, nt, preferred_element_type=f32)
    logits = (both[:N_EXPERTS] + both[N_EXPERTS:]
              + lax.dot_general(rw_ref[:N_EXPERTS, :], h_lo, nt, preferred_element_type=f32)
              + rb_ref[...])
    eio = lax.broadcasted_iota(i32, (N_EXPERTS, tm), 0)
    cur = logits
    vals, idxs = [], []
    for _ in range(TOP_K):
        mx = jnp.max(cur, axis=0, keepdims=True)
        ix = jnp.min(jnp.where(cur == mx, eio, N_EXPERTS), axis=0, keepdims=True)
        vals.append(mx)
        idxs.append(ix)
        cur = jnp.where(eio == ix, -jnp.inf, cur)
    ex = [jnp.exp(v - vals[0]) for v in vals]
    den = ex[0] + ex[1] + ex[2] + ex[3]

    @pl.when(i % steps_per_tile == 0)
    def _():
        cnt_s[...] = jnp.zeros_like(cnt_s)

    onehot = jnp.zeros((N_EXPERTS, tm), f32)
    for ix in idxs:
        onehot = onehot + (eio == ix).astype(f32)
    @pl.when(i == 0)
    def _():
        before_s[...] = (lax.broadcasted_iota(i32, (tm, tm), 0) < lax.broadcasted_iota(i32, (tm, tm), 1)).astype(bf16)

    prefix = jnp.dot(onehot.astype(bf16), before_s[...], preferred_element_type=f32) + cnt_s[:, 0:1]
    for k in range(TOP_K):
        idx_ref[k:k + 1, :] = idxs[k]
        gate_ref[k:k + 1, :] = ex[k] / den
        rank_ref[k:k + 1, :] = jnp.sum(jnp.where(eio == idxs[k], prefix, 0.0), axis=0, keepdims=True).astype(i32)
    cnt_s[...] = cnt_s[...] + jnp.sum(onehot, axis=1, keepdims=True)
    cnt_ref[0] = cnt_s[...]


def _out_proj(x, yp, yl, ya, og, wo, g2, rw_t, rb, seq):
    t = x.shape[0]
    tm = PROJ_TILE
    steps = t // tm
    nt = seq // tm
    tile = lambda w_: pl.BlockSpec((tm, w_), lambda i: (i, 0))
    full = lambda shape: pl.BlockSpec(shape, lambda i: (0,) * len(shape))
    route = pl.BlockSpec((TOP_K, tm), lambda i: (0, i))
    return pl.pallas_call(
        functools.partial(_out_proj_kernel, tm=tm, steps_per_tile=MOE_TILE // tm),
        grid=(steps,),
        in_specs=[tile(D_MODEL), tile(POOL_WIDTH), tile(LRU_WIDTH),
                  pl.BlockSpec((1, MAX_DIL, tm // MAX_DIL, ATTN_WIDTH), lambda i: (i // nt, 0, i % nt, 0)),
                  full((1, D_MODEL)), full((D_MODEL, D_MODEL)), full((1, D_MODEL)),
                  full((2 * N_EXPERTS, D_MODEL)), full((N_EXPERTS, 1))],
        out_specs=[tile(D_MODEL), route, route, route,
                   pl.BlockSpec((1, N_EXPERTS, LANES), lambda i: (i, 0, 0))],
        out_shape=[jax.ShapeDtypeStruct((t, D_MODEL), f32),
                   jax.ShapeDtypeStruct((TOP_K, t), i32),
                   jax.ShapeDtypeStruct((TOP_K, t), f32),
                   jax.ShapeDtypeStruct((TOP_K, t), i32),
                   jax.ShapeDtypeStruct((steps, N_EXPERTS, LANES), f32)],
        scratch_shapes=[pltpu.VMEM((N_EXPERTS, LANES), f32), pltpu.VMEM((tm, tm), bf16),
                        pltpu.VMEM((ATTN_WIDTH // LANES, tm, LANES), f32)],
        compiler_params=pltpu.CompilerParams(dimension_semantics=("arbitrary",),
                                             vmem_limit_bytes=VMEM_LIMIT),
        name="out_proj_router",
    )(x, yp, yl, ya, og, wo, g2, rw_t, rb)


ROW_SLAB = 8
ROUTE_GROUP = 8
MOE_EXPERTS_PER_STEP = 2


def _slab_cols(ref, first_row, n_rows, col):
    return ref[pl.ds(first_row * ROW_SLAB + col, n_rows, stride=ROW_SLAB), :]


def _moe_kernel(off_ref, pos_ref, gate_ref, x_ref, g2_ref, wgu_ref, bgu_ref, wd_ref, bd_ref,
                o_ref, sorted_s, stage_s, *, tt, ch, io_rows):
    i = pl.program_id(0)
    s = pl.program_id(1)
    n_io = tt // io_rows
    n_exp_steps = N_EXPERTS // MOE_EXPERTS_PER_STEP
    base = i * (N_EXPERTS + 1)
    n_col = D_MODEL // LANES

    def slab(ref, first_sublane):
        return ref.at[pl.ds(pl.multiple_of(first_sublane, ROW_SLAB), ROW_SLAB), :]

    @pl.when(s < n_io)
    def _dispatch():
        @pl.when(s == 0)
        def _():
            sorted_s[pl.ds(TOP_K * tt * ROW_SLAB, ch * ROW_SLAB), :] = jnp.zeros((ch * ROW_SLAB, LANES), f32)

        h = _rms(x_ref[...], g2_ref[...])
        for j in range(n_col):
            stage_s[pl.ds(j, io_rows, stride=ROW_SLAB), :] = h[:, j * LANES:(j + 1) * LANES]

        for c in range(io_rows // ROUTE_GROUP):
            for u_ in range(ROUTE_GROUP):
                row = stage_s[(c * ROUTE_GROUP + u_) * ROW_SLAB:(c * ROUTE_GROUP + u_ + 1) * ROW_SLAB, :]
                for k in range(TOP_K):
                    slab(sorted_s, pos_ref[k, c * ROUTE_GROUP + u_])[...] = row

    @pl.when((s >= n_io) & (s < n_io + n_exp_steps))
    def _experts():
        def run_chunk(sub, start, hi, rows):
            xc = jnp.concatenate([_slab_cols(sorted_s, start, rows, j) for j in range(n_col)], axis=1)
            gu = jnp.dot(xc.astype(bf16), wgu_ref[sub], preferred_element_type=f32) + bgu_ref[sub]
            gate = jnp.minimum(gu[:, :EXPERT_FF], SWIGLU_LIMIT)
            up = jnp.clip(gu[:, EXPERT_FF:], -SWIGLU_LIMIT, SWIGLU_LIMIT)
            act = (up + 1.0) * gate * jax.nn.sigmoid(SWIGLU_ALPHA * gate)
            y = jnp.dot(act.astype(bf16), wd_ref[sub], preferred_element_type=f32) + bd_ref[sub]
            mine = (lax.broadcasted_iota(i32, (rows, 1), 0) + start) < hi
            y = jnp.where(mine, y, xc)
            for j in range(n_col):
                sorted_s[pl.ds(start * ROW_SLAB + j, rows, stride=ROW_SLAB), :] = y[:, j * LANES:(j + 1) * LANES]

        def expert(sub, carry):
            e = (s - n_io) * MOE_EXPERTS_PER_STEP + sub
            lo = off_ref[base + e]
            hi = off_ref[base + e + 1]
            n_full = jnp.maximum(hi - lo - 1, 0) // ch

            def full(c, carry2):
                run_chunk(sub, lo + c * ch, hi, ch)
                return carry2

            lax.fori_loop(0, n_full, full, 0)
            start = lo + n_full * ch
            rest = hi - start
            for size_lo, size in zip((0,) + MOE_TAIL_SIZES[:-1], MOE_TAIL_SIZES):
                @pl.when((rest > size_lo) & (rest <= size))
                def _(size=size):
                    run_chunk(sub, start, hi, size)
            return carry

        lax.fori_loop(0, MOE_EXPERTS_PER_STEP, expert, 0)

    @pl.when(s >= n_io + n_exp_steps)
    def _combine():
        for c in range(io_rows // ROUTE_GROUP):
            for u_ in range(ROUTE_GROUP):
                acc = gate_ref[0, c * ROUTE_GROUP + u_] * slab(sorted_s, pos_ref[0, c * ROUTE_GROUP + u_])[...]
                for k in range(1, TOP_K):
                    acc = acc + gate_ref[k, c * ROUTE_GROUP + u_] * slab(sorted_s, pos_ref[k, c * ROUTE_GROUP + u_])[...]
                stage_s[(c * ROUTE_GROUP + u_) * ROW_SLAB:(c * ROUTE_GROUP + u_ + 1) * ROW_SLAB, :] = acc
        moe = jnp.concatenate([_slab_cols(stage_s, 0, io_rows, j) for j in range(n_col)], axis=1)
        o_ref[...] = x_ref[...] + moe


def _moe(x, pos, gates, offsets, g2, wgu, bgu, wd, bd):
    t = x.shape[0]
    tt = MOE_TILE
    ch = MOE_CHUNK
    io_rows = PROJ_TILE
    n_io = tt // io_rows
    eps = MOE_EXPERTS_PER_STEP
    n_exp_steps = N_EXPERTS // eps
    n_steps = 2 * n_io + n_exp_steps

    def tok_idx(i, s, off):
        in_combine = s >= n_io + n_exp_steps
        return i * n_io + jnp.where(s < n_io, s, jnp.where(in_combine, s - n_io - n_exp_steps, n_io - 1))

    def out_idx(i, s, off):
        return i * n_io + jnp.maximum(s - n_io - n_exp_steps, 0)

    def expert_idx(i, s, off):
        return jnp.clip(s - n_io, 0, n_exp_steps - 1)

    route = pl.BlockSpec((TOP_K, io_rows), lambda i, s, off: (0, tok_idx(i, s, off)), memory_space=pltpu.SMEM)
    return pl.pallas_call(
        functools.partial(_moe_kernel, tt=tt, ch=ch, io_rows=io_rows),
        grid_spec=pltpu.PrefetchScalarGridSpec(
            num_scalar_prefetch=1,
            grid=(t // tt, n_steps),
            in_specs=[route, route,
                      pl.BlockSpec((io_rows, D_MODEL), lambda i, s, off: (tok_idx(i, s, off), 0)),
                      pl.BlockSpec((1, D_MODEL), lambda i, s, off: (0, 0)),
                      pl.BlockSpec((eps, D_MODEL, 2 * EXPERT_FF), lambda i, s, off: (expert_idx(i, s, off), 0, 0)),
                      pl.BlockSpec((eps, 1, 2 * EXPERT_FF), lambda i, s, off: (expert_idx(i, s, off), 0, 0)),
                      pl.BlockSpec((eps, EXPERT_FF, D_MODEL), lambda i, s, off: (expert_idx(i, s, off), 0, 0)),
                      pl.BlockSpec((eps, 1, D_MODEL), lambda i, s, off: (expert_idx(i, s, off), 0, 0))],
            out_specs=pl.BlockSpec((io_rows, D_MODEL), lambda i, s, off: (out_idx(i, s, off), 0)),
            scratch_shapes=[pltpu.VMEM(((TOP_K * tt + ch) * ROW_SLAB, LANES), f32),
                            pltpu.VMEM((io_rows * ROW_SLAB, LANES), f32)]),
        out_shape=jax.ShapeDtypeStruct((t, D_MODEL), f32),
        compiler_params=pltpu.CompilerParams(dimension_semantics=("arbitrary", "arbitrary"),
                                             vmem_limit_bytes=VMEM_LIMIT),
        name="moe_experts",
    )(offsets, pos, gates, x, g2, wgu, bgu, wd, bd)


def _pair_layout(v):
    first, second = v[..., :QK_QUARTER], v[..., QK_QUARTER:]
    return jnp.concatenate([first, first, second, second], axis=-1)


def _qk_column_order():
    pair = jnp.arange(LANES).reshape(2, 2, QK_QUARTER).transpose(1, 0, 2).reshape(-1)
    cols = jnp.arange(_IN_SPLITS[-1])
    qk = cols[_IN_SPLITS[3]:_IN_SPLITS[5]].reshape(-1, LANES)[:, pair].reshape(-1)
    return jnp.concatenate([cols[:_IN_SPLITS[3]], qk, cols[_IN_SPLITS[5]:]])


def _rope_tables(seq):
    pos = jnp.arange(seq, dtype=f32)
    inv_freq = ROPE_THETA ** (-jnp.arange(0, HEAD_DIM, 2, dtype=f32) / HEAD_DIM)
    ang = pos[:, None] * inv_freq[None, :]
    cos, sin = jnp.cos(ang), jnp.sin(ang)
    return _pair_layout(jnp.concatenate([cos, cos], axis=1)), _pair_layout(jnp.concatenate([-sin, sin], axis=1))


def _layer(x, cos_t, sin_t, batch, seq, p):
    t = batch * seq
    row = lambda v: v.reshape(1, -1)
    pool_u, lru_u, gate, q, k, v = _in_proj(x, row(p["norm1_g"]), p["w_in"][:, _qk_column_order()].astype(bf16),
                                            cos_t, sin_t, row(_pair_layout(p["q_norm_g"])),
                                            row(_pair_layout(p["k_norm_g"])), batch, seq)
    wri = jnp.concatenate([block_diag(*p["lru_wa"]), block_diag(*p["lru_wx"])], axis=1).astype(bf16)
    bri = jnp.concatenate([p["lru_ba"], p["lru_bx"]]).reshape(1, -1)
    y_pool, y_lru = _mixer(pool_u, lru_u, gate, block_diag(*p["pool_w"]).astype(bf16), row(p["pool_scale"]),
                           p["conv_w"], row(p["conv_b"]), wri, bri, row(p["lru_lambda"]), batch, seq)
    y_attn = _attention(q, k, v, batch, seq)
    rw_hi = p["router_w"].T.astype(bf16)
    rw_split = jnp.concatenate([rw_hi, (p["router_w"].T - rw_hi.astype(f32)).astype(bf16)], axis=0)
    x, idx, gates, rank, cnt = _out_proj(x, y_pool, y_lru, y_attn, row(p["out_norm_g"]), p["w_out"].astype(bf16),
                                         row(p["norm2_g"]), rw_split, p["router_b"].reshape(-1, 1), seq)
    steps_per_tile = MOE_TILE // PROJ_TILE
    n_tiles = t // MOE_TILE
    counts = cnt[steps_per_tile - 1::steps_per_tile, :, 0].astype(i32)
    offsets = jnp.concatenate([jnp.zeros((n_tiles, 1), i32), jnp.cumsum(counts, axis=1)], axis=1)
    idx_t = idx.reshape(TOP_K, n_tiles, MOE_TILE)
    seg_start = jnp.zeros_like(idx_t)
    for e in range(N_EXPERTS):
        seg_start = jnp.where(idx_t == e, offsets[None, :, e, None], seg_start)
    pos = (rank + seg_start.reshape(TOP_K, t)) * ROW_SLAB
    return _moe(x, pos, gates, offsets.reshape(-1), row(p["norm2_g"]),
                p["w_gate_up"].astype(bf16), p["b_gate_up"][:, None, :],
                p["w_down"].astype(bf16), p["b_down"][:, None, :])


def kernel(x, norm1_g, w_in, pool_w, pool_scale, conv_w, conv_b, lru_wa, lru_ba, lru_wx, lru_bx, lru_lambda,
           q_norm_g, k_norm_g, out_norm_g, w_out, norm2_g, router_w, router_b, w_gate_up, b_gate_up,
           w_down, b_down):
    batch, seq, d = x.shape
    params = dict(norm1_g=norm1_g, w_in=w_in, pool_w=pool_w, pool_scale=pool_scale, conv_w=conv_w, conv_b=conv_b,
                  lru_wa=lru_wa, lru_ba=lru_ba, lru_wx=lru_wx, lru_bx=lru_bx, lru_lambda=lru_lambda,
                  q_norm_g=q_norm_g, k_norm_g=k_norm_g, out_norm_g=out_norm_g, w_out=w_out, norm2_g=norm2_g,
                  router_w=router_w, router_b=router_b, w_gate_up=w_gate_up, b_gate_up=b_gate_up,
                  w_down=w_down, b_down=b_down)
    cos_t, sin_t = _rope_tables(seq)
    xt = x.reshape(batch * seq, d)
    for layer in range(norm1_g.shape[0]):
        xt = _layer(xt, cos_t, sin_t, batch, seq, {name: val[layer] for name, val in params.items()})
    return xt.reshape(batch, seq, d)
```
